```python
import jax, jax.numpy as jnp
from jax import lax
import numpy as np

D_MODEL = 1024
BATCH = 8
SEQ = 4096
DEPTH = 4

GRID_W = 64
CTX_LEN = 256
N_MIXERS = 3
N_A = (DEPTH + 2) // 3
N_B = (DEPTH + 1) // 3
N_C = DEPTH // 3
ATTN_HEADS = 8
ATTN_HD = D_MODEL // (2 * ATTN_HEADS)
ROPE_FREQS = ATTN_HD // 4
ROPE_BASE = 10000.0
Q_BLOCK = 128
FNET_GROUPS = 8
FNET_GW = D_MODEL // FNET_GROUPS
CONV_WIDTH = 31
CONV_PAD = CONV_WIDTH // 2
N_EXPERTS = 32
TOP_K = 4
D_FF = D_MODEL
SWIGLU_ALPHA = 1.702
SWIGLU_LIMIT = 7.0
MOE_BLOCK = 128
DEEPNORM_ALPHA = (2 * DEPTH) ** 0.25
DEEPNORM_BETA = (8 * DEPTH) ** -0.25
LN_EPS = 1e-5

kernel_name = "hybrid_diffattn_fnet_conformer_moe_dit"


def layer_norm(x, g, b):
    xf = x.astype(jnp.float32)
    mu = jnp.mean(xf, axis=-1, keepdims=True)
    var = jnp.mean(jnp.square(xf - mu), axis=-1, keepdims=True)
    return ((xf - mu) * lax.rsqrt(var + LN_EPS) * g + b).astype(x.dtype)


def axial_rope_tables(n):
    rows_count = n // GRID_W
    rows = jnp.repeat(jnp.arange(rows_count), GRID_W).astype(jnp.float32)
    cols = jnp.tile(jnp.arange(GRID_W), rows_count).astype(jnp.float32)
    inv_freq = ROPE_BASE ** (-jnp.arange(ROPE_FREQS, dtype=jnp.float32) / ROPE_FREQS)
    ang_r = rows[:, None] * inv_freq
    ang_c = cols[:, None] * inv_freq
    shp = (1, n, 1, 1, ROPE_FREQS)
    return (jnp.cos(ang_r).reshape(shp), jnp.sin(ang_r).reshape(shp),
            jnp.cos(ang_c).reshape(shp), jnp.sin(ang_c).reshape(shp))


def rotate_half(xp, cos, sin):
    x1, x2 = jnp.split(xp, 2, axis=-1)
    return jnp.concatenate([x1 * cos - x2 * sin, x2 * cos + x1 * sin], axis=-1)


def axial_rope(x, tables):
    cr, sr, cc, sc = tables
    half = ATTN_HD // 2
    xr = rotate_half(x[..., :half], cr, sr)
    xc = rotate_half(x[..., half:], cc, sc)
    return jnp.concatenate([xr, xc], axis=-1).astype(x.dtype)


def diff_core(q, k, v, lam):
    s = jnp.einsum('bqhcd,bkhcd->cbhqk', q, k, preferred_element_type=jnp.float32) * (ATTN_HD ** -0.5)
    p = jax.nn.softmax(s, axis=-1)
    attn = p[0] - lam * p[1]
    return jnp.einsum('bhqk,bkhv->bqhv', attn.astype(v.dtype), v)


def diff_head_out(o, subln_g, lam_init, w_o):
    of = o.astype(jnp.float32)
    of = of * lax.rsqrt(jnp.mean(jnp.square(of), axis=-1, keepdims=True) + LN_EPS) * subln_g * (1.0 - lam_init)
    B, N = o.shape[0], o.shape[1]
    return of.astype(o.dtype).reshape(B, N, D_MODEL) @ w_o


def diff_attention(u_lat, u_ctx, w_qkv, w_o, lq1, lk1, lq2, lk2, subln_g, lam_init, tables, with_ctx_out):
    B, N, _ = u_lat.shape

    def project(u):
        q, k, v = jnp.split(u @ w_qkv, 3, axis=-1)
        n = u.shape[1]
        return (q.reshape(B, n, ATTN_HEADS, 2, ATTN_HD),
                k.reshape(B, n, ATTN_HEADS, 2, ATTN_HD),
                v.reshape(B, n, ATTN_HEADS, 2 * ATTN_HD))

    q_l, k_l, v_l = project(u_lat)
    q_c, k_c, v_c = project(u_ctx)
    q_l = axial_rope(q_l, tables)
    k_l = axial_rope(k_l, tables)
    lam = (jnp.exp(jnp.sum(lq1 * lk1).astype(jnp.float32))
           - jnp.exp(jnp.sum(lq2 * lk2).astype(jnp.float32)) + lam_init)

    k_all = jnp.concatenate([k_l, k_c], axis=1)
    v_all = jnp.concatenate([v_l, v_c], axis=1)
    nb = N // Q_BLOCK
    q_blocks = q_l.reshape(B, nb, Q_BLOCK, ATTN_HEADS, 2, ATTN_HD).transpose(1, 0, 2, 3, 4, 5)
    o_l = lax.map(lambda qb: diff_core(qb, k_all, v_all, lam), q_blocks)
    o_l = o_l.transpose(1, 0, 2, 3, 4).reshape(B, N, ATTN_HEADS, 2 * ATTN_HD)
    y_lat = diff_head_out(o_l, subln_g, lam_init, w_o)
    y_ctx = None
    if with_ctx_out:
        y_ctx = diff_head_out(diff_core(q_c, k_c, v_c, lam), subln_g, lam_init, w_o)
    return y_lat, y_ctx


def fourier_mix(u, w, b):
    B, N, _ = u.shape
    ug = u.astype(jnp.float32).reshape(B, N, FNET_GROUPS, FNET_GW)
    f = jnp.fft.fft2(ug, axes=(1, 3), norm='ortho').real
    return f.reshape(B, N, D_MODEL).astype(u.dtype) @ w + b


def conformer_conv(u, w_pw1, b_pw1, w_dw, b_dw, ln_g, ln_b, w_pw2, b_pw2):
    h = u @ w_pw1 + b_pw1
    a, g = jnp.split(h, 2, axis=-1)
    h = a * jax.nn.sigmoid(g)
    h = lax.conv_general_dilated(h, w_dw[:, None, :].astype(h.dtype), window_strides=(1,),
                                 padding=[(CONV_PAD, CONV_PAD)],
                                 dimension_numbers=('NWC', 'WIO', 'NWC'),
                                 feature_group_count=D_MODEL) + b_dw
    h = jax.nn.silu(layer_norm(h, ln_g, ln_b))
    return h @ w_pw2 + b_pw2


def clamped_swiglu(h):
    glu, lin = jnp.split(h, 2, axis=-1)
    glu = jnp.minimum(glu, SWIGLU_LIMIT)
    lin = jnp.clip(lin, -SWIGLU_LIMIT, SWIGLU_LIMIT)
    return glu * jax.nn.sigmoid(SWIGLU_ALPHA * glu) * (lin + 1.0)


def moe(tokens, w_router, b_router, w1, b1, w2, b2):
    T = tokens.shape[0]
    logits = (tokens @ w_router + b_router).astype(jnp.float32)
    top_val, top_idx = lax.top_k(logits, TOP_K)
    gates = jax.nn.softmax(top_val, axis=-1)
    A = T * TOP_K
    flat_e = top_idx.reshape(-1)
    order = jnp.argsort(flat_e)
    sorted_e = flat_e[order]
    counts = jnp.bincount(flat_e, length=N_EXPERTS)
    padded = (counts + MOE_BLOCK - 1) // MOE_BLOCK * MOE_BLOCK
    start = jnp.cumsum(counts) - counts
    pend = jnp.cumsum(padded)
    pstart = pend - padded
    dest = pstart[sorted_e] + (jnp.arange(A) - start[sorted_e])
    n_blocks = -(-A // MOE_BLOCK) + N_EXPERTS
    P = n_blocks * MOE_BLOCK
    row_tok = jnp.zeros((P,), jnp.int32).at[dest].set((order // TOP_K).astype(jnp.int32))
    row_gate = jnp.zeros((P,), jnp.float32).at[dest].set(gates.reshape(-1)[order])
    block_e = jnp.minimum(jnp.searchsorted(pend, jnp.arange(n_blocks) * MOE_BLOCK, side='right'),
                          N_EXPERTS - 1)
    xb = tokens[row_tok].reshape(n_blocks, MOE_BLOCK, D_MODEL)

    def expert_block(args):
        xe, e = args
        h = clamped_swiglu(xe @ w1[e] + b1[e])
        return h @ w2[e] + b2[e]

    yb = lax.map(expert_block, (xb, block_e)).reshape(P, D_MODEL)
    return jnp.zeros_like(tokens).at[row_tok].add(yb * row_gate[:, None].astype(tokens.dtype))


def setup_inputs(seed: int = 0) -> dict:
    key = jax.random.key(seed)
    keys = iter(jax.random.split(key, 48))

    def nrm(shape, scale):
        return jax.random.normal(next(keys), shape, jnp.float32) * scale

    D, E, F, HD = D_MODEL, N_EXPERTS, D_FF, ATTN_HD
    din = D ** -0.5
    return {
        "x": nrm((BATCH, SEQ, D), 1.0),
        "c": nrm((BATCH, D), 1.0),
        "ctx": nrm((BATCH, CTX_LEN, D), 1.0),
        "c_ctx": nrm((D,), 1.0),
        "w_mod": nrm((DEPTH, D, 6 * D), 0.5 * din),
        "b_mod": nrm((DEPTH, 6 * D), 0.02),
        "ln1_g": 1.0 + nrm((DEPTH, D), 0.02),
        "ln1_b": nrm((DEPTH, D), 0.02),
        "ln2_g": 1.0 + nrm((DEPTH, D), 0.02),
        "ln2_b": nrm((DEPTH, D), 0.02),
        "attn_w_qkv": nrm((N_A, D, 3 * D), din),
        "attn_w_o": nrm((N_A, D, D), din * DEEPNORM_BETA),
        "attn_lam_q1": nrm((N_A, HD), 0.1),
        "attn_lam_k1": nrm((N_A, HD), 0.1),
        "attn_lam_q2": nrm((N_A, HD), 0.1),
        "attn_lam_k2": nrm((N_A, HD), 0.1),
        "attn_subln_g": 1.0 + nrm((N_A, 2 * HD), 0.02),
        "fnet_w": nrm((N_B, D, D), din * DEEPNORM_BETA),
        "fnet_b": nrm((N_B, D), 0.02),
        "conv_w_pw1": nrm((N_C, D, 2 * D), din),
        "conv_b_pw1": nrm((N_C, 2 * D), 0.02),
        "conv_w_dw": nrm((N_C, CONV_WIDTH, D), CONV_WIDTH ** -0.5),
        "conv_b_dw": nrm((N_C, D), 0.02),
        "conv_ln_g": 1.0 + nrm((N_C, D), 0.02),
        "conv_ln_b": nrm((N_C, D), 0.02),
        "conv_w_pw2": nrm((N_C, D, D), din * DEEPNORM_BETA),
        "conv_b_pw2": nrm((N_C, D), 0.02),
        "moe_w_router": nrm((DEPTH, D, E), din),
        "moe_b_router": nrm((DEPTH, E), 0.01),
        "moe_w1": nrm((DEPTH, E, D, 2 * F), din),
        "moe_b1": nrm((DEPTH, E, 2 * F), 0.02),
        "moe_w2": nrm((DEPTH, E, F, D), F ** -0.5 * DEEPNORM_BETA),
        "moe_b2": nrm((DEPTH, E, D), 0.02),
    }


def reference(x, c, ctx, c_ctx, w_mod, b_mod, ln1_g, ln1_b, ln2_g, ln2_b,
              attn_w_qkv, attn_w_o, attn_lam_q1, attn_lam_k1, attn_lam_q2, attn_lam_k2, attn_subln_g,
              fnet_w, fnet_b,
              conv_w_pw1, conv_b_pw1, conv_w_dw, conv_b_dw, conv_ln_g, conv_ln_b, conv_w_pw2, conv_b_pw2,
              moe_w_router, moe_b_router, moe_w1, moe_b1, moe_w2, moe_b2):
    B, N, D = x.shape
    C = ctx.shape[1]
    tables = axial_rope_tables(N)
    cond_lat = jax.nn.silu(c)
    cond_ctx = jax.nn.silu(c_ctx)
    x_lat, x_ctx = x, ctx

    for i in range(DEPTH):
        last = i == DEPTH - 1
        kind = i % N_MIXERS
        j = i // N_MIXERS
        m_lat = cond_lat @ w_mod[i] + b_mod[i]
        m_ctx = cond_ctx @ w_mod[i] + b_mod[i]
        sh1, sc1, g1, sh2, sc2, g2 = jnp.split(m_lat[:, None, :], 6, axis=-1)
        ch1, cs1, cg1, ch2, cs2, cg2 = jnp.split(m_ctx, 6)

        u_lat = x_lat * (1.0 + sc1) + sh1
        need_ctx = (not last) or kind == 0
        u_ctx = x_ctx * (1.0 + cs1) + ch1 if need_ctx else None
        if kind == 0:
            lam_init = 0.8 - 0.6 * float(np.exp(-0.3 * i))
            y_lat, y_ctx = diff_attention(u_lat, u_ctx, attn_w_qkv[j], attn_w_o[j],
                                          attn_lam_q1[j], attn_lam_k1[j], attn_lam_q2[j], attn_lam_k2[j],
                                          attn_subln_g[j], lam_init, tables, with_ctx_out=not last)
        elif kind == 1:
            y_lat = fourier_mix(u_lat, fnet_w[j], fnet_b[j])
            y_ctx = None if last else fourier_mix(u_ctx, fnet_w[j], fnet_b[j])
        else:
            cw = (conv_w_pw1[j], conv_b_pw1[j], conv_w_dw[j], conv_b_dw[j],
                  conv_ln_g[j], conv_ln_b[j], conv_w_pw2[j], conv_b_pw2[j])
            y_lat = conformer_conv(u_lat, *cw)
            y_ctx = None if last else conformer_conv(u_ctx, *cw)
        x_lat = layer_norm(DEEPNORM_ALPHA * x_lat + g1 * y_lat, ln1_g[i], ln1_b[i])
        if not last:
            x_ctx = layer_norm(DEEPNORM_ALPHA * x_ctx + cg1 * y_ctx, ln1_g[i], ln1_b[i])

        mw = (moe_w_router[i], moe_b_router[i], moe_w1[i], moe_b1[i], moe_w2[i], moe_b2[i])
        v_lat = (x_lat * (1.0 + sc2) + sh2).reshape(B * N, D)
        if last:
            f_lat = moe(v_lat, *mw).reshape(B, N, D)
        else:
            v_ctx = (x_ctx * (1.0 + cs2) + ch2).reshape(B * C, D)
            f_all = moe(jnp.concatenate([v_lat, v_ctx], axis=0), *mw)
            f_lat = f_all[:B * N].reshape(B, N, D)
            f_ctx = f_all[B * N:].reshape(B, C, D)
            x_ctx = layer_norm(DEEPNORM_ALPHA * x_ctx + cg2 * f_ctx, ln2_g[i], ln2_b[i])
        x_lat = layer_norm(DEEPNORM_ALPHA * x_lat + g2 * f_lat, ln2_g[i], ln2_b[i])

    return x_lat
```

```python
import functools

import jax
import jax.numpy as jnp
import numpy as np
from jax import lax
from jax.experimental import pallas as pl
from jax.experimental.pallas import tpu as pltpu

F32 = jnp.float32
BF16 = jnp.bfloat16
HIGHEST = lax.Precision.HIGHEST

GRID_W = 64
HEAD_W = 128
MAP_W = HEAD_W // 2
ROPE_FREQS = MAP_W // 4
ROPE_BASE = 10000.0
CONV_WIDTH = 31
CONV_PAD = CONV_WIDTH // 2
TOP_K = 4
SWIGLU_ALPHA = 1.702
SWIGLU_LIMIT = 7.0
LN_EPS = 1e-5

LANES = 128
ROW_BLOCK = 256
HALO = 16
ATTN_TQ = 512
ATTN_TK = 256
DFT_TILE = 1024
EXPERT_BLOCK = 512
DISPATCH_BLOCK = 512
VMEM_LIMIT = 56 * 1024 * 1024


def _cparams(*sem):
    return pltpu.CompilerParams(dimension_semantics=sem, vmem_limit_bytes=VMEM_LIMIT)


def _mod_row_map(nl, nc):
    rb = nl + nc

    def f(r):
        return 2 * (r // rb) + jnp.where((r % rb) >= nl, 1, 0)

    return f


def _layer_norm(z, g, b):
    mu = jnp.mean(z, axis=-1, keepdims=True)
    zc = z - mu
    var = jnp.mean(zc * zc, axis=-1, keepdims=True)
    return zc * lax.rsqrt(var + LN_EPS) * g + b


def _sigmoid(x):
    return 1.0 / (1.0 + jnp.exp(-x))


def _modulation_kernel(cond_ref, w_ref, b_ref, o_ref):
    cnd = cond_ref[...]
    s = cnd * _sigmoid(cnd)
    o_ref[...] = jnp.dot(s, w_ref[...], precision=HIGHEST, preferred_element_type=F32) + b_ref[...]


def _modulation(cond, w_mod, b_mod):
    depth, d, d6 = w_mod.shape
    rows = cond.shape[0]
    out = pl.pallas_call(
        _modulation_kernel,
        grid=(depth, d6 // d),
        in_specs=[
            pl.BlockSpec((rows, d), lambda i, j: (0, 0)),
            pl.BlockSpec((None, d, d), lambda i, j: (i, 0, j)),
            pl.BlockSpec((None, 1, d), lambda i, j: (i, 0, j)),
        ],
        out_specs=pl.BlockSpec((None, rows, d), lambda i, j: (i, 0, j)),
        out_shape=jax.ShapeDtypeStruct((depth, rows, d6), F32),
        compiler_params=_cparams("parallel", "parallel"),
        name="modulation",
    )(cond, w_mod, b_mod.reshape(depth, 1, d6))
    return out.reshape(depth, rows, d6 // d, d)


def _rope_tables(n, c):
    pos = jnp.arange(n)
    rows = (pos // GRID_W).astype(F32)
    cols = (pos % GRID_W).astype(F32)
    inv_freq = ROPE_BASE ** (-jnp.arange(ROPE_FREQS, dtype=F32) / ROPE_FREQS)
    lane = np.arange(HEAD_W)
    dim = lane % MAP_W
    freq = dim % ROPE_FREQS
    use_row = (dim // (MAP_W // 2)) == 0
    first_half = (dim % (MAP_W // 2)) < ROPE_FREQS
    ang = jnp.where(use_row[None, :], rows[:, None] * inv_freq[freq][None, :],
                    cols[:, None] * inv_freq[freq][None, :])
    cos = jnp.cos(ang)
    sin = jnp.sin(ang)
    sin = jnp.where(first_half[None, :], -sin, sin)
    cos = jnp.concatenate([cos, jnp.ones((c, HEAD_W), F32)], axis=0)
    sin = jnp.concatenate([sin, jnp.zeros((c, HEAD_W), F32)], axis=0)
    return cos, sin


def _qkv_kernel(x_ref, mod_ref, w_ref, cos_ref, sin_ref, q_ref, k_ref, v_ref):
    d = x_ref.shape[1]
    u = (x_ref[...] * (1.0 + mod_ref[1:2, :]) + mod_ref[0:1, :]).astype(BF16)
    cos = cos_ref[...]
    sin = sin_ref[...]
    lane = lax.broadcasted_iota(jnp.int32, cos.shape, 1)
    first_half = (lane % (MAP_W // 2)) < ROPE_FREQS

    def rope(t):
        partner = jnp.where(first_half, pltpu.roll(t, HEAD_W - ROPE_FREQS, 1), pltpu.roll(t, ROPE_FREQS, 1))
        return t * cos + partner * sin

    q = jnp.dot(u, w_ref[:, :d], preferred_element_type=F32)
    k = jnp.dot(u, w_ref[:, d:2 * d], preferred_element_type=F32)
    for h in range(d // HEAD_W):
        cs = slice(h * HEAD_W, (h + 1) * HEAD_W)
        q_ref[:, cs] = rope(q[:, cs]).astype(BF16)
        k_ref[:, cs] = rope(k[:, cs]).astype(BF16)
    v_ref[...] = jnp.dot(u, w_ref[:, 2 * d:], preferred_element_type=F32).astype(BF16)


def _qkv(x, mod, w_qkv, cos, sin, nl, nc):
    t, d = x.shape
    rb = nl + nc
    tm = ROW_BLOCK
    scale = jnp.concatenate([jnp.full((d,), MAP_W ** -0.5, F32), jnp.ones((2 * d,), F32)])
    w = (w_qkv * scale[None, :]).astype(BF16)
    row_spec = pl.BlockSpec((tm, d), lambda r: (r, 0))
    tab_spec = pl.BlockSpec((tm, HEAD_W), lambda r: (r % rb, 0))
    return pl.pallas_call(
        _qkv_kernel,
        grid=(t // tm,),
        in_specs=[
            row_spec,
            pl.BlockSpec((None, 6, d), lambda r, f=_mod_row_map(nl, nc): (f(r), 0, 0)),
            pl.BlockSpec((d, 3 * d), lambda r: (0, 0)),
            tab_spec, tab_spec,
        ],
        out_specs=[row_spec, row_spec, row_spec],
        out_shape=[jax.ShapeDtypeStruct((t, d), BF16)] * 3,
        compiler_params=_cparams("parallel"),
        name="qkv_rope",
    )(x, mod, w, cos, sin)


def _attn_kernel(lam_init, tk, q_ref, k_ref, vt_ref, lq1_ref, lk1_ref, lq2_ref, lk2_ref, g_ref,
                 o_ref, acc1_ref, acc2_ref):
    tq = q_ref.shape[0]
    n_kv = k_ref.shape[0] // tk
    q = q_ref[...].astype(F32)
    lane = lax.broadcasted_iota(jnp.int32, q.shape, 1)
    q1 = jnp.where(lane < MAP_W, q, 0.0).astype(BF16)
    q2 = jnp.where(lane >= MAP_W, q, 0.0).astype(BF16)
    acc1_ref[...] = jnp.zeros_like(acc1_ref)
    acc2_ref[...] = jnp.zeros_like(acc2_ref)

    def one_map(k, vt, qm, m, l, acc_ref):
        s = lax.dot_general(k, qm, (((1,), (1,)), ((), ())), preferred_element_type=F32)
        m_new = jnp.maximum(m, jnp.max(s, axis=0, keepdims=True))
        a = jnp.exp(m - m_new)
        p = jnp.exp(s - m_new)
        l = a * l + jnp.sum(p, axis=0, keepdims=True)
        acc_ref[...] = a * acc_ref[...] + jnp.dot(vt, p.astype(BF16), preferred_element_type=F32)
        return m_new, l

    def step(i, carry):
        m1, l1, m2, l2 = carry
        off = pl.multiple_of(i * tk, tk)
        k = k_ref[pl.ds(off, tk), :]
        vt = vt_ref[:, pl.ds(off, tk)]
        m1, l1 = one_map(k, vt, q1, m1, l1, acc1_ref)
        m2, l2 = one_map(k, vt, q2, m2, l2, acc2_ref)
        return m1, l1, m2, l2

    neg = jnp.full((1, tq), -jnp.inf, F32)
    zero = jnp.zeros((1, tq), F32)
    _, l1, _, l2 = lax.fori_loop(0, n_kv, step, (neg, zero, neg, zero))

    lam = (jnp.exp(jnp.sum(lq1_ref[...] * lk1_ref[...], axis=1, keepdims=True))
           - jnp.exp(jnp.sum(lq2_ref[...] * lk2_ref[...], axis=1, keepdims=True)) + lam_init)
    o = acc1_ref[...] * (1.0 / l1) - lam * (acc2_ref[...] * (1.0 / l2))
    ms = jnp.mean(o * o, axis=0, keepdims=True)
    o = o * lax.rsqrt(ms + LN_EPS) * (g_ref[...] * (1.0 - lam_init))
    o_ref[...] = o.T.astype(BF16)


def _attention(q3, k3, vt, lam_params, subln_g, lam_init, n, c, context):
    b, s, d = q3.shape
    h = d // HEAD_W
    if context:
        tq, kv_len, kv_blk, q_off, nq = c, c, n // c, n // c, 1
    else:
        tq, kv_len, kv_blk, q_off, nq = ATTN_TQ, s, 0, 0, n // ATTN_TQ
    tk = min(ATTN_TK, kv_len)
    lam_spec = pl.BlockSpec((1, MAP_W), lambda bi, hi, qi: (0, 0))
    return pl.pallas_call(
        functools.partial(_attn_kernel, lam_init, tk),
        grid=(b, h, nq),
        in_specs=[
            pl.BlockSpec((None, tq, HEAD_W), lambda bi, hi, qi: (bi, qi + q_off, hi)),
            pl.BlockSpec((None, kv_len, HEAD_W), lambda bi, hi, qi: (bi, kv_blk, hi)),
            pl.BlockSpec((None, None, HEAD_W, kv_len), lambda bi, hi, qi: (bi, hi, 0, kv_blk)),
            lam_spec, lam_spec, lam_spec, lam_spec,
            pl.BlockSpec((HEAD_W, 1), lambda bi, hi, qi: (0, 0)),
        ],
        out_specs=pl.BlockSpec((None, tq, HEAD_W), lambda bi, hi, qi: (bi, qi, hi)),
        out_shape=jax.ShapeDtypeStruct((b, tq * nq, d), BF16),
        scratch_shapes=[pltpu.VMEM((HEAD_W, tq), F32), pltpu.VMEM((HEAD_W, tq), F32)],
        compiler_params=_cparams("parallel", "parallel", "parallel"),
        name="attn_ctx" if context else "attn_lat",
    )(q3, k3, vt, *[p.reshape(1, MAP_W) for p in lam_params], subln_g.reshape(HEAD_W, 1))


def _dft_matrices(n, scale):
    blk = min(ROW_BLOCK, n)
    k = jnp.arange(n, dtype=jnp.int32)

    def cs(j):
        r = (j[:, None] * k[None, :]) % n
        ang = r.astype(F32) * (2.0 * np.pi / n)
        return jnp.cos(ang), jnp.sin(ang)

    c0, s0 = cs(jnp.arange(blk, dtype=jnp.int32))
    cj, sj = cs(jnp.arange(0, n, blk, dtype=jnp.int32))
    cm = cj[:, None, :] * c0[None] - sj[:, None, :] * s0[None]
    sm = sj[:, None, :] * c0[None] + cj[:, None, :] * s0[None]
    return (cm * scale).reshape(n, n).astype(BF16), (sm * scale).reshape(n, n).astype(BF16)


def _chan_dft_kernel(x_ref, mod_ref, cs_ref, a_ref, b_ref):
    d = x_ref.shape[1]
    u = (x_ref[...] * (1.0 + mod_ref[1:2, :]) + mod_ref[0:1, :]).astype(BF16)
    for g in range(d // LANES):
        sl = slice(g * LANES, (g + 1) * LANES)
        ab = jnp.dot(u[:, sl], cs_ref[...], preferred_element_type=F32)
        a_ref[:, sl] = ab[:, :LANES].astype(BF16)
        b_ref[:, sl] = ab[:, LANES:].astype(BF16)


def _chan_dft(x, mod, nl, nc):
    t, d = x.shape
    tm = ROW_BLOCK
    j = np.arange(LANES)
    ang = 2.0 * np.pi * ((j[:, None] * j[None, :]) % LANES) / LANES
    cs = jnp.asarray(np.concatenate([np.cos(ang), np.sin(ang)], axis=1) / np.sqrt(LANES), BF16)
    row_spec = pl.BlockSpec((tm, d), lambda r: (r, 0))
    return pl.pallas_call(
        _chan_dft_kernel,
        grid=(t // tm,),
        in_specs=[
            row_spec,
            pl.BlockSpec((None, 6, d), lambda r, f=_mod_row_map(nl, nc): (f(r), 0, 0)),
            pl.BlockSpec((LANES, 2 * LANES), lambda r: (0, 0)),
        ],
        out_specs=[row_spec, row_spec],
        out_shape=[jax.ShapeDtypeStruct((t, d), BF16)] * 2,
        compiler_params=_cparams("parallel"),
        name="chan_dft",
    )(x, mod, cs)


def _seq_dft_kernel(c_ref, s_ref, a_ref, b_ref, o_ref, acc_ref):
    kk = pl.program_id(2)

    @pl.when(kk == 0)
    def _():
        acc_ref[...] = jnp.zeros_like(acc_ref)

    acc_ref[...] += (jnp.dot(c_ref[...], a_ref[...], preferred_element_type=F32)
                     - jnp.dot(s_ref[...], b_ref[...], preferred_element_type=F32))

    @pl.when(kk == pl.num_programs(2) - 1)
    def _():
        o_ref[...] = acc_ref[...].astype(o_ref.dtype)


def _seq_dft(a3, b3, length, row_off):
    b, s, d = a3.shape
    tile = min(DFT_TILE, length)
    cm, sm = _dft_matrices(length, length ** -0.5)
    nt = length // tile
    off = row_off // tile
    mat_spec = pl.BlockSpec((tile, tile), lambda bi, m, k: (m, k))
    in_spec = pl.BlockSpec((None, tile, d), lambda bi, m, k: (bi, k + off, 0))
    return pl.pallas_call(
        _seq_dft_kernel,
        grid=(b, nt, nt),
        in_specs=[mat_spec, mat_spec, in_spec, in_spec],
        out_specs=pl.BlockSpec((None, tile, d), lambda bi, m, k: (bi, m, 0)),
        out_shape=jax.ShapeDtypeStruct((b, length, d), BF16),
        scratch_shapes=[pltpu.VMEM((tile, d), F32)],
        compiler_params=_cparams("parallel", "parallel", "arbitrary"),
        name="seq_dft",
    )(cm, sm, a3, b3)


def _glu_kernel(x_ref, mod_ref, w_ref, b_ref, o_ref):
    d = x_ref.shape[1]
    u = (x_ref[...] * (1.0 + mod_ref[1:2, :]) + mod_ref[0:1, :]).astype(BF16)
    a = jnp.dot(u, w_ref[:, :d], preferred_element_type=F32) + b_ref[:, :d]
    g = jnp.dot(u, w_ref[:, d:], preferred_element_type=F32) + b_ref[:, d:]
    o_ref[...] = a * _sigmoid(g)


def _glu(x, mod, w, bias, nl, nc):
    t, d = x.shape
    tm = ROW_BLOCK
    row_spec = pl.BlockSpec((tm, d), lambda r: (r, 0))
    return pl.pallas_call(
        _glu_kernel,
        grid=(t // tm,),
        in_specs=[
            row_spec,
            pl.BlockSpec((None, 6, d), lambda r, f=_mod_row_map(nl, nc): (f(r), 0, 0)),
            pl.BlockSpec((d, 2 * d), lambda r: (0, 0)),
            pl.BlockSpec((1, 2 * d), lambda r: (0, 0)),
        ],
        out_specs=row_spec,
        out_shape=jax.ShapeDtypeStruct((t, d), F32),
        compiler_params=_cparams("parallel"),
        name="pw1_glu",
    )(x, mod, w.astype(BF16), bias.reshape(1, 2 * d))


def _dwconv_kernel(nl, nc, prev_ref, cur_ref, next_ref, w_ref, b_ref, g_ref, beta_ref, o_ref, win_ref, h_ref):
    tm, d = cur_ref.shape
    rr = pl.program_id(0) % (nl + nc)
    has_prev = jnp.logical_and(rr != 0, rr != nl)
    has_next = jnp.logical_and(rr != nl - 1, rr != nl + nc - 1)
    win_ref[0:HALO, :] = jnp.where(has_prev, prev_ref[...], 0.0)
    win_ref[HALO:HALO + tm, :] = cur_ref[...]
    win_ref[HALO + tm:, :] = jnp.where(has_next, next_ref[...], 0.0)
    rows = 64
    for c in range(d // LANES):
        cs = slice(c * LANES, (c + 1) * LANES)
        wc = w_ref[:, cs]
        for r0 in range(0, tm, rows):
            acc = jnp.broadcast_to(b_ref[:, cs], (rows, LANES))
            for j in range(CONV_WIDTH):
                start = r0 + HALO - CONV_PAD + j
                acc = acc + wc[j:j + 1, :] * win_ref[start:start + rows, cs]
            h_ref[r0:r0 + rows, cs] = acc
    y = _layer_norm(h_ref[...], g_ref[...], beta_ref[...])
    o_ref[...] = (y * _sigmoid(y)).astype(BF16)


def _dwconv_ln_silu(hid, w_dw, b_dw, ln_g, ln_b, nl, nc):
    t, d = hid.shape
    tm = ROW_BLOCK
    per = tm // HALO
    n_halo = t // HALO
    vec = pl.BlockSpec((1, d), lambda r: (0, 0))
    return pl.pallas_call(
        functools.partial(_dwconv_kernel, nl, nc),
        grid=(t // tm,),
        in_specs=[
            pl.BlockSpec((HALO, d), lambda r: (jnp.maximum(r * per - 1, 0), 0)),
            pl.BlockSpec((tm, d), lambda r: (r, 0)),
            pl.BlockSpec((HALO, d), lambda r: (jnp.minimum((r + 1) * per, n_halo - 1), 0)),
            pl.BlockSpec((CONV_WIDTH, d), lambda r: (0, 0)),
            vec, vec, vec,
        ],
        out_specs=pl.BlockSpec((tm, d), lambda r: (r, 0)),
        out_shape=jax.ShapeDtypeStruct((t, d), BF16),
        scratch_shapes=[pltpu.VMEM((tm + 2 * HALO, d), F32), pltpu.VMEM((tm, d), F32)],
        compiler_params=_cparams("parallel"),
        name="dwconv_ln_silu",
    )(hid, hid, hid, w_dw, b_dw.reshape(1, d), ln_g.reshape(1, d), ln_b.reshape(1, d))


def _pack_cols(cols, dtype):
    rows = cols[0].shape[0]
    lane = lax.broadcasted_iota(jnp.int32, (rows, len(cols)), 1)
    out = jnp.zeros((rows, len(cols)), dtype)
    for i, col in enumerate(cols):
        out = jnp.where(lane == i, col.astype(dtype), out)
    return out


def _proj_route_kernel(alpha, a_ref, w_ref, bias_ref, x_ref, mod_ref, g_ref, beta_ref, wr_ref, br_ref,
                       x1_ref, v_ref, idx_ref, gate_ref, rank_ref, cnt_ref, run_ref):
    tm = x_ref.shape[0]
    n_exp = wr_ref.shape[1]

    @pl.when(pl.program_id(0) == 0)
    def _():
        run_ref[...] = jnp.zeros_like(run_ref)

    y = jnp.dot(a_ref[...], w_ref[...], preferred_element_type=F32) + bias_ref[...]
    x1 = _layer_norm(alpha * x_ref[...] + mod_ref[2:3, :] * y, g_ref[...], beta_ref[...])
    x1_ref[...] = x1
    v = x1 * (1.0 + mod_ref[4:5, :]) + mod_ref[3:4, :]
    v_ref[...] = v

    logits = jnp.dot(v, wr_ref[...], precision=HIGHEST, preferred_element_type=F32) + br_ref[...]
    col = lax.broadcasted_iota(jnp.int32, logits.shape, 1).astype(F32)
    work = logits
    vals, idxs, sels = [], [], []
    for _ in range(TOP_K):
        mx = jnp.max(work, axis=1, keepdims=True)
        first = jnp.min(jnp.where(work == mx, col, float(n_exp)), axis=1, keepdims=True)
        sel = col == first
        work = jnp.where(sel, -jnp.inf, work)
        vals.append(mx)
        idxs.append(first)
        sels.append(sel)
    es = [jnp.exp(val - vals[0]) for val in vals]
    inv = 1.0 / (es[0] + es[1] + es[2] + es[3])

    onehot = jnp.zeros(logits.shape, F32)
    for sel in sels:
        onehot = onehot + jnp.where(sel, 1.0, 0.0)
    ri = lax.broadcasted_iota(jnp.int32, (tm, tm), 0)
    ci = lax.broadcasted_iota(jnp.int32, (tm, tm), 1)
    tri = jnp.where(ci < ri, 1.0, 0.0).astype(BF16)
    before = jnp.dot(tri, onehot.astype(BF16), preferred_element_type=F32) + run_ref[...]
    ranks = [jnp.sum(jnp.where(sel, before, 0.0), axis=1, keepdims=True) for sel in sels]
    run_ref[...] = run_ref[...] + jnp.sum(onehot, axis=0, keepdims=True)

    idx_ref[...] = _pack_cols(idxs, jnp.int32)
    gate_ref[...] = _pack_cols([e * inv for e in es], F32)
    rank_ref[...] = _pack_cols(ranks, jnp.int32)
    cnt_ref[...] = run_ref[...].astype(jnp.int32)


def _proj_route(a, w, bias, x, mod, ln_g, ln_b, w_router, b_router, alpha, nl, nc):
    t, d = x.shape
    n_exp = w_router.shape[1]
    tm = ROW_BLOCK
    row_spec = pl.BlockSpec((tm, d), lambda r: (r, 0))
    vec = pl.BlockSpec((1, d), lambda r: (0, 0))
    k_spec = pl.BlockSpec((tm, TOP_K), lambda r: (r, 0))
    e_spec = pl.BlockSpec((1, n_exp), lambda r: (0, 0))
    return pl.pallas_call(
        functools.partial(_proj_route_kernel, alpha),
        grid=(t // tm,),
        in_specs=[
            row_spec,
            pl.BlockSpec((d, d), lambda r: (0, 0)),
            vec,
            row_spec,
            pl.BlockSpec((None, 6, d), lambda r, f=_mod_row_map(nl, nc): (f(r), 0, 0)),
            vec, vec,
            pl.BlockSpec((d, n_exp), lambda r: (0, 0)),
            e_spec,
        ],
        out_specs=[row_spec, row_spec, k_spec, k_spec, k_spec, e_spec],
        out_shape=[
            jax.ShapeDtypeStruct((t, d), F32),
            jax.ShapeDtypeStruct((t, d), F32),
            jax.ShapeDtypeStruct((t, TOP_K), jnp.int32),
            jax.ShapeDtypeStruct((t, TOP_K), F32),
            jax.ShapeDtypeStruct((t, TOP_K), jnp.int32),
            jax.ShapeDtypeStruct((1, n_exp), jnp.int32),
        ],
        scratch_shapes=[pltpu.VMEM((1, n_exp), F32)],
        compiler_params=_cparams("arbitrary"),
        name="proj_ln_route",
    )(a, w.astype(BF16), bias.reshape(1, d), x, mod, ln_g.reshape(1, d), ln_b.reshape(1, d),
      w_router, b_router.reshape(1, n_exp))


def _dispatch_kernel(dest_ref, v_ref, xb_in_ref, xb_ref, sem):
    del xb_in_ref
    n = dest_ref.shape[0]

    def row_copy(i):
        return pltpu.make_async_copy(v_ref.at[pl.ds(i // TOP_K, 1)], xb_ref.at[pl.ds(dest_ref[i], 1)], sem)

    def issue(i, carry):
        row_copy(i).start()
        return carry

    def drain(i, carry):
        row_copy(i).wait()
        return carry

    lax.fori_loop(0, n, issue, 0)
    lax.fori_loop(0, n, drain, 0)


def _dispatch(v, dest_flat, p_rows):
    t, d = v.shape
    tm = DISPATCH_BLOCK
    return pl.pallas_call(
        _dispatch_kernel,
        grid=(t // tm,),
        in_specs=[
            pl.BlockSpec((tm * TOP_K,), lambda r: (r,), memory_space=pltpu.SMEM),
            pl.BlockSpec((tm, d), lambda r: (r, 0)),
            pl.BlockSpec(memory_space=pl.ANY),
        ],
        out_specs=pl.BlockSpec(memory_space=pl.ANY),
        out_shape=jax.ShapeDtypeStruct((p_rows, d), F32),
        scratch_shapes=[pltpu.SemaphoreType.DMA],
        input_output_aliases={2: 0},
        compiler_params=_cparams("arbitrary"),
        name="moe_dispatch",
    )(dest_flat, v, jnp.zeros((p_rows, d), F32))


def _expert_kernel(be_ref, nu_ref, x_ref, w1_ref, b1_ref, w2_ref, b2_ref, o_ref, w1b_ref, w2b_ref):
    j = pl.program_id(0)
    f = w2_ref.shape[0]
    prev = be_ref[jnp.maximum(j - 1, 0)]

    @pl.when(jnp.logical_or(j == 0, be_ref[j] != prev))
    def _():
        w1b_ref[...] = w1_ref[...].astype(BF16)
        w2b_ref[...] = w2_ref[...].astype(BF16)

    @pl.when(j < nu_ref[0])
    def _():
        h = jnp.dot(x_ref[...].astype(BF16), w1b_ref[...], preferred_element_type=F32) + b1_ref[...]
        glu = jnp.minimum(h[:, :f], SWIGLU_LIMIT)
        lin = jnp.clip(h[:, f:], -SWIGLU_LIMIT, SWIGLU_LIMIT)
        act = glu * _sigmoid(SWIGLU_ALPHA * glu) * (lin + 1.0)
        o_ref[...] = jnp.dot(act.astype(BF16), w2b_ref[...], preferred_element_type=F32) + b2_ref[...]

    @pl.when(j >= nu_ref[0])
    def _():
        o_ref[...] = jnp.zeros_like(o_ref)


def _experts(xb, block_e, n_used, w1, b1, w2, b2):
    p_rows, d = xb.shape
    n_exp, _, f2 = w1.shape
    f = f2 // 2
    bm = EXPERT_BLOCK
    grid_spec = pltpu.PrefetchScalarGridSpec(
        num_scalar_prefetch=2,
        grid=(p_rows // bm,),
        in_specs=[
            pl.BlockSpec((bm, d), lambda j, be, nu: (j, 0)),
            pl.BlockSpec((None, d, f2), lambda j, be, nu: (be[j], 0, 0)),
            pl.BlockSpec((None, 1, f2), lambda j, be, nu: (be[j], 0, 0)),
            pl.BlockSpec((None, f, d), lambda j, be, nu: (be[j], 0, 0)),
            pl.BlockSpec((None, 1, d), lambda j, be, nu: (be[j], 0, 0)),
        ],
        out_specs=pl.BlockSpec((bm, d), lambda j, be, nu: (j, 0)),
        scratch_shapes=[pltpu.VMEM((d, f2), BF16), pltpu.VMEM((f, d), BF16)],
    )
    return pl.pallas_call(
        _expert_kernel,
        grid_spec=grid_spec,
        out_shape=jax.ShapeDtypeStruct((p_rows, d), F32),
        compiler_params=_cparams("arbitrary"),
        name="moe_experts",
    )(block_e, n_used, xb, w1, b1.reshape(n_exp, 1, f2), w2, b2.reshape(n_exp, 1, d))


def _combine_kernel(alpha, dest_ref, gate_ref, x1_ref, mod_ref, g_ref, beta_ref, yb_ref, o_ref, rows_ref, sem):
    n = dest_ref.shape[0]

    def row_copy(i):
        return pltpu.make_async_copy(yb_ref.at[pl.ds(dest_ref[i], 1)],
                                     rows_ref.at[i % TOP_K, pl.ds(i // TOP_K, 1)], sem)

    def issue(i, carry):
        row_copy(i).start()
        return carry

    def drain(i, carry):
        row_copy(i).wait()
        return carry

    lax.fori_loop(0, n, issue, 0)
    lax.fori_loop(0, n, drain, 0)
    gates = gate_ref[...]
    f = gates[:, 0:1] * rows_ref[0]
    for k in range(1, TOP_K):
        f = f + gates[:, k:k + 1] * rows_ref[k]
    o_ref[...] = _layer_norm(alpha * x1_ref[...] + mod_ref[5:6, :] * f, g_ref[...], beta_ref[...])


def _combine(dest_flat, gates, x1, mod, ln_g, ln_b, yb, alpha, nl, nc):
    t, d = x1.shape
    tm = ROW_BLOCK
    row_spec = pl.BlockSpec((tm, d), lambda r: (r, 0))
    vec = pl.BlockSpec((1, d), lambda r: (0, 0))
    return pl.pallas_call(
        functools.partial(_combine_kernel, alpha),
        grid=(t // tm,),
        in_specs=[
            pl.BlockSpec((tm * TOP_K,), lambda r: (r,), memory_space=pltpu.SMEM),
            pl.BlockSpec((tm, TOP_K), lambda r: (r, 0)),
            row_spec,
            pl.BlockSpec((None, 6, d), lambda r, f=_mod_row_map(nl, nc): (f(r), 0, 0)),
            vec, vec,
            pl.BlockSpec(memory_space=pl.ANY),
        ],
        out_specs=row_spec,
        out_shape=jax.ShapeDtypeStruct((t, d), F32),
        scratch_shapes=[pltpu.VMEM((TOP_K, tm, d), F32), pltpu.SemaphoreType.DMA],
        compiler_params=_cparams("arbitrary"),
        name="moe_combine_ln",
    )(dest_flat, gates, x1, mod, ln_g.reshape(1, d), ln_b.reshape(1, d), yb)


def _moe_plan(idx, rank, counts, n_blocks):
    n_exp = counts.shape[0]
    bm = EXPERT_BLOCK
    padded = (counts + bm - 1) // bm * bm
    pend = jnp.cumsum(padded)
    pstart = pend - padded
    start_of = jnp.sum(jnp.where(idx[..., None] == jnp.arange(n_exp), pstart, 0), axis=-1)
    dest = (start_of + rank).astype(jnp.int32).reshape(-1)
    block_e = jnp.minimum(jnp.searchsorted(pend, jnp.arange(n_blocks) * bm, side="right"), n_exp - 1)
    return dest, block_e.astype(jnp.int32), (pend[-1:] // bm).astype(jnp.int32)


def kernel(x, c, ctx, c_ctx, w_mod, b_mod, ln1_g, ln1_b, ln2_g, ln2_b, attn_w_qkv, attn_w_o, attn_lam_q1, attn_lam_k1, attn_lam_q2, attn_lam_k2, attn_subln_g, fnet_w, fnet_b, conv_w_pw1, conv_b_pw1, conv_w_dw, conv_b_dw, conv_ln_g, conv_ln_b, conv_w_pw2, conv_b_pw2, moe_w_router, moe_b_router, moe_w1, moe_b1, moe_w2, moe_b2):
    b, n, d = x.shape
    cl = ctx.shape[1]
    s = n + cl
    t = b * s
    depth = w_mod.shape[0]
    n_exp = moe_w_router.shape[2]
    nl, nc = n // ROW_BLOCK, cl // ROW_BLOCK
    alpha = (2 * depth) ** 0.25
    n_blocks = -(-(t * TOP_K) // EXPERT_BLOCK) + n_exp
    p_rows = n_blocks * EXPERT_BLOCK

    xs = jnp.concatenate([x, ctx], axis=1).reshape(t, d)
    cond = jnp.stack([c, jnp.broadcast_to(c_ctx, c.shape)], axis=1).reshape(2 * b, d)
    mod = _modulation(cond, w_mod, b_mod)
    cos, sin = _rope_tables(n, cl)

    for i in range(depth):
        kind, j = i % 3, i // 3
        if kind == 0:
            lam_init = 0.8 - 0.6 * float(np.exp(-0.3 * i))
            q, k, v = _qkv(xs, mod[i], attn_w_qkv[j], cos, sin, nl, nc)
            q3, k3 = q.reshape(b, s, d), k.reshape(b, s, d)
            vt = v.reshape(b, s, d // HEAD_W, HEAD_W).transpose(0, 2, 3, 1)
            lam_params = (attn_lam_q1[j], attn_lam_k1[j], attn_lam_q2[j], attn_lam_k2[j])
            o_lat = _attention(q3, k3, vt, lam_params, attn_subln_g[j], lam_init, n, cl, context=False)
            o_ctx = _attention(q3, k3, vt, lam_params, attn_subln_g[j], lam_init, n, cl, context=True)
            a = jnp.concatenate([o_lat, o_ctx], axis=1).reshape(t, d)
            w_out, b_out = attn_w_o[j], jnp.zeros((d,), F32)
        elif kind == 1:
            fa, fb = _chan_dft(xs, mod[i], nl, nc)
            fa3, fb3 = fa.reshape(b, s, d), fb.reshape(b, s, d)
            a = jnp.concatenate([_seq_dft(fa3, fb3, n, 0), _seq_dft(fa3, fb3, cl, n)], axis=1).reshape(t, d)
            w_out, b_out = fnet_w[j], fnet_b[j]
        else:
            hid = _glu(xs, mod[i], conv_w_pw1[j], conv_b_pw1[j], nl, nc)
            a = _dwconv_ln_silu(hid, conv_w_dw[j], conv_b_dw[j], conv_ln_g[j], conv_ln_b[j], nl, nc)
            w_out, b_out = conv_w_pw2[j], conv_b_pw2[j]

        x1, v, idx, gates, rank, counts = _proj_route(
            a, w_out, b_out, xs, mod[i], ln1_g[i], ln1_b[i], moe_w_router[i], moe_b_router[i], alpha, nl, nc)
        dest, block_e, n_used = _moe_plan(idx, rank, counts[0], n_blocks)
        xb = _dispatch(v, dest, p_rows)
        yb = _experts(xb, block_e, n_used, moe_w1[i], moe_b1[i], moe_w2[i], moe_b2[i])
        xs = _combine(dest, gates, x1, mod[i], ln2_g[i], ln2_b[i], yb, alpha, nl, nc)

    return xs.reshape(b, s, d)[:, :n]
```

```python
import functools

import jax
import jax.numpy as jnp
import numpy as np
from jax import lax
from jax.experimental import pallas as pl
from jax.experimental.pallas import tpu as pltpu

F32 = jnp.float32
BF16 = jnp.bfloat16
HIGHEST = lax.Precision.HIGHEST

GRID_W = 64
HEAD_W = 128
MAP_W = HEAD_W // 2
ROPE_FREQS = MAP_W // 4
ROPE_BASE = 10000.0
CONV_WIDTH = 31
CONV_PAD = CONV_WIDTH // 2
TOP_K = 4
SWIGLU_ALPHA = 1.702
SWIGLU_LIMIT = 7.0
LN_EPS = 1e-5

LANES = 128
ROW_BLOCK = 256
SUM_ROWS = 8
HALO = 16
ATTN_TQ = 512
ATTN_TK = 256
DFT_TILE = 1024
EXPERT_BLOCK = 512
DISPATCH_BLOCK = 512
WAIT_ROWS = 64
VMEM_LIMIT = 56 * 1024 * 1024


def _cparams(*sem):
    return pltpu.CompilerParams(dimension_semantics=sem, vmem_limit_bytes=VMEM_LIMIT)


def _mod_row_map(nl, nc):
    rb = nl + nc

    def f(r):
        return 2 * (r // rb) + jnp.where((r % rb) >= nl, 1, 0)

    return f


def _layer_norm(z, g, b):
    mu = jnp.mean(z, axis=-1, keepdims=True)
    zc = z - mu
    var = jnp.mean(zc * zc, axis=-1, keepdims=True)
    return zc * lax.rsqrt(var + LN_EPS) * g + b


def _sigmoid(x):
    return 1.0 / (1.0 + jnp.exp(-x))


def _modulation_kernel(cond_ref, w_ref, b_ref, o_ref):
    cnd = cond_ref[...]
    s = cnd * _sigmoid(cnd)
    o_ref[...] = jnp.dot(s, w_ref[...], precision=HIGHEST, preferred_element_type=F32) + b_ref[...]


def _modulation(cond, w_mod, b_mod):
    depth, d, d6 = w_mod.shape
    rows = cond.shape[0]
    out = pl.pallas_call(
        _modulation_kernel,
        grid=(depth, d6 // d),
        in_specs=[
            pl.BlockSpec((rows, d), lambda i, j: (0, 0)),
            pl.BlockSpec((None, d, d), lambda i, j: (i, 0, j)),
            pl.BlockSpec((None, 1, d), lambda i, j: (i, 0, j)),
        ],
        out_specs=pl.BlockSpec((None, rows, d), lambda i, j: (i, 0, j)),
        out_shape=jax.ShapeDtypeStruct((depth, rows, d6), F32),
        compiler_params=_cparams("parallel", "parallel"),
        name="modulation",
    )(cond, w_mod, b_mod.reshape(depth, 1, d6))
    return out.reshape(depth, rows, d6 // d, d)


def _rope_tables(n, c):
    pos = jnp.arange(n)
    rows = (pos // GRID_W).astype(F32)
    cols = (pos % GRID_W).astype(F32)
    inv_freq = ROPE_BASE ** (-jnp.arange(ROPE_FREQS, dtype=F32) / ROPE_FREQS)
    lane = np.arange(HEAD_W)
    dim = lane % MAP_W
    freq = dim % ROPE_FREQS
    use_row = (dim // (MAP_W // 2)) == 0
    first_half = (dim % (MAP_W // 2)) < ROPE_FREQS
    ang = jnp.where(use_row[None, :], rows[:, None] * inv_freq[freq][None, :],
                    cols[:, None] * inv_freq[freq][None, :])
    cos = jnp.cos(ang)
    sin = jnp.sin(ang)
    sin = jnp.where(first_half[None, :], -sin, sin)
    cos = jnp.concatenate([cos, jnp.ones((c, HEAD_W), F32)], axis=0)
    sin = jnp.concatenate([sin, jnp.zeros((c, HEAD_W), F32)], axis=0)
    return cos, sin


def _qkv_kernel(x_ref, mod_ref, w_ref, cos_ref, sin_ref, q_ref, k_ref, v_ref):
    d = x_ref.shape[1]
    u = (x_ref[...] * (1.0 + mod_ref[1:2, :]) + mod_ref[0:1, :]).astype(BF16)
    cos = cos_ref[...]
    sin = sin_ref[...]
    lane = lax.broadcasted_iota(jnp.int32, cos.shape, 1)
    first_half = (lane % (MAP_W // 2)) < ROPE_FREQS

    def rope(t):
        partner = jnp.where(first_half, pltpu.roll(t, HEAD_W - ROPE_FREQS, 1), pltpu.roll(t, ROPE_FREQS, 1))
        return t * cos + partner * sin

    q = jnp.dot(u, w_ref[:, :d], preferred_element_type=F32)
    k = jnp.dot(u, w_ref[:, d:2 * d], preferred_element_type=F32)
    for h in range(d // HEAD_W):
        cs = slice(h * HEAD_W, (h + 1) * HEAD_W)
        q_ref[:, cs] = rope(q[:, cs]).astype(BF16)
        k_ref[:, cs] = rope(k[:, cs]).astype(BF16)
    v_ref[...] = jnp.dot(u, w_ref[:, 2 * d:], preferred_element_type=F32).astype(BF16)


def _qkv(x, mod, w_qkv, cos, sin, nl, nc):
    t, d = x.shape
    rb = nl + nc
    tm = ROW_BLOCK
    scale = jnp.concatenate([jnp.full((d,), np.log2(np.e) * MAP_W ** -0.5, F32), jnp.ones((2 * d,), F32)])
    w = (w_qkv * scale[None, :]).astype(BF16)
    row_spec = pl.BlockSpec((tm, d), lambda r: (r, 0))
    tab_spec = pl.BlockSpec((tm, HEAD_W), lambda r: (r % rb, 0))
    return pl.pallas_call(
        _qkv_kernel,
        grid=(t // tm,),
        in_specs=[
            row_spec,
            pl.BlockSpec((None, 6, d), lambda r, f=_mod_row_map(nl, nc): (f(r), 0, 0)),
            pl.BlockSpec((d, 3 * d), lambda r: (0, 0)),
            tab_spec, tab_spec,
        ],
        out_specs=[row_spec, row_spec, row_spec],
        out_shape=[jax.ShapeDtypeStruct((t, d), BF16)] * 3,
        compiler_params=_cparams("parallel"),
        name="qkv_rope",
    )(x, mod, w, cos, sin)


def _attn_kernel(lam_init, tk, q_ref, k_ref, vt_ref, lq1_ref, lk1_ref, lq2_ref, lk2_ref, g_ref,
                 o_ref, acc1_ref, acc2_ref, sa1_ref, sa2_ref, sb1_ref, sb2_ref):
    tq = q_ref.shape[0]
    n_kv = k_ref.shape[0] // tk
    q = q_ref[...].astype(F32)
    lane = lax.broadcasted_iota(jnp.int32, q.shape, 1)
    q1 = jnp.where(lane < MAP_W, q, 0.0).astype(BF16)
    q2 = jnp.where(lane >= MAP_W, q, 0.0).astype(BF16)
    acc1_ref[...] = jnp.zeros_like(acc1_ref)
    acc2_ref[...] = jnp.zeros_like(acc2_ref)
    acc_refs = (acc1_ref, acc2_ref)
    s_refs = ((sa1_ref, sa2_ref), (sb1_ref, sb2_ref))

    def scores(i, slot):
        k = k_ref[pl.ds(pl.multiple_of(i * tk, tk), tk), :]
        col_max = []
        for qm, s_ref in zip((q1, q2), s_refs[slot]):
            s = lax.dot_general(k, qm, (((1,), (1,)), ((), ())), preferred_element_type=F32)
            s_ref[...] = s
            col_max.append(jnp.max(s, axis=0, keepdims=True))
        return tuple(col_max)

    def accumulate(i, slot, col_max, ms):
        vt = vt_ref[:, pl.ds(pl.multiple_of(i * tk, tk), tk)]
        new_ms = []
        for m, cm, s_ref, acc_ref in zip(ms, col_max, s_refs[slot], acc_refs):
            m_new = jnp.maximum(m, cm)
            p = jnp.exp2(s_ref[...] - m_new).astype(BF16)
            acc_ref[...] = jnp.exp2(m - m_new) * acc_ref[...] + jnp.dot(vt, p, preferred_element_type=F32)
            new_ms.append(m_new)
        return tuple(new_ms)

    def pair(j, carry):
        ms, cm0 = carry
        cm1 = scores(2 * j + 1, 1)
        ms = accumulate(2 * j, 0, cm0, ms)
        cm0 = scores(2 * j + 2, 0)
        ms = accumulate(2 * j + 1, 1, cm1, ms)
        return ms, cm0

    neg = jnp.full((1, tq), -jnp.inf, F32)
    ms, cm0 = lax.fori_loop(0, (n_kv - 1) // 2, pair, ((neg, neg), scores(0, 0)))
    last = n_kv - 1
    if last % 2:
        cm1 = scores(last, 1)
        ms = accumulate(last - 1, 0, cm0, ms)
        accumulate(last, 1, cm1, ms)
    else:
        accumulate(last, 0, cm0, ms)

    lam = (jnp.exp(jnp.sum(lq1_ref[...] * lk1_ref[...], axis=1, keepdims=True))
           - jnp.exp(jnp.sum(lq2_ref[...] * lk2_ref[...], axis=1, keepdims=True)) + lam_init)
    inv1 = 1.0 / acc1_ref[HEAD_W:HEAD_W + 1, :]
    inv2 = 1.0 / acc2_ref[HEAD_W:HEAD_W + 1, :]
    o = acc1_ref[0:HEAD_W, :] * inv1 - lam * (acc2_ref[0:HEAD_W, :] * inv2)
    ms = jnp.mean(o * o, axis=0, keepdims=True)
    o = o * lax.rsqrt(ms + LN_EPS) * (g_ref[...] * (1.0 - lam_init))
    o_ref[...] = o.T.astype(BF16)


def _attention(q3, k3, vt, lam_params, subln_g, lam_init, n, c, context):
    b, s, d = q3.shape
    h = d // HEAD_W
    if context:
        tq, kv_len, kv_blk, q_off, nq = c, c, n // c, n // c, 1
    else:
        tq, kv_len, kv_blk, q_off, nq = ATTN_TQ, s, 0, 0, n // ATTN_TQ
    tk = min(ATTN_TK, kv_len)
    lam_spec = pl.BlockSpec((1, MAP_W), lambda bi, hi, qi: (0, 0))
    return pl.pallas_call(
        functools.partial(_attn_kernel, lam_init, tk),
        grid=(b, h, nq),
        in_specs=[
            pl.BlockSpec((None, tq, HEAD_W), lambda bi, hi, qi: (bi, qi + q_off, hi)),
            pl.BlockSpec((None, kv_len, HEAD_W), lambda bi, hi, qi: (bi, kv_blk, hi)),
            pl.BlockSpec((None, None, HEAD_W + SUM_ROWS, kv_len), lambda bi, hi, qi: (bi, hi, 0, kv_blk)),
            lam_spec, lam_spec, lam_spec, lam_spec,
            pl.BlockSpec((HEAD_W, 1), lambda bi, hi, qi: (0, 0)),
        ],
        out_specs=pl.BlockSpec((None, tq, HEAD_W), lambda bi, hi, qi: (bi, qi, hi)),
        out_shape=jax.ShapeDtypeStruct((b, tq * nq, d), BF16),
        scratch_shapes=[pltpu.VMEM((HEAD_W + SUM_ROWS, tq), F32)] * 2 + [pltpu.VMEM((tk, tq), F32)] * 4,
        compiler_params=_cparams("parallel", "parallel", "parallel"),
        name="attn_ctx" if context else "attn_lat",
    )(q3, k3, vt, *[p.reshape(1, MAP_W) for p in lam_params], subln_g.reshape(HEAD_W, 1))


def _dft_matrices(n, scale):
    blk = min(ROW_BLOCK, n)
    k = jnp.arange(n, dtype=jnp.int32)

    def cs(j):
        r = (j[:, None] * k[None, :]) % n
        ang = r.astype(F32) * (2.0 * np.pi / n)
        return jnp.cos(ang), jnp.sin(ang)

    c0, s0 = cs(jnp.arange(blk, dtype=jnp.int32))
    cj, sj = cs(jnp.arange(0, n, blk, dtype=jnp.int32))
    cm = cj[:, None, :] * c0[None] - sj[:, None, :] * s0[None]
    sm = sj[:, None, :] * c0[None] + cj[:, None, :] * s0[None]
    return (cm * scale).reshape(n, n).astype(BF16), (sm * scale).reshape(n, n).astype(BF16)


def _chan_dft_kernel(x_ref, mod_ref, cs_ref, a_ref, b_ref):
    d = x_ref.shape[1]
    u = (x_ref[...] * (1.0 + mod_ref[1:2, :]) + mod_ref[0:1, :]).astype(BF16)
    for g in range(d // LANES):
        sl = slice(g * LANES, (g + 1) * LANES)
        ab = jnp.dot(u[:, sl], cs_ref[...], preferred_element_type=F32)
        a_ref[:, sl] = ab[:, :LANES].astype(BF16)
        b_ref[:, sl] = ab[:, LANES:].astype(BF16)


def _chan_dft(x, mod, nl, nc):
    t, d = x.shape
    tm = ROW_BLOCK
    j = np.arange(LANES)
    ang = 2.0 * np.pi * ((j[:, None] * j[None, :]) % LANES) / LANES
    cs = jnp.asarray(np.concatenate([np.cos(ang), np.sin(ang)], axis=1) / np.sqrt(LANES), BF16)
    row_spec = pl.BlockSpec((tm, d), lambda r: (r, 0))
    return pl.pallas_call(
        _chan_dft_kernel,
        grid=(t // tm,),
        in_specs=[
            row_spec,
            pl.BlockSpec((None, 6, d), lambda r, f=_mod_row_map(nl, nc): (f(r), 0, 0)),
            pl.BlockSpec((LANES, 2 * LANES), lambda r: (0, 0)),
        ],
        out_specs=[row_spec, row_spec],
        out_shape=[jax.ShapeDtypeStruct((t, d), BF16)] * 2,
        compiler_params=_cparams("parallel"),
        name="chan_dft",
    )(x, mod, cs)


def _seq_dft_kernel(c_ref, s_ref, a_ref, b_ref, o_ref, acc_ref):
    kk = pl.program_id(2)

    @pl.when(kk == 0)
    def _():
        acc_ref[...] = jnp.zeros_like(acc_ref)

    acc_ref[...] += (jnp.dot(c_ref[...], a_ref[...], preferred_element_type=F32)
                     - jnp.dot(s_ref[...], b_ref[...], preferred_element_type=F32))

    @pl.when(kk == pl.num_programs(2) - 1)
    def _():
        o_ref[...] = acc_ref[...].astype(o_ref.dtype)


def _seq_dft(a3, b3, length, row_off):
    b, s, d = a3.shape
    tile = min(DFT_TILE, length)
    cm, sm = _dft_matrices(length, length ** -0.5)
    nt = length // tile
    off = row_off // tile
    mat_spec = pl.BlockSpec((tile, tile), lambda bi, m, k: (m, k))
    in_spec = pl.BlockSpec((None, tile, d), lambda bi, m, k: (bi, k + off, 0))
    return pl.pallas_call(
        _seq_dft_kernel,
        grid=(b, nt, nt),
        in_specs=[mat_spec, mat_spec, in_spec, in_spec],
        out_specs=pl.BlockSpec((None, tile, d), lambda bi, m, k: (bi, m, 0)),
        out_shape=jax.ShapeDtypeStruct((b, length, d), BF16),
        scratch_shapes=[pltpu.VMEM((tile, d), F32)],
        compiler_params=_cparams("parallel", "parallel", "arbitrary"),
        name="seq_dft",
    )(cm, sm, a3, b3)


def _glu_kernel(x_ref, mod_ref, w_ref, b_ref, o_ref):
    d = x_ref.shape[1]
    u = (x_ref[...] * (1.0 + mod_ref[1:2, :]) + mod_ref[0:1, :]).astype(BF16)
    a = jnp.dot(u, w_ref[:, :d], preferred_element_type=F32) + b_ref[:, :d]
    g = jnp.dot(u, w_ref[:, d:], preferred_element_type=F32) + b_ref[:, d:]
    o_ref[...] = a * _sigmoid(g)


def _glu(x, mod, w, bias, nl, nc):
    t, d = x.shape
    tm = ROW_BLOCK
    row_spec = pl.BlockSpec((tm, d), lambda r: (r, 0))
    return pl.pallas_call(
        _glu_kernel,
        grid=(t // tm,),
        in_specs=[
            row_spec,
            pl.BlockSpec((None, 6, d), lambda r, f=_mod_row_map(nl, nc): (f(r), 0, 0)),
            pl.BlockSpec((d, 2 * d), lambda r: (0, 0)),
            pl.BlockSpec((1, 2 * d), lambda r: (0, 0)),
        ],
        out_specs=row_spec,
        out_shape=jax.ShapeDtypeStruct((t, d), F32),
        compiler_params=_cparams("parallel"),
        name="pw1_glu",
    )(x, mod, w.astype(BF16), bias.reshape(1, 2 * d))


def _dwconv_kernel(nl, nc, prev_ref, cur_ref, next_ref, w_ref, b_ref, g_ref, beta_ref, o_ref, win_ref, h_ref):
    tm, d = cur_ref.shape
    rr = pl.program_id(0) % (nl + nc)
    has_prev = jnp.logical_and(rr != 0, rr != nl)
    has_next = jnp.logical_and(rr != nl - 1, rr != nl + nc - 1)
    win_ref[0:HALO, :] = jnp.where(has_prev, prev_ref[...], 0.0)
    win_ref[HALO:HALO + tm, :] = cur_ref[...]
    win_ref[HALO + tm:, :] = jnp.where(has_next, next_ref[...], 0.0)
    rows = 64
    for c in range(d // LANES):
        cs = slice(c * LANES, (c + 1) * LANES)
        wc = w_ref[:, cs]
        for r0 in range(0, tm, rows):
            acc = jnp.broadcast_to(b_ref[:, cs], (rows, LANES))
            for j in range(CONV_WIDTH):
                start = r0 + HALO - CONV_PAD + j
                acc = acc + wc[j:j + 1, :] * win_ref[start:start + rows, cs]
            h_ref[r0:r0 + rows, cs] = acc
    y = _layer_norm(h_ref[...], g_ref[...], beta_ref[...])
    o_ref[...] = (y * _sigmoid(y)).astype(BF16)


def _dwconv_ln_silu(hid, w_dw, b_dw, ln_g, ln_b, nl, nc):
    t, d = hid.shape
    tm = ROW_BLOCK
    per = tm // HALO
    n_halo = t // HALO
    vec = pl.BlockSpec((1, d), lambda r: (0, 0))
    return pl.pallas_call(
        functools.partial(_dwconv_kernel, nl, nc),
        grid=(t // tm,),
        in_specs=[
            pl.BlockSpec((HALO, d), lambda r: (jnp.maximum(r * per - 1, 0), 0)),
            pl.BlockSpec((tm, d), lambda r: (r, 0)),
            pl.BlockSpec((HALO, d), lambda r: (jnp.minimum((r + 1) * per, n_halo - 1), 0)),
            pl.BlockSpec((CONV_WIDTH, d), lambda r: (0, 0)),
            vec, vec, vec,
        ],
        out_specs=pl.BlockSpec((tm, d), lambda r: (r, 0)),
        out_shape=jax.ShapeDtypeStruct((t, d), BF16),
        scratch_shapes=[pltpu.VMEM((tm + 2 * HALO, d), F32), pltpu.VMEM((tm, d), F32)],
        compiler_params=_cparams("parallel"),
        name="dwconv_ln_silu",
    )(hid, hid, hid, w_dw, b_dw.reshape(1, d), ln_g.reshape(1, d), ln_b.reshape(1, d))


def _pack_cols(cols, dtype):
    rows = cols[0].shape[0]
    lane = lax.broadcasted_iota(jnp.int32, (rows, len(cols)), 1)
    out = jnp.zeros((rows, len(cols)), dtype)
    for i, col in enumerate(cols):
        out = jnp.where(lane == i, col.astype(dtype), out)
    return out


def _proj_route_kernel(alpha, a_ref, w_ref, bias_ref, x_ref, mod_ref, g_ref, beta_ref, wr_ref, br_ref,
                       x1_ref, v_ref, idx_ref, gate_ref, rank_ref, cnt_ref, run_ref):
    tm = x_ref.shape[0]
    n_exp = wr_ref.shape[2]

    @pl.when(pl.program_id(0) == 0)
    def _():
        run_ref[...] = jnp.zeros_like(run_ref)

    y = jnp.dot(a_ref[...], w_ref[...], preferred_element_type=F32) + bias_ref[...]
    x1 = _layer_norm(alpha * x_ref[...] + mod_ref[2:3, :] * y, g_ref[...], beta_ref[...])
    x1_ref[...] = x1
    v = x1 * (1.0 + mod_ref[4:5, :]) + mod_ref[3:4, :]
    v_ref[...] = v

    v_hi = v.astype(BF16)
    v_lo = (v - v_hi.astype(F32)).astype(BF16)
    logits = (jnp.dot(v_hi, wr_ref[0], preferred_element_type=F32)
              + jnp.dot(v_lo, wr_ref[0], preferred_element_type=F32)
              + jnp.dot(v_hi, wr_ref[1], preferred_element_type=F32)) + br_ref[...]
    col = lax.broadcasted_iota(jnp.int32, logits.shape, 1).astype(F32)
    work = logits
    vals, idxs, sels = [], [], []
    for _ in range(TOP_K):
        mx = jnp.max(work, axis=1, keepdims=True)
        first = jnp.min(jnp.where(work == mx, col, float(n_exp)), axis=1, keepdims=True)
        sel = col == first
        work = jnp.where(sel, -jnp.inf, work)
        vals.append(mx)
        idxs.append(first)
        sels.append(sel)
    es = [jnp.exp(val - vals[0]) for val in vals]
    inv = 1.0 / (es[0] + es[1] + es[2] + es[3])

    onehot = jnp.zeros(logits.shape, F32)
    for sel in sels:
        onehot = onehot + jnp.where(sel, 1.0, 0.0)
    ri = lax.broadcasted_iota(jnp.int32, (tm, tm), 0)
    ci = lax.broadcasted_iota(jnp.int32, (tm, tm), 1)
    tri = jnp.where(ci < ri, 1.0, 0.0).astype(BF16)
    before = jnp.dot(tri, onehot.astype(BF16), preferred_element_type=F32) + run_ref[...]
    ranks = [jnp.sum(jnp.where(sel, before, 0.0), axis=1, keepdims=True) for sel in sels]
    run_ref[...] = run_ref[...] + jnp.sum(onehot, axis=0, keepdims=True)

    idx_ref[...] = _pack_cols(idxs, jnp.int32)
    gate_ref[...] = _pack_cols([e * inv for e in es], F32)
    rank_ref[...] = _pack_cols(ranks, jnp.int32)
    cnt_ref[...] = run_ref[...].astype(jnp.int32)


def _proj_route(a, w, bias, x, mod, ln_g, ln_b, w_router, b_router, alpha, nl, nc):
    t, d = x.shape
    n_exp = w_router.shape[1]
    tm = ROW_BLOCK
    row_spec = pl.BlockSpec((tm, d), lambda r: (r, 0))
    vec = pl.BlockSpec((1, d), lambda r: (0, 0))
    k_spec = pl.BlockSpec((tm, TOP_K), lambda r: (r, 0))
    e_spec = pl.BlockSpec((1, n_exp), lambda r: (0, 0))
    wr_hi = w_router.astype(BF16)
    return pl.pallas_call(
        functools.partial(_proj_route_kernel, alpha),
        grid=(t // tm,),
        in_specs=[
            row_spec,
            pl.BlockSpec((d, d), lambda r: (0, 0)),
            vec,
            row_spec,
            pl.BlockSpec((None, 6, d), lambda r, f=_mod_row_map(nl, nc): (f(r), 0, 0)),
            vec, vec,
            pl.BlockSpec((2, d, n_exp), lambda r: (0, 0, 0)),
            e_spec,
        ],
        out_specs=[row_spec, row_spec, k_spec, k_spec, k_spec, e_spec],
        out_shape=[
            jax.ShapeDtypeStruct((t, d), F32),
            jax.ShapeDtypeStruct((t, d), F32),
            jax.ShapeDtypeStruct((t, TOP_K), jnp.int32),
            jax.ShapeDtypeStruct((t, TOP_K), F32),
            jax.ShapeDtypeStruct((t, TOP_K), jnp.int32),
            jax.ShapeDtypeStruct((1, n_exp), jnp.int32),
        ],
        scratch_shapes=[pltpu.VMEM((1, n_exp), F32)],
        compiler_params=_cparams("arbitrary"),
        name="proj_ln_route",
    )(a, w.astype(BF16), bias.reshape(1, d), x, mod, ln_g.reshape(1, d), ln_b.reshape(1, d),
      jnp.stack([wr_hi, (w_router - wr_hi.astype(F32)).astype(BF16)]), b_router.reshape(1, n_exp))


def _dispatch_kernel(dest_ref, v_ref, xb_ref, sem):
    tm = v_ref.shape[0]

    def issue(t, carry):
        for k in range(TOP_K):
            pltpu.make_async_copy(v_ref.at[pl.ds(t, 1)], xb_ref.at[pl.ds(dest_ref[t * TOP_K + k], 1)], sem).start()
        return carry

    def drain(g, carry):
        pltpu.make_async_copy(v_ref.at[pl.ds(0, WAIT_ROWS)], xb_ref.at[pl.ds(0, WAIT_ROWS)], sem).wait()
        return carry

    lax.fori_loop(0, tm, issue, 0, unroll=4)
    lax.fori_loop(0, tm * TOP_K // WAIT_ROWS, drain, 0)


def _dispatch(v, dest_flat, p_rows):
    t, d = v.shape
    tm = DISPATCH_BLOCK
    return pl.pallas_call(
        _dispatch_kernel,
        grid=(t // tm,),
        in_specs=[
            pl.BlockSpec((tm * TOP_K,), lambda r: (r,), memory_space=pltpu.SMEM),
            pl.BlockSpec((tm, d), lambda r: (r, 0)),
        ],
        out_specs=pl.BlockSpec(memory_space=pl.ANY),
        out_shape=jax.ShapeDtypeStruct((p_rows, d), F32),
        scratch_shapes=[pltpu.SemaphoreType.DMA],
        compiler_params=_cparams("arbitrary"),
        name="moe_dispatch",
    )(dest_flat, v)


def _expert_kernel(be_ref, nv_ref, nu_ref, x_ref, w1_ref, b1_ref, w2_ref, b2_ref, o_ref, w1b_ref, w2b_ref):
    j = pl.program_id(0)
    f = w2_ref.shape[0]
    prev = be_ref[jnp.maximum(j - 1, 0)]

    @pl.when(jnp.logical_and(j < nu_ref[0], jnp.logical_or(j == 0, be_ref[j] != prev)))
    def _():
        w1b_ref[...] = w1_ref[...].astype(BF16)
        w2b_ref[...] = w2_ref[...].astype(BF16)

    @pl.when(j < nu_ref[0])
    def _():
        row = lax.broadcasted_iota(jnp.int32, (x_ref.shape[0], 1), 0)
        x = jnp.where(row < nv_ref[j], x_ref[...], 0.0).astype(BF16)
        h = jnp.dot(x, w1b_ref[...], preferred_element_type=F32) + b1_ref[...]
        glu = jnp.minimum(h[:, :f], SWIGLU_LIMIT)
        lin = jnp.clip(h[:, f:], -SWIGLU_LIMIT, SWIGLU_LIMIT)
        act = glu * _sigmoid(SWIGLU_ALPHA * glu) * (lin + 1.0)
        o_ref[...] = jnp.dot(act.astype(BF16), w2b_ref[...], preferred_element_type=F32) + b2_ref[...]

    @pl.when(j >= nu_ref[0])
    def _():
        o_ref[...] = jnp.zeros_like(o_ref)


def _experts(xb, block_e, n_valid, n_used, layer, w1, b1, w2, b2):
    d = xb.shape[1]
    depth, n_exp, _, f2 = w1.shape
    f = f2 // 2
    bm = EXPERT_BLOCK
    n_blocks = block_e.shape[0]
    last = lambda j, nu: jnp.minimum(j, nu[0] - 1)
    grid_spec = pltpu.PrefetchScalarGridSpec(
        num_scalar_prefetch=3,
        grid=(n_blocks,),
        in_specs=[
            pl.BlockSpec((bm, d), lambda j, be, nv, nu: (last(j, nu), 0)),
            pl.BlockSpec((None, None, d, f2), lambda j, be, nv, nu: (layer, be[last(j, nu)], 0, 0)),
            pl.BlockSpec((None, None, 1, f2), lambda j, be, nv, nu: (layer, be[last(j, nu)], 0, 0)),
            pl.BlockSpec((None, None, f, d), lambda j, be, nv, nu: (layer, be[last(j, nu)], 0, 0)),
            pl.BlockSpec((None, None, 1, d), lambda j, be, nv, nu: (layer, be[last(j, nu)], 0, 0)),
        ],
        out_specs=pl.BlockSpec((bm, d), lambda j, be, nv, nu: (j, 0)),
        scratch_shapes=[pltpu.VMEM((d, f2), BF16), pltpu.VMEM((f, d), BF16)],
    )
    return pl.pallas_call(
        _expert_kernel,
        grid_spec=grid_spec,
        out_shape=jax.ShapeDtypeStruct((n_blocks * bm, d), F32),
        compiler_params=_cparams("arbitrary"),
        name="moe_experts",
    )(block_e, n_valid, n_used, xb, w1, b1.reshape(depth, n_exp, 1, f2), w2, b2.reshape(depth, n_exp, 1, d))


def _combine_kernel(alpha, dest_ref, gate_ref, x1_ref, mod_ref, g_ref, beta_ref, yb_ref, o_ref, rows_ref, sem):
    tm = x1_ref.shape[0]

    def issue(t, carry):
        for k in range(TOP_K):
            pltpu.make_async_copy(yb_ref.at[pl.ds(dest_ref[t * TOP_K + k], 1)],
                                  rows_ref.at[k, pl.ds(t, 1)], sem).start()
        return carry

    def drain(g, carry):
        pltpu.make_async_copy(yb_ref.at[pl.ds(0, WAIT_ROWS)], rows_ref.at[0, pl.ds(0, WAIT_ROWS)], sem).wait()
        return carry

    lax.fori_loop(0, tm, issue, 0, unroll=4)
    lax.fori_loop(0, tm * TOP_K // WAIT_ROWS, drain, 0)
    gates = gate_ref[...]
    f = gates[:, 0:1] * rows_ref[0]
    for k in range(1, TOP_K):
        f = f + gates[:, k:k + 1] * rows_ref[k]
    o_ref[...] = _layer_norm(alpha * x1_ref[...] + mod_ref[5:6, :] * f, g_ref[...], beta_ref[...])


def _combine(dest_flat, gates, x1, mod, ln_g, ln_b, yb, alpha, nl, nc):
    t, d = x1.shape
    tm = ROW_BLOCK
    row_spec = pl.BlockSpec((tm, d), lambda r: (r, 0))
    vec = pl.BlockSpec((1, d), lambda r: (0, 0))
    return pl.pallas_call(
        functools.partial(_combine_kernel, alpha),
        grid=(t // tm,),
        in_specs=[
            pl.BlockSpec((tm * TOP_K,), lambda r: (r,), memory_space=pltpu.SMEM),
            pl.BlockSpec((tm, TOP_K), lambda r: (r, 0)),
            row_spec,
            pl.BlockSpec((None, 6, d), lambda r, f=_mod_row_map(nl, nc): (f(r), 0, 0)),
            vec, vec,
            pl.BlockSpec(memory_space=pl.ANY),
        ],
        out_specs=row_spec,
        out_shape=jax.ShapeDtypeStruct((t, d), F32),
        scratch_shapes=[pltpu.VMEM((TOP_K, tm, d), F32), pltpu.SemaphoreType.DMA],
        compiler_params=_cparams("arbitrary"),
        name="moe_combine_ln",
    )(dest_flat, gates, x1, mod, ln_g.reshape(1, d), ln_b.reshape(1, d), yb)


def _moe_plan(idx, rank, counts, n_blocks):
    n_exp = counts.shape[0]
    bm = EXPERT_BLOCK
    experts = jnp.arange(n_exp)
    padded = (counts + bm - 1) // bm * bm
    pend = jnp.sum(jnp.where(experts[None, :] <= experts[:, None], padded[None, :], 0), axis=1)
    pstart = pend - padded
    start_of = jnp.sum(jnp.where(idx[..., None] == experts, pstart, 0), axis=-1)
    dest = (start_of + rank).astype(jnp.int32).reshape(-1)
    block_e = jnp.sum((pend[None, :] <= (jnp.arange(n_blocks) * bm)[:, None]).astype(jnp.int32), axis=1)
    block_e = jnp.minimum(block_e, n_exp - 1)
    end_of = jnp.sum(jnp.where(block_e[:, None] == experts, pstart + counts, 0), axis=-1)
    n_valid = jnp.clip(end_of - jnp.arange(n_blocks) * bm, 0, bm)
    return (dest, block_e.astype(jnp.int32), n_valid.astype(jnp.int32), (pend[-1:] // bm).astype(jnp.int32))


def kernel(x, c, ctx, c_ctx, w_mod, b_mod, ln1_g, ln1_b, ln2_g, ln2_b, attn_w_qkv, attn_w_o, attn_lam_q1, attn_lam_k1, attn_lam_q2, attn_lam_k2, attn_subln_g, fnet_w, fnet_b, conv_w_pw1, conv_b_pw1, conv_w_dw, conv_b_dw, conv_ln_g, conv_ln_b, conv_w_pw2, conv_b_pw2, moe_w_router, moe_b_router, moe_w1, moe_b1, moe_w2, moe_b2):
    b, n, d = x.shape
    cl = ctx.shape[1]
    s = n + cl
    t = b * s
    depth = w_mod.shape[0]
    n_exp = moe_w_router.shape[2]
    nl, nc = n // ROW_BLOCK, cl // ROW_BLOCK
    alpha = (2 * depth) ** 0.25
    n_blocks = -(-(t * TOP_K) // EXPERT_BLOCK) + n_exp
    p_rows = n_blocks * EXPERT_BLOCK

    xs = jnp.concatenate([x, ctx], axis=1).reshape(t, d)
    cond = jnp.stack([c, jnp.broadcast_to(c_ctx, c.shape)], axis=1).reshape(2 * b, d)
    mod = _modulation(cond, w_mod, b_mod)
    cos, sin = _rope_tables(n, cl)

    for i in range(depth):
        kind, j = i % 3, i // 3
        if kind == 0:
            lam_init = 0.8 - 0.6 * float(np.exp(-0.3 * i))
            q, k, v = _qkv(xs, mod[i], attn_w_qkv[j], cos, sin, nl, nc)
            q3, k3 = q.reshape(b, s, d), k.reshape(b, s, d)
            vt = v.reshape(b, s, d // HEAD_W, HEAD_W).transpose(0, 2, 3, 1)
            vt = jnp.concatenate([vt, jnp.ones((b, d // HEAD_W, SUM_ROWS, s), BF16)], axis=2)
            lam_params = (attn_lam_q1[j], attn_lam_k1[j], attn_lam_q2[j], attn_lam_k2[j])
            o_lat = _attention(q3, k3, vt, lam_params, attn_subln_g[j], lam_init, n, cl, context=False)
            o_ctx = _attention(q3, k3, vt, lam_params, attn_subln_g[j], lam_init, n, cl, context=True)
            a = jnp.concatenate([o_lat, o_ctx], axis=1).reshape(t, d)
            w_out, b_out = attn_w_o[j], jnp.zeros((d,), F32)
        elif kind == 1:
            fa, fb = _chan_dft(xs, mod[i], nl, nc)
            fa3, fb3 = fa.reshape(b, s, d), fb.reshape(b, s, d)
            a = jnp.concatenate([_seq_dft(fa3, fb3, n, 0), _seq_dft(fa3, fb3, cl, n)], axis=1).reshape(t, d)
            w_out, b_out = fnet_w[j], fnet_b[j]
        else:
            hid = _glu(xs, mod[i], conv_w_pw1[j], conv_b_pw1[j], nl, nc)
            a = _dwconv_ln_silu(hid, conv_w_dw[j], conv_b_dw[j], conv_ln_g[j], conv_ln_b[j], nl, nc)
            w_out, b_out = conv_w_pw2[j], conv_b_pw2[j]

        x1, v, idx, gates, rank, counts = _proj_route(
            a, w_out, b_out, xs, mod[i], ln1_g[i], ln1_b[i], moe_w_router[i], moe_b_router[i], alpha, nl, nc)
        dest, block_e, n_valid, n_used = _moe_plan(idx, rank, counts[0], n_blocks)
        xb = _dispatch(v, dest, p_rows)
        yb = _experts(xb, block_e, n_valid, n_used, i, moe_w1, moe_b1, moe_w2, moe_b2)
        xs = _combine(dest, gates, x1, mod[i], ln2_g[i], ln2_b[i], yb, alpha, nl, nc)

    return xs.reshape(b, s, d)[:, :n]
```

```python
import functools

import jax
import jax.numpy as jnp
import numpy as np
from jax import lax
from jax.experimental import pallas as pl
from jax.experimental.pallas import tpu as pltpu

F32 = jnp.float32
BF16 = jnp.bfloat16
HIGHEST = lax.Precision.HIGHEST

GRID_W = 64
HEAD_W = 128
MAP_W = HEAD_W // 2
ROPE_FREQS = MAP_W // 4
ROPE_BASE = 10000.0
CONV_WIDTH = 31
CONV_PAD = CONV_WIDTH // 2
TOP_K = 4
SWIGLU_ALPHA = 1.702
SWIGLU_LIMIT = 7.0
LN_EPS = 1e-5

LANES = 128
SUBLANES = 8
ROW_BLOCK = 256
SUM_ROWS = 8
HALO = 16
ATTN_TQ = 512
ATTN_TK = 256
ATTN_QC = 256
ATTN_UNROLL = 16
DFT_TILE = 1024
EXPERT_BLOCK = 512
DISPATCH_BLOCK = 512
WAIT_ROWS = 64
VMEM_LIMIT = 56 * 1024 * 1024


def _cparams(*sem):
    return pltpu.CompilerParams(dimension_semantics=sem, vmem_limit_bytes=VMEM_LIMIT)


def _mod_row_map(nl, nc):
    rb = nl + nc

    def f(r):
        return 2 * (r // rb) + jnp.where((r % rb) >= nl, 1, 0)

    return f


def _layer_norm(z, g, b):
    mu = jnp.mean(z, axis=-1, keepdims=True)
    zc = z - mu
    var = jnp.mean(zc * zc, axis=-1, keepdims=True)
    return zc * lax.rsqrt(var + LN_EPS) * g + b


def _sigmoid(x):
    return 1.0 / (1.0 + jnp.exp(-x))


def _modulation_kernel(cond_ref, w_ref, b_ref, o_ref):
    cnd = cond_ref[...]
    s = cnd * _sigmoid(cnd)
    o_ref[...] = jnp.dot(s, w_ref[...], precision=HIGHEST, preferred_element_type=F32) + b_ref[...]


def _modulation(cond, w_mod, b_mod):
    depth, d, d6 = w_mod.shape
    rows = cond.shape[0]
    out = pl.pallas_call(
        _modulation_kernel,
        grid=(depth, d6 // d),
        in_specs=[
            pl.BlockSpec((rows, d), lambda i, j: (0, 0)),
            pl.BlockSpec((None, d, d), lambda i, j: (i, 0, j)),
            pl.BlockSpec((None, 1, d), lambda i, j: (i, 0, j)),
        ],
        out_specs=pl.BlockSpec((None, rows, d), lambda i, j: (i, 0, j)),
        out_shape=jax.ShapeDtypeStruct((depth, rows, d6), F32),
        compiler_params=_cparams("parallel", "parallel"),
        name="modulation",
    )(cond, w_mod, b_mod.reshape(depth, 1, d6))
    return out.reshape(depth, rows, d6 // d, d)


def _rope_tables(n, c):
    pos = jnp.arange(n)
    rows = (pos // GRID_W).astype(F32)
    cols = (pos % GRID_W).astype(F32)
    inv_freq = ROPE_BASE ** (-jnp.arange(ROPE_FREQS, dtype=F32) / ROPE_FREQS)
    lane = np.arange(HEAD_W)
    dim = lane % MAP_W
    freq = dim % ROPE_FREQS
    use_row = (dim // (MAP_W // 2)) == 0
    first_half = (dim % (MAP_W // 2)) < ROPE_FREQS
    ang = jnp.where(use_row[None, :], rows[:, None] * inv_freq[freq][None, :],
                    cols[:, None] * inv_freq[freq][None, :])
    cos = jnp.cos(ang)
    sin = jnp.sin(ang)
    sin = jnp.where(first_half[None, :], -sin, sin)
    cos = jnp.concatenate([cos, jnp.ones((c, HEAD_W), F32)], axis=0)
    sin = jnp.concatenate([sin, jnp.zeros((c, HEAD_W), F32)], axis=0)
    return cos, sin


def _qkv_kernel(x_ref, mod_ref, w_ref, cos_ref, sin_ref, q_ref, k_ref, v_ref):
    d = x_ref.shape[1]
    u = (x_ref[...] * (1.0 + mod_ref[1:2, :]) + mod_ref[0:1, :]).astype(BF16)
    cos = cos_ref[...]
    sin = sin_ref[...]
    lane = lax.broadcasted_iota(jnp.int32, cos.shape, 1)
    first_half = (lane % (MAP_W // 2)) < ROPE_FREQS

    def rope(t):
        partner = jnp.where(first_half, pltpu.roll(t, HEAD_W - ROPE_FREQS, 1), pltpu.roll(t, ROPE_FREQS, 1))
        return t * cos + partner * sin

    q = jnp.dot(u, w_ref[:, :d], preferred_element_type=F32)
    k = jnp.dot(u, w_ref[:, d:2 * d], preferred_element_type=F32)
    for h in range(d // HEAD_W):
        cs = slice(h * HEAD_W, (h + 1) * HEAD_W)
        q_ref[:, cs] = rope(q[:, cs]).astype(BF16)
        k_ref[:, cs] = rope(k[:, cs]).astype(BF16)
    v_ref[...] = jnp.dot(u, w_ref[:, 2 * d:], preferred_element_type=F32).astype(BF16)


def _qkv(x, mod, w_qkv, cos, sin, nl, nc):
    t, d = x.shape
    rb = nl + nc
    tm = ROW_BLOCK
    scale = jnp.concatenate([jnp.full((d,), np.log2(np.e) * MAP_W ** -0.5, F32), jnp.ones((2 * d,), F32)])
    w = (w_qkv * scale[None, :]).astype(BF16)
    row_spec = pl.BlockSpec((tm, d), lambda r: (r, 0))
    tab_spec = pl.BlockSpec((tm, HEAD_W), lambda r: (r % rb, 0))
    return pl.pallas_call(
        _qkv_kernel,
        grid=(t // tm,),
        in_specs=[
            row_spec,
            pl.BlockSpec((None, 6, d), lambda r, f=_mod_row_map(nl, nc): (f(r), 0, 0)),
            pl.BlockSpec((d, 3 * d), lambda r: (0, 0)),
            tab_spec, tab_spec,
        ],
        out_specs=[row_spec, row_spec, row_spec],
        out_shape=[jax.ShapeDtypeStruct((t, d), BF16)] * 3,
        compiler_params=_cparams("parallel"),
        name="qkv_rope",
    )(x, mod, w, cos, sin)


def _attn_kernel(lam_init, tk, q_ref, k_ref, vt_ref, lq1_ref, lk1_ref, lq2_ref, lk2_ref, g_ref,
                 o_ref, acc1_ref, acc2_ref, sa1_ref, sa2_ref, sb1_ref, sb2_ref):
    tq = q_ref.shape[0]
    n_kv = k_ref.shape[0] // tk
    q = q_ref[...].astype(F32)
    lane = lax.broadcasted_iota(jnp.int32, q.shape, 1)
    q1 = jnp.where(lane < MAP_W, q, 0.0).astype(BF16)
    q2 = jnp.where(lane >= MAP_W, q, 0.0).astype(BF16)
    acc1_ref[...] = jnp.zeros_like(acc1_ref)
    acc2_ref[...] = jnp.zeros_like(acc2_ref)
    acc_refs = (acc1_ref, acc2_ref)
    s_refs = ((sa1_ref, sa2_ref), (sb1_ref, sb2_ref))

    qc = min(tq, ATTN_QC)
    pieces = [(mp, c) for c in range(tq // qc) for mp in range(2)]
    qms = (q1, q2)

    def score_piece(i, slot, mp, c):
        k = k_ref[pl.ds(pl.multiple_of(i * tk, tk), tk), :]
        s = lax.dot_general(k, qms[mp][c * qc:(c + 1) * qc, :], (((1,), (1,)), ((), ())),
                            preferred_element_type=F32)
        s_refs[slot][mp][:, c * qc:(c + 1) * qc] = s
        return jnp.max(s, axis=0, keepdims=True)

    def acc_piece(i, slot, mp, c, cm, m):
        vt = vt_ref[:, pl.ds(pl.multiple_of(i * tk, tk), tk)]
        cs = slice(c * qc, (c + 1) * qc)
        m_new = jnp.maximum(m, cm)
        p = jnp.exp2(s_refs[slot][mp][:, cs] - m_new).astype(BF16)
        acc_refs[mp][:, cs] = (jnp.exp2(m - m_new) * acc_refs[mp][:, cs]
                               + jnp.dot(vt, p, preferred_element_type=F32))
        return m_new

    def stage(i_score, slot_score, i_acc, slot_acc, cm_acc, ms):
        cm_new, ms_new = [], []
        for n_piece, (mp, c) in enumerate(pieces):
            if i_score is not None:
                cm_new.append(score_piece(i_score, slot_score, mp, c))
            if i_acc is not None:
                ms_new.append(acc_piece(i_acc, slot_acc, mp, c, cm_acc[n_piece], ms[n_piece]))
        return tuple(cm_new), tuple(ms_new) if i_acc is not None else ms

    def group(g, carry):
        ms, cm = carry
        for u in range(ATTN_UNROLL):
            i = g * ATTN_UNROLL + u
            cm, ms = stage(i + 1, (u + 1) % 2, i, u % 2, cm, ms)
        return ms, cm

    neg = tuple(jnp.full((1, qc), -jnp.inf, F32) for _ in pieces)
    cm, _ = stage(0, 0, None, None, None, neg)
    n_groups = (n_kv - 1) // ATTN_UNROLL
    ms, cm = lax.fori_loop(0, n_groups, group, (neg, cm))
    for i in range(n_groups * ATTN_UNROLL, n_kv - 1):
        cm, ms = stage(i + 1, (i + 1) % 2, i, i % 2, cm, ms)
    stage(None, None, n_kv - 1, (n_kv - 1) % 2, cm, ms)

    lam = (jnp.exp(jnp.sum(lq1_ref[...] * lk1_ref[...], axis=1, keepdims=True))
           - jnp.exp(jnp.sum(lq2_ref[...] * lk2_ref[...], axis=1, keepdims=True)) + lam_init)
    inv1 = 1.0 / acc1_ref[HEAD_W:HEAD_W + 1, :]
    inv2 = 1.0 / acc2_ref[HEAD_W:HEAD_W + 1, :]
    o = acc1_ref[0:HEAD_W, :] * inv1 - lam * (acc2_ref[0:HEAD_W, :] * inv2)
    ms = jnp.mean(o * o, axis=0, keepdims=True)
    o = o * lax.rsqrt(ms + LN_EPS) * (g_ref[...] * (1.0 - lam_init))
    o_ref[...] = o.T.astype(BF16)


def _attention(q3, k3, vt, lam_params, subln_g, lam_init, n, c, context):
    b, s, d = q3.shape
    h = d // HEAD_W
    if context:
        tq, kv_len, kv_blk, q_off, nq = c, c, n // c, n // c, 1
    else:
        tq, kv_len, kv_blk, q_off, nq = ATTN_TQ, s, 0, 0, n // ATTN_TQ
    tk = min(ATTN_TK, kv_len)
    lam_spec = pl.BlockSpec((1, MAP_W), lambda bi, hi, qi: (0, 0))
    return pl.pallas_call(
        functools.partial(_attn_kernel, lam_init, tk),
        grid=(b, h, nq),
        in_specs=[
            pl.BlockSpec((None, tq, HEAD_W), lambda bi, hi, qi: (bi, qi + q_off, hi)),
            pl.BlockSpec((None, kv_len, HEAD_W), lambda bi, hi, qi: (bi, kv_blk, hi)),
            pl.BlockSpec((None, None, HEAD_W + SUM_ROWS, kv_len), lambda bi, hi, qi: (bi, hi, 0, kv_blk)),
            lam_spec, lam_spec, lam_spec, lam_spec,
            pl.BlockSpec((HEAD_W, 1), lambda bi, hi, qi: (0, 0)),
        ],
        out_specs=pl.BlockSpec((None, tq, HEAD_W), lambda bi, hi, qi: (bi, qi, hi)),
        out_shape=jax.ShapeDtypeStruct((b, tq * nq, d), BF16),
        scratch_shapes=[pltpu.VMEM((HEAD_W + SUM_ROWS, tq), F32)] * 2 + [pltpu.VMEM((tk, tq), F32)] * 4,
        compiler_params=_cparams("parallel", "parallel", "parallel"),
        name="attn_ctx" if context else "attn_lat",
    )(q3, k3, vt, *[p.reshape(1, MAP_W) for p in lam_params], subln_g.reshape(HEAD_W, 1))


def _dft_matrices(n, scale):
    blk = min(ROW_BLOCK, n)
    k = jnp.arange(n, dtype=jnp.int32)

    def cs(j):
        r = (j[:, None] * k[None, :]) % n
        ang = r.astype(F32) * (2.0 * np.pi / n)
        return jnp.cos(ang), jnp.sin(ang)

    c0, s0 = cs(jnp.arange(blk, dtype=jnp.int32))
    cj, sj = cs(jnp.arange(0, n, blk, dtype=jnp.int32))
    cm = cj[:, None, :] * c0[None] - sj[:, None, :] * s0[None]
    sm = sj[:, None, :] * c0[None] + cj[:, None, :] * s0[None]
    return (cm * scale).reshape(n, n).astype(BF16), (sm * scale).reshape(n, n).astype(BF16)


def _chan_dft_kernel(x_ref, mod_ref, cs_ref, a_ref, b_ref):
    d = x_ref.shape[1]
    u = (x_ref[...] * (1.0 + mod_ref[1:2, :]) + mod_ref[0:1, :]).astype(BF16)
    for g in range(d // LANES):
        sl = slice(g * LANES, (g + 1) * LANES)
        ab = jnp.dot(u[:, sl], cs_ref[...], preferred_element_type=F32)
        a_ref[:, sl] = ab[:, :LANES].astype(BF16)
        b_ref[:, sl] = ab[:, LANES:].astype(BF16)


def _chan_dft(x, mod, nl, nc):
    t, d = x.shape
    tm = ROW_BLOCK
    j = np.arange(LANES)
    ang = 2.0 * np.pi * ((j[:, None] * j[None, :]) % LANES) / LANES
    cs = jnp.asarray(np.concatenate([np.cos(ang), np.sin(ang)], axis=1) / np.sqrt(LANES), BF16)
    row_spec = pl.BlockSpec((tm, d), lambda r: (r, 0))
    return pl.pallas_call(
        _chan_dft_kernel,
        grid=(t // tm,),
        in_specs=[
            row_spec,
            pl.BlockSpec((None, 6, d), lambda r, f=_mod_row_map(nl, nc): (f(r), 0, 0)),
            pl.BlockSpec((LANES, 2 * LANES), lambda r: (0, 0)),
        ],
        out_specs=[row_spec, row_spec],
        out_shape=[jax.ShapeDtypeStruct((t, d), BF16)] * 2,
        compiler_params=_cparams("parallel"),
        name="chan_dft",
    )(x, mod, cs)


def _seq_dft_kernel(c_ref, s_ref, a_ref, b_ref, o_ref, acc_ref):
    kk = pl.program_id(2)

    @pl.when(kk == 0)
    def _():
        acc_ref[...] = jnp.zeros_like(acc_ref)

    acc_ref[...] += (jnp.dot(c_ref[...], a_ref[...], preferred_element_type=F32)
                     - jnp.dot(s_ref[...], b_ref[...], preferred_element_type=F32))

    @pl.when(kk == pl.num_programs(2) - 1)
    def _():
        o_ref[...] = acc_ref[...].astype(o_ref.dtype)


def _seq_dft(a3, b3, length, row_off):
    b, s, d = a3.shape
    tile = min(DFT_TILE, length)
    cm, sm = _dft_matrices(length, length ** -0.5)
    nt = length // tile
    off = row_off // tile
    mat_spec = pl.BlockSpec((tile, tile), lambda bi, m, k: (m, k))
    in_spec = pl.BlockSpec((None, tile, d), lambda bi, m, k: (bi, k + off, 0))
    return pl.pallas_call(
        _seq_dft_kernel,
        grid=(b, nt, nt),
        in_specs=[mat_spec, mat_spec, in_spec, in_spec],
        out_specs=pl.BlockSpec((None, tile, d), lambda bi, m, k: (bi, m, 0)),
        out_shape=jax.ShapeDtypeStruct((b, length, d), BF16),
        scratch_shapes=[pltpu.VMEM((tile, d), F32)],
        compiler_params=_cparams("parallel", "parallel", "arbitrary"),
        name="seq_dft",
    )(cm, sm, a3, b3)


def _glu_kernel(x_ref, mod_ref, w_ref, b_ref, o_ref):
    d = x_ref.shape[1]
    u = (x_ref[...] * (1.0 + mod_ref[1:2, :]) + mod_ref[0:1, :]).astype(BF16)
    a = jnp.dot(u, w_ref[:, :d], preferred_element_type=F32) + b_ref[:, :d]
    g = jnp.dot(u, w_ref[:, d:], preferred_element_type=F32) + b_ref[:, d:]
    o_ref[...] = a * _sigmoid(g)


def _glu(x, mod, w, bias, nl, nc):
    t, d = x.shape
    tm = ROW_BLOCK
    row_spec = pl.BlockSpec((tm, d), lambda r: (r, 0))
    return pl.pallas_call(
        _glu_kernel,
        grid=(t // tm,),
        in_specs=[
            row_spec,
            pl.BlockSpec((None, 6, d), lambda r, f=_mod_row_map(nl, nc): (f(r), 0, 0)),
            pl.BlockSpec((d, 2 * d), lambda r: (0, 0)),
            pl.BlockSpec((1, 2 * d), lambda r: (0, 0)),
        ],
        out_specs=row_spec,
        out_shape=jax.ShapeDtypeStruct((t, d), F32),
        compiler_params=_cparams("parallel"),
        name="pw1_glu",
    )(x, mod, w.astype(BF16), bias.reshape(1, 2 * d))


def _dwconv_kernel(nl, nc, prev_ref, cur_ref, next_ref, w_ref, b_ref, g_ref, beta_ref, o_ref, win_ref, h_ref):
    tm, d = cur_ref.shape
    rr = pl.program_id(0) % (nl + nc)
    has_prev = jnp.logical_and(rr != 0, rr != nl)
    has_next = jnp.logical_and(rr != nl - 1, rr != nl + nc - 1)
    win_ref[0, 0:HALO, :] = jnp.where(has_prev, prev_ref[...], 0.0)
    win_ref[0, HALO:HALO + tm, :] = cur_ref[...]
    win_ref[0, HALO + tm:, :] = jnp.where(has_next, next_ref[...], 0.0)
    span = tm + 2 * HALO - SUBLANES
    for s in range(1, SUBLANES):
        win_ref[s, 0:span, :] = win_ref[0, s:s + span, :]
    rows = 64
    for c in range(d // LANES):
        cs = slice(c * LANES, (c + 1) * LANES)
        wc = w_ref[:, cs]
        for r0 in range(0, tm, rows):
            acc = jnp.broadcast_to(b_ref[:, cs], (rows, LANES))
            for j in range(CONV_WIDTH):
                off = HALO - CONV_PAD + j
                start = r0 + off - off % SUBLANES
                acc = acc + wc[j:j + 1, :] * win_ref[off % SUBLANES, start:start + rows, cs]
            h_ref[r0:r0 + rows, cs] = acc
    y = _layer_norm(h_ref[...], g_ref[...], beta_ref[...])
    o_ref[...] = (y * _sigmoid(y)).astype(BF16)


def _dwconv_ln_silu(hid, w_dw, b_dw, ln_g, ln_b, nl, nc):
    t, d = hid.shape
    tm = ROW_BLOCK
    per = tm // HALO
    n_halo = t // HALO
    vec = pl.BlockSpec((1, d), lambda r: (0, 0))
    return pl.pallas_call(
        functools.partial(_dwconv_kernel, nl, nc),
        grid=(t // tm,),
        in_specs=[
            pl.BlockSpec((HALO, d), lambda r: (jnp.maximum(r * per - 1, 0), 0)),
            pl.BlockSpec((tm, d), lambda r: (r, 0)),
            pl.BlockSpec((HALO, d), lambda r: (jnp.minimum((r + 1) * per, n_halo - 1), 0)),
            pl.BlockSpec((CONV_WIDTH, d), lambda r: (0, 0)),
            vec, vec, vec,
        ],
        out_specs=pl.BlockSpec((tm, d), lambda r: (r, 0)),
        out_shape=jax.ShapeDtypeStruct((t, d), BF16),
        scratch_shapes=[pltpu.VMEM((SUBLANES, tm + 2 * HALO, d), F32), pltpu.VMEM((tm, d), F32)],
        compiler_params=_cparams("parallel"),
        name="dwconv_ln_silu",
    )(hid, hid, hid, w_dw, b_dw.reshape(1, d), ln_g.reshape(1, d), ln_b.reshape(1, d))


def _pack_cols(cols, dtype):
    rows = cols[0].shape[0]
    lane = lax.broadcasted_iota(jnp.int32, (rows, len(cols)), 1)
    out = jnp.zeros((rows, len(cols)), dtype)
    for i, col in enumerate(cols):
        out = jnp.where(lane == i, col.astype(dtype), out)
    return out


def _proj_route_kernel(alpha, a_ref, w_ref, bias_ref, x_ref, mod_ref, g_ref, beta_ref, wr_ref, br_ref,
                       x1_ref, v_ref, idx_ref, gate_ref, rank_ref, cnt_ref, run_ref):
    tm = x_ref.shape[0]
    n_exp = wr_ref.shape[2]

    @pl.when(pl.program_id(0) == 0)
    def _():
        run_ref[...] = jnp.zeros_like(run_ref)

    y = jnp.dot(a_ref[...], w_ref[...], preferred_element_type=F32) + bias_ref[...]
    x1 = _layer_norm(alpha * x_ref[...] + mod_ref[2:3, :] * y, g_ref[...], beta_ref[...])
    x1_ref[...] = x1
    v = x1 * (1.0 + mod_ref[4:5, :]) + mod_ref[3:4, :]
    v_ref[...] = v

    v_hi = v.astype(BF16)
    v_lo = (v - v_hi.astype(F32)).astype(BF16)
    logits = (jnp.dot(v_hi, wr_ref[0], preferred_element_type=F32)
              + jnp.dot(v_lo, wr_ref[0], preferred_element_type=F32)
              + jnp.dot(v_hi, wr_ref[1], preferred_element_type=F32)) + br_ref[...]
    col = lax.broadcasted_iota(jnp.int32, logits.shape, 1).astype(F32)
    work = logits
    vals, idxs, sels = [], [], []
    for _ in range(TOP_K):
        mx = jnp.max(work, axis=1, keepdims=True)
        first = jnp.min(jnp.where(work == mx, col, float(n_exp)), axis=1, keepdims=True)
        sel = col == first
        work = jnp.where(sel, -jnp.inf, work)
        vals.append(mx)
        idxs.append(first)
        sels.append(sel)
    es = [jnp.exp(val - vals[0]) for val in vals]
    inv = 1.0 / (es[0] + es[1] + es[2] + es[3])

    onehot = jnp.zeros(logits.shape, F32)
    for sel in sels:
        onehot = onehot + jnp.where(sel, 1.0, 0.0)
    ri = lax.broadcasted_iota(jnp.int32, (tm, tm), 0)
    ci = lax.broadcasted_iota(jnp.int32, (tm, tm), 1)
    tri = jnp.where(ci < ri, 1.0, 0.0).astype(BF16)
    before = jnp.dot(tri, onehot.astype(BF16), preferred_element_type=F32) + run_ref[...]
    ranks = [jnp.sum(jnp.where(sel, before, 0.0), axis=1, keepdims=True) for sel in sels]
    run_ref[...] = run_ref[...] + jnp.sum(onehot, axis=0, keepdims=True)

    idx_ref[...] = _pack_cols(idxs, jnp.int32)
    gate_ref[...] = _pack_cols([e * inv for e in es], F32)
    rank_ref[...] = _pack_cols(ranks, jnp.int32)
    cnt_ref[...] = run_ref[...].astype(jnp.int32)


def _proj_route(a, w, bias, x, mod, ln_g, ln_b, w_router, b_router, alpha, nl, nc):
    t, d = x.shape
    n_exp = w_router.shape[1]
    tm = ROW_BLOCK
    row_spec = pl.BlockSpec((tm, d), lambda r: (r, 0))
    vec = pl.BlockSpec((1, d), lambda r: (0, 0))
    k_spec = pl.BlockSpec((tm, TOP_K), lambda r: (r, 0))
    e_spec = pl.BlockSpec((1, n_exp), lambda r: (0, 0))
    wr_hi = w_router.astype(BF16)
    return pl.pallas_call(
        functools.partial(_proj_route_kernel, alpha),
        grid=(t // tm,),
        in_specs=[
            row_spec,
            pl.BlockSpec((d, d), lambda r: (0, 0)),
            vec,
            row_spec,
            pl.BlockSpec((None, 6, d), lambda r, f=_mod_row_map(nl, nc): (f(r), 0, 0)),
            vec, vec,
            pl.BlockSpec((2, d, n_exp), lambda r: (0, 0, 0)),
            e_spec,
        ],
        out_specs=[row_spec, row_spec, k_spec, k_spec, k_spec, e_spec],
        out_shape=[
            jax.ShapeDtypeStruct((t, d), F32),
            jax.ShapeDtypeStruct((t, d), F32),
            jax.ShapeDtypeStruct((t, TOP_K), jnp.int32),
            jax.ShapeDtypeStruct((t, TOP_K), F32),
            jax.ShapeDtypeStruct((t, TOP_K), jnp.int32),
            jax.ShapeDtypeStruct((1, n_exp), jnp.int32),
        ],
        scratch_shapes=[pltpu.VMEM((1, n_exp), F32)],
        compiler_params=_cparams("arbitrary"),
        name="proj_ln_route",
    )(a, w.astype(BF16), bias.reshape(1, d), x, mod, ln_g.reshape(1, d), ln_b.reshape(1, d),
      jnp.stack([wr_hi, (w_router - wr_hi.astype(F32)).astype(BF16)]), b_router.reshape(1, n_exp))


def _dispatch_kernel(tail_ref, pad_ref, nu_ref, dest_ref, v_ref, xb_ref, zero_ref, sem, fill_sem):
    tm = v_ref.shape[0]
    bm = zero_ref.shape[0]
    n_blocks = xb_ref.shape[0] // bm

    @pl.when(pl.program_id(0) == 0)
    def _():
        zero_ref[...] = jnp.zeros_like(zero_ref)

        def padding(do):
            for e in range(tail_ref.shape[0]):
                tail, pad = tail_ref[e], pad_ref[e]
                head = jnp.minimum((-tail) % SUBLANES, pad)
                for i in range(SUBLANES - 1):
                    copy = pltpu.make_async_copy(zero_ref.at[pl.ds(0, 1)], xb_ref.at[pl.ds(tail + i, 1)], fill_sem)
                    pl.when(i < head)(functools.partial(do, copy))
                rest = pad - head
                for bit in reversed(range(SUBLANES.bit_length() - 1, bm.bit_length() - 1)):
                    size = 1 << bit
                    off = pl.multiple_of(tail + head + ((rest >> (bit + 1)) << (bit + 1)), SUBLANES)
                    copy = pltpu.make_async_copy(zero_ref.at[pl.ds(0, size)], xb_ref.at[pl.ds(off, size)], fill_sem)
                    pl.when((rest & size) != 0)(functools.partial(do, copy))

        def blocks(do):
            def body(j, carry):
                do(pltpu.make_async_copy(zero_ref, xb_ref.at[pl.ds(pl.multiple_of(j * bm, bm), bm)], fill_sem))
                return carry
            lax.fori_loop(nu_ref[0], n_blocks, body, 0)

        padding(lambda copy: copy.start())
        blocks(lambda copy: copy.start())
        padding(lambda copy: copy.wait())
        blocks(lambda copy: copy.wait())

    def issue(t, carry):
        for k in range(TOP_K):
            pltpu.make_async_copy(v_ref.at[pl.ds(t, 1)], xb_ref.at[pl.ds(dest_ref[t * TOP_K + k], 1)], sem).start()
        return carry

    def drain(g, carry):
        pltpu.make_async_copy(v_ref.at[pl.ds(0, WAIT_ROWS)], xb_ref.at[pl.ds(0, WAIT_ROWS)], sem).wait()
        return carry

    lax.fori_loop(0, tm, issue, 0, unroll=4)
    lax.fori_loop(0, tm * TOP_K // WAIT_ROWS, drain, 0)


def _dispatch(v, dest_flat, tail_start, pad_len, n_used, p_rows):
    t, d = v.shape
    tm = DISPATCH_BLOCK
    grid_spec = pltpu.PrefetchScalarGridSpec(
        num_scalar_prefetch=3,
        grid=(t // tm,),
        in_specs=[
            pl.BlockSpec((tm * TOP_K,), lambda r, *_: (r,), memory_space=pltpu.SMEM),
            pl.BlockSpec((tm, d), lambda r, *_: (r, 0)),
        ],
        out_specs=pl.BlockSpec(memory_space=pl.ANY),
        scratch_shapes=[pltpu.VMEM((EXPERT_BLOCK, d), F32), pltpu.SemaphoreType.DMA, pltpu.SemaphoreType.DMA],
    )
    return pl.pallas_call(
        _dispatch_kernel,
        grid_spec=grid_spec,
        out_shape=jax.ShapeDtypeStruct((p_rows, d), F32),
        compiler_params=_cparams("arbitrary"),
        name="moe_dispatch",
    )(tail_start, pad_len, n_used, dest_flat, v)


def _expert_kernel(be_ref, nu_ref, x_ref, w1_ref, b1_ref, w2_ref, b2_ref, o_ref, w1b_ref, w2b_ref):
    j = pl.program_id(0)
    f = w2_ref.shape[0]
    prev = be_ref[jnp.maximum(j - 1, 0)]

    @pl.when(jnp.logical_and(j < nu_ref[0], jnp.logical_or(j == 0, be_ref[j] != prev)))
    def _():
        w1b_ref[...] = w1_ref[...].astype(BF16)
        w2b_ref[...] = w2_ref[...].astype(BF16)

    @pl.when(j < nu_ref[0])
    def _():
        h = jnp.dot(x_ref[...].astype(BF16), w1b_ref[...], preferred_element_type=F32) + b1_ref[...]
        glu = jnp.minimum(h[:, :f], SWIGLU_LIMIT)
        lin = jnp.clip(h[:, f:], -SWIGLU_LIMIT, SWIGLU_LIMIT)
        act = glu * _sigmoid(SWIGLU_ALPHA * glu) * (lin + 1.0)
        o_ref[...] = jnp.dot(act.astype(BF16), w2b_ref[...], preferred_element_type=F32) + b2_ref[...]

    @pl.when(j >= nu_ref[0])
    def _():
        o_ref[...] = jnp.zeros_like(o_ref)


def _experts(xb, block_e, n_used, layer, w1, b1, w2, b2):
    d = xb.shape[1]
    depth, n_exp, _, f2 = w1.shape
    f = f2 // 2
    bm = EXPERT_BLOCK
    n_blocks = block_e.shape[0]
    last = lambda j, nu: jnp.minimum(j, nu[0] - 1)
    grid_spec = pltpu.PrefetchScalarGridSpec(
        num_scalar_prefetch=2,
        grid=(n_blocks,),
        in_specs=[
            pl.BlockSpec((bm, d), lambda j, be, nu: (last(j, nu), 0)),
            pl.BlockSpec((None, None, d, f2), lambda j, be, nu: (layer, be[last(j, nu)], 0, 0)),
            pl.BlockSpec((None, None, 1, f2), lambda j, be, nu: (layer, be[last(j, nu)], 0, 0)),
            pl.BlockSpec((None, None, f, d), lambda j, be, nu: (layer, be[last(j, nu)], 0, 0)),
            pl.BlockSpec((None, None, 1, d), lambda j, be, nu: (layer, be[last(j, nu)], 0, 0)),
        ],
        out_specs=pl.BlockSpec((bm, d), lambda j, be, nu: (j, 0)),
        scratch_shapes=[pltpu.VMEM((d, f2), BF16), pltpu.VMEM((f, d), BF16)],
    )
    return pl.pallas_call(
        _expert_kernel,
        grid_spec=grid_spec,
        out_shape=jax.ShapeDtypeStruct((n_blocks * bm, d), F32),
        compiler_params=_cparams("arbitrary"),
        name="moe_experts",
    )(block_e, n_used, xb, w1, b1.reshape(depth, n_exp, 1, f2), w2, b2.reshape(depth, n_exp, 1, d))


def _combine_kernel(alpha, dest_ref, next_dest_ref, gate_ref, x1_ref, mod_ref, g_ref, beta_ref, yb_ref, o_ref,
                    rows_ref, sems):
    r = pl.program_id(0)
    tm = x1_ref.shape[0]
    slot = r % 2

    def gather(d_ref, into):
        def issue(t, carry):
            for k in range(TOP_K):
                pltpu.make_async_copy(yb_ref.at[pl.ds(d_ref[t * TOP_K + k], 1)],
                                      rows_ref.at[into, k, pl.ds(t, 1)], sems.at[into]).start()
            return carry
        lax.fori_loop(0, tm, issue, 0, unroll=4)

    @pl.when(r == 0)
    def _():
        gather(dest_ref, 0)

    @pl.when(r + 1 < pl.num_programs(0))
    def _():
        gather(next_dest_ref, 1 - slot)

    def drain(g, carry):
        pltpu.make_async_copy(yb_ref.at[pl.ds(0, WAIT_ROWS)], rows_ref.at[slot, 0, pl.ds(0, WAIT_ROWS)],
                              sems.at[slot]).wait()
        return carry

    lax.fori_loop(0, tm * TOP_K // WAIT_ROWS, drain, 0)
    gates = gate_ref[...]
    f = gates[:, 0:1] * rows_ref[slot, 0]
    for k in range(1, TOP_K):
        f = f + gates[:, k:k + 1] * rows_ref[slot, k]
    o_ref[...] = _layer_norm(alpha * x1_ref[...] + mod_ref[5:6, :] * f, g_ref[...], beta_ref[...])


def _combine(dest_flat, gates, x1, mod, ln_g, ln_b, yb, alpha, nl, nc):
    t, d = x1.shape
    tm = ROW_BLOCK
    row_spec = pl.BlockSpec((tm, d), lambda r: (r, 0))
    vec = pl.BlockSpec((1, d), lambda r: (0, 0))
    return pl.pallas_call(
        functools.partial(_combine_kernel, alpha),
        grid=(t // tm,),
        in_specs=[
            pl.BlockSpec((tm * TOP_K,), lambda r: (r,), memory_space=pltpu.SMEM),
            pl.BlockSpec((tm * TOP_K,), lambda r: (jnp.minimum(r + 1, t // tm - 1),), memory_space=pltpu.SMEM),
            pl.BlockSpec((tm, TOP_K), lambda r: (r, 0)),
            row_spec,
            pl.BlockSpec((None, 6, d), lambda r, f=_mod_row_map(nl, nc): (f(r), 0, 0)),
            vec, vec,
            pl.BlockSpec(memory_space=pl.ANY),
        ],
        out_specs=row_spec,
        out_shape=jax.ShapeDtypeStruct((t, d), F32),
        scratch_shapes=[pltpu.VMEM((2, TOP_K, tm, d), F32), pltpu.SemaphoreType.DMA((2,))],
        compiler_params=_cparams("arbitrary"),
        name="moe_combine_ln",
    )(dest_flat, dest_flat, gates, x1, mod, ln_g.reshape(1, d), ln_b.reshape(1, d), yb)


def _moe_plan(idx, rank, counts, n_blocks):
    n_exp = counts.shape[0]
    bm = EXPERT_BLOCK
    experts = jnp.arange(n_exp)
    padded = (counts + bm - 1) // bm * bm
    pend = jnp.sum(jnp.where(experts[None, :] <= experts[:, None], padded[None, :], 0), axis=1)
    pstart = pend - padded
    start_of = jnp.sum(jnp.where(idx[..., None] == experts, pstart, 0), axis=-1)
    dest = (start_of + rank).astype(jnp.int32).reshape(-1)
    block_e = jnp.sum((pend[None, :] <= (jnp.arange(n_blocks) * bm)[:, None]).astype(jnp.int32), axis=1)
    block_e = jnp.minimum(block_e, n_exp - 1)
    return (dest, (pstart + counts).astype(jnp.int32), (padded - counts).astype(jnp.int32),
            block_e.astype(jnp.int32), (pend[-1:] // bm).astype(jnp.int32))


def kernel(x, c, ctx, c_ctx, w_mod, b_mod, ln1_g, ln1_b, ln2_g, ln2_b, attn_w_qkv, attn_w_o, attn_lam_q1, attn_lam_k1, attn_lam_q2, attn_lam_k2, attn_subln_g, fnet_w, fnet_b, conv_w_pw1, conv_b_pw1, conv_w_dw, conv_b_dw, conv_ln_g, conv_ln_b, conv_w_pw2, conv_b_pw2, moe_w_router, moe_b_router, moe_w1, moe_b1, moe_w2, moe_b2):
    b, n, d = x.shape
    cl = ctx.shape[1]
    s = n + cl
    t = b * s
    depth = w_mod.shape[0]
    n_exp = moe_w_router.shape[2]
    nl, nc = n // ROW_BLOCK, cl // ROW_BLOCK
    alpha = (2 * depth) ** 0.25
    n_blocks = -(-(t * TOP_K) // EXPERT_BLOCK) + n_exp
    p_rows = n_blocks * EXPERT_BLOCK

    xs = jnp.concatenate([x, ctx], axis=1).reshape(t, d)
    cond = jnp.stack([c, jnp.broadcast_to(c_ctx, c.shape)], axis=1).reshape(2 * b, d)
    mod = _modulation(cond, w_mod, b_mod)
    cos, sin = _rope_tables(n, cl)

    for i in range(depth):
        kind, j = i % 3, i // 3
        if kind == 0:
            lam_init = 0.8 - 0.6 * float(np.exp(-0.3 * i))
            q, k, v = _qkv(xs, mod[i], attn_w_qkv[j], cos, sin, nl, nc)
            q3, k3 = q.reshape(b, s, d), k.reshape(b, s, d)
            vt = v.reshape(b, s, d // HEAD_W, HEAD_W).transpose(0, 2, 3, 1)
            vt = jnp.concatenate([vt, jnp.ones((b, d // HEAD_W, SUM_ROWS, s), BF16)], axis=2)
            lam_params = (attn_lam_q1[j], attn_lam_k1[j], attn_lam_q2[j], attn_lam_k2[j])
            o_lat = _attention(q3, k3, vt, lam_params, attn_subln_g[j], lam_init, n, cl, context=False)
            o_ctx = _attention(q3, k3, vt, lam_params, attn_subln_g[j], lam_init, n, cl, context=True)
            a = jnp.concatenate([o_lat, o_ctx], axis=1).reshape(t, d)
            w_out, b_out = attn_w_o[j], jnp.zeros((d,), F32)
        elif kind == 1:
            fa, fb = _chan_dft(xs, mod[i], nl, nc)
            fa3, fb3 = fa.reshape(b, s, d), fb.reshape(b, s, d)
            a = jnp.concatenate([_seq_dft(fa3, fb3, n, 0), _seq_dft(fa3, fb3, cl, n)], axis=1).reshape(t, d)
            w_out, b_out = fnet_w[j], fnet_b[j]
        else:
            hid = _glu(xs, mod[i], conv_w_pw1[j], conv_b_pw1[j], nl, nc)
            a = _dwconv_ln_silu(hid, conv_w_dw[j], conv_b_dw[j], conv_ln_g[j], conv_ln_b[j], nl, nc)
            w_out, b_out = conv_w_pw2[j], conv_b_pw2[j]

        x1, v, idx, gates, rank, counts = _proj_route(
            a, w_out, b_out, xs, mod[i], ln1_g[i], ln1_b[i], moe_w_router[i], moe_b_router[i], alpha, nl, nc)
        dest, tail_start, pad_len, block_e, n_used = _moe_plan(idx, rank, counts[0], n_blocks)
        xb = _dispatch(v, dest, tail_start, pad_len, n_used, p_rows)
        yb = _experts(xb, block_e, n_used, i, moe_w1, moe_b1, moe_w2, moe_b2)
        xs = _combine(dest, gates, x1, mod[i], ln2_g[i], ln2_b[i], yb, alpha, nl, nc)

    return xs.reshape(b, s, d)[:, :n]
```

```python
import functools

import jax
import jax.numpy as jnp
import numpy as np
from jax import lax
from jax.experimental import pallas as pl
from jax.experimental.pallas import tpu as pltpu

F32 = jnp.float32
BF16 = jnp.bfloat16
HIGHEST = lax.Precision.HIGHEST

GRID_W = 64
HEAD_W = 128
MAP_W = HEAD_W // 2
ROPE_FREQS = MAP_W // 4
ROPE_BASE = 10000.0
CONV_WIDTH = 31
CONV_PAD = CONV_WIDTH // 2
TOP_K = 4
SWIGLU_ALPHA = 1.702
SWIGLU_LIMIT = 7.0
LN_EPS = 1e-5

LANES = 128
SUBLANES = 8
ROW_BLOCK = 256
SUM_ROWS = 8
HALO = 16
ATTN_TQ = 512
ATTN_TK = 256
ATTN_QC = 256
ATTN_UNROLL = 16
DFT_TILE = 1024
EXPERT_BLOCK = 512
TILE_TABLE = 1024
VMEM_LIMIT = 56 * 1024 * 1024


def _cparams(*sem):
    return pltpu.CompilerParams(dimension_semantics=sem, vmem_limit_bytes=VMEM_LIMIT)


def _mod_row_map(nl, nc):
    rb = nl + nc

    def f(r):
        return 2 * (r // rb) + jnp.where((r % rb) >= nl, 1, 0)

    return f


def _layer_norm(z, g, b):
    mu = jnp.mean(z, axis=-1, keepdims=True)
    zc = z - mu
    var = jnp.mean(zc * zc, axis=-1, keepdims=True)
    return zc * lax.rsqrt(var + LN_EPS) * g + b


def _sigmoid(x):
    return 1.0 / (1.0 + jnp.exp(-x))


def _modulation_kernel(cond_ref, w_ref, b_ref, o_ref):
    cnd = cond_ref[...]
    s = cnd * _sigmoid(cnd)
    o_ref[...] = jnp.dot(s, w_ref[...], precision=HIGHEST, preferred_element_type=F32) + b_ref[...]


def _modulation(cond, w_mod, b_mod):
    depth, d, d6 = w_mod.shape
    rows = cond.shape[0]
    out = pl.pallas_call(
        _modulation_kernel,
        grid=(depth, d6 // d),
        in_specs=[
            pl.BlockSpec((rows, d), lambda i, j: (0, 0)),
            pl.BlockSpec((None, d, d), lambda i, j: (i, 0, j)),
            pl.BlockSpec((None, 1, d), lambda i, j: (i, 0, j)),
        ],
        out_specs=pl.BlockSpec((None, rows, d), lambda i, j: (i, 0, j)),
        out_shape=jax.ShapeDtypeStruct((depth, rows, d6), F32),
        compiler_params=_cparams("parallel", "parallel"),
        name="modulation",
    )(cond, w_mod, b_mod.reshape(depth, 1, d6))
    return out.reshape(depth, rows, d6 // d, d)


def _rope_tables(n, c):
    pos = jnp.arange(n)
    rows = (pos // GRID_W).astype(F32)
    cols = (pos % GRID_W).astype(F32)
    inv_freq = ROPE_BASE ** (-jnp.arange(ROPE_FREQS, dtype=F32) / ROPE_FREQS)
    lane = np.arange(HEAD_W)
    dim = lane % MAP_W
    freq = dim % ROPE_FREQS
    use_row = (dim // (MAP_W // 2)) == 0
    first_half = (dim % (MAP_W // 2)) < ROPE_FREQS
    ang = jnp.where(use_row[None, :], rows[:, None] * inv_freq[freq][None, :],
                    cols[:, None] * inv_freq[freq][None, :])
    cos = jnp.cos(ang)
    sin = jnp.sin(ang)
    sin = jnp.where(first_half[None, :], -sin, sin)
    cos = jnp.concatenate([cos, jnp.ones((c, HEAD_W), F32)], axis=0)
    sin = jnp.concatenate([sin, jnp.zeros((c, HEAD_W), F32)], axis=0)
    return cos, sin


def _qkv_kernel(x_ref, mod_ref, w_ref, cos_ref, sin_ref, q_ref, k_ref, v_ref):
    d = x_ref.shape[1]
    u = (x_ref[...] * (1.0 + mod_ref[1:2, :]) + mod_ref[0:1, :]).astype(BF16)
    cos = cos_ref[...]
    sin = sin_ref[...]
    lane = lax.broadcasted_iota(jnp.int32, cos.shape, 1)
    first_half = (lane % (MAP_W // 2)) < ROPE_FREQS

    def rope(t):
        partner = jnp.where(first_half, pltpu.roll(t, HEAD_W - ROPE_FREQS, 1), pltpu.roll(t, ROPE_FREQS, 1))
        return t * cos + partner * sin

    q = jnp.dot(u, w_ref[:, :d], preferred_element_type=F32)
    k = jnp.dot(u, w_ref[:, d:2 * d], preferred_element_type=F32)
    for h in range(d // HEAD_W):
        cs = slice(h * HEAD_W, (h + 1) * HEAD_W)
        q_ref[:, cs] = rope(q[:, cs]).astype(BF16)
        k_ref[:, cs] = rope(k[:, cs]).astype(BF16)
    v_ref[...] = jnp.dot(u, w_ref[:, 2 * d:], preferred_element_type=F32).astype(BF16)


def _qkv(x, mod, w_qkv, cos, sin, nl, nc):
    t, d = x.shape
    rb = nl + nc
    tm = ROW_BLOCK
    scale = jnp.concatenate([jnp.full((d,), np.log2(np.e) * MAP_W ** -0.5, F32), jnp.ones((2 * d,), F32)])
    w = (w_qkv * scale[None, :]).astype(BF16)
    row_spec = pl.BlockSpec((tm, d), lambda r: (r, 0))
    tab_spec = pl.BlockSpec((tm, HEAD_W), lambda r: (r % rb, 0))
    return pl.pallas_call(
        _qkv_kernel,
        grid=(t // tm,),
        in_specs=[
            row_spec,
            pl.BlockSpec((None, 6, d), lambda r, f=_mod_row_map(nl, nc): (f(r), 0, 0)),
            pl.BlockSpec((d, 3 * d), lambda r: (0, 0)),
            tab_spec, tab_spec,
        ],
        out_specs=[row_spec, row_spec, row_spec],
        out_shape=[jax.ShapeDtypeStruct((t, d), BF16)] * 3,
        compiler_params=_cparams("parallel"),
        name="qkv_rope",
    )(x, mod, w, cos, sin)


def _attn_kernel(lam_init, tk, q_ref, k_ref, vt_ref, lq1_ref, lk1_ref, lq2_ref, lk2_ref, g_ref,
                 o_ref, acc1_ref, acc2_ref, sa1_ref, sa2_ref, sb1_ref, sb2_ref):
    tq = q_ref.shape[0]
    n_kv = k_ref.shape[0] // tk
    q = q_ref[...].astype(F32)
    lane = lax.broadcasted_iota(jnp.int32, q.shape, 1)
    q1 = jnp.where(lane < MAP_W, q, 0.0).astype(BF16)
    q2 = jnp.where(lane >= MAP_W, q, 0.0).astype(BF16)
    acc1_ref[...] = jnp.zeros_like(acc1_ref)
    acc2_ref[...] = jnp.zeros_like(acc2_ref)
    acc_refs = (acc1_ref, acc2_ref)
    s_refs = ((sa1_ref, sa2_ref), (sb1_ref, sb2_ref))

    qc = min(tq, ATTN_QC)
    pieces = [(mp, c) for c in range(tq // qc) for mp in range(2)]
    qms = (q1, q2)

    def score_piece(i, slot, mp, c):
        k = k_ref[pl.ds(pl.multiple_of(i * tk, tk), tk), :]
        s = lax.dot_general(k, qms[mp][c * qc:(c + 1) * qc, :], (((1,), (1,)), ((), ())),
                            preferred_element_type=F32)
        s_refs[slot][mp][:, c * qc:(c + 1) * qc] = s
        return jnp.max(s, axis=0, keepdims=True)

    def acc_piece(i, slot, mp, c, cm, m):
        vt = vt_ref[:, pl.ds(pl.multiple_of(i * tk, tk), tk)]
        cs = slice(c * qc, (c + 1) * qc)
        m_new = jnp.maximum(m, cm)
        p = jnp.exp2(s_refs[slot][mp][:, cs] - m_new).astype(BF16)
        acc_refs[mp][:, cs] = (jnp.exp2(m - m_new) * acc_refs[mp][:, cs]
                               + jnp.dot(vt, p, preferred_element_type=F32))
        return m_new

    def stage(i_score, slot_score, i_acc, slot_acc, cm_acc, ms):
        cm_new, ms_new = [], []
        for n_piece, (mp, c) in enumerate(pieces):
            if i_score is not None:
                cm_new.append(score_piece(i_score, slot_score, mp, c))
            if i_acc is not None:
                ms_new.append(acc_piece(i_acc, slot_acc, mp, c, cm_acc[n_piece], ms[n_piece]))
        return tuple(cm_new), tuple(ms_new) if i_acc is not None else ms

    def group(g, carry):
        ms, cm = carry
        for u in range(ATTN_UNROLL):
            i = g * ATTN_UNROLL + u
            cm, ms = stage(i + 1, (u + 1) % 2, i, u % 2, cm, ms)
        return ms, cm

    neg = tuple(jnp.full((1, qc), -jnp.inf, F32) for _ in pieces)
    cm, _ = stage(0, 0, None, None, None, neg)
    n_groups = (n_kv - 1) // ATTN_UNROLL
    ms, cm = lax.fori_loop(0, n_groups, group, (neg, cm))
    for i in range(n_groups * ATTN_UNROLL, n_kv - 1):
        cm, ms = stage(i + 1, (i + 1) % 2, i, i % 2, cm, ms)
    stage(None, None, n_kv - 1, (n_kv - 1) % 2, cm, ms)

    lam = (jnp.exp(jnp.sum(lq1_ref[...] * lk1_ref[...], axis=1, keepdims=True))
           - jnp.exp(jnp.sum(lq2_ref[...] * lk2_ref[...], axis=1, keepdims=True)) + lam_init)
    inv1 = 1.0 / acc1_ref[HEAD_W:HEAD_W + 1, :]
    inv2 = 1.0 / acc2_ref[HEAD_W:HEAD_W + 1, :]
    o = acc1_ref[0:HEAD_W, :] * inv1 - lam * (acc2_ref[0:HEAD_W, :] * inv2)
    ms = jnp.mean(o * o, axis=0, keepdims=True)
    o = o * lax.rsqrt(ms + LN_EPS) * (g_ref[...] * (1.0 - lam_init))
    o_ref[...] = o.T.astype(BF16)


def _attention(q3, k3, vt, lam_params, subln_g, lam_init, n, c, context):
    b, s, d = q3.shape
    h = d // HEAD_W
    if context:
        tq, kv_len, kv_blk, q_off, nq = c, c, n // c, n // c, 1
    else:
        tq, kv_len, kv_blk, q_off, nq = ATTN_TQ, s, 0, 0, n // ATTN_TQ
    tk = min(ATTN_TK, kv_len)
    lam_spec = pl.BlockSpec((1, MAP_W), lambda bi, hi, qi: (0, 0))
    return pl.pallas_call(
        functools.partial(_attn_kernel, lam_init, tk),
        grid=(b, h, nq),
        in_specs=[
            pl.BlockSpec((None, tq, HEAD_W), lambda bi, hi, qi: (bi, qi + q_off, hi)),
            pl.BlockSpec((None, kv_len, HEAD_W), lambda bi, hi, qi: (bi, kv_blk, hi)),
            pl.BlockSpec((None, None, HEAD_W + SUM_ROWS, kv_len), lambda bi, hi, qi: (bi, hi, 0, kv_blk)),
            lam_spec, lam_spec, lam_spec, lam_spec,
            pl.BlockSpec((HEAD_W, 1), lambda bi, hi, qi: (0, 0)),
        ],
        out_specs=pl.BlockSpec((None, tq, HEAD_W), lambda bi, hi, qi: (bi, qi, hi)),
        out_shape=jax.ShapeDtypeStruct((b, tq * nq, d), BF16),
        scratch_shapes=[pltpu.VMEM((HEAD_W + SUM_ROWS, tq), F32)] * 2 + [pltpu.VMEM((tk, tq), F32)] * 4,
        compiler_params=_cparams("parallel", "parallel", "parallel"),
        name="attn_ctx" if context else "attn_lat",
    )(q3, k3, vt, *[p.reshape(1, MAP_W) for p in lam_params], subln_g.reshape(HEAD_W, 1))


def _dft_matrices(n, scale):
    blk = min(ROW_BLOCK, n)
    k = jnp.arange(n, dtype=jnp.int32)

    def cs(j):
        r = (j[:, None] * k[None, :]) % n
        ang = r.astype(F32) * (2.0 * np.pi / n)
        return jnp.cos(ang), jnp.sin(ang)

    c0, s0 = cs(jnp.arange(blk, dtype=jnp.int32))
    cj, sj = cs(jnp.arange(0, n, blk, dtype=jnp.int32))
    cm = cj[:, None, :] * c0[None] - sj[:, None, :] * s0[None]
    sm = sj[:, None, :] * c0[None] + cj[:, None, :] * s0[None]
    return (cm * scale).reshape(n, n).astype(BF16), (sm * scale).reshape(n, n).astype(BF16)


def _chan_dft_kernel(x_ref, mod_ref, cs_ref, a_ref, b_ref):
    d = x_ref.shape[1]
    u = (x_ref[...] * (1.0 + mod_ref[1:2, :]) + mod_ref[0:1, :]).astype(BF16)
    for g in range(d // LANES):
        sl = slice(g * LANES, (g + 1) * LANES)
        ab = jnp.dot(u[:, sl], cs_ref[...], preferred_element_type=F32)
        a_ref[:, sl] = ab[:, :LANES].astype(BF16)
        b_ref[:, sl] = ab[:, LANES:].astype(BF16)


def _chan_dft(x, mod, nl, nc):
    t, d = x.shape
    tm = ROW_BLOCK
    j = np.arange(LANES)
    ang = 2.0 * np.pi * ((j[:, None] * j[None, :]) % LANES) / LANES
    cs = jnp.asarray(np.concatenate([np.cos(ang), np.sin(ang)], axis=1) / np.sqrt(LANES), BF16)
    row_spec = pl.BlockSpec((tm, d), lambda r: (r, 0))
    return pl.pallas_call(
        _chan_dft_kernel,
        grid=(t // tm,),
        in_specs=[
            row_spec,
            pl.BlockSpec((None, 6, d), lambda r, f=_mod_row_map(nl, nc): (f(r), 0, 0)),
            pl.BlockSpec((LANES, 2 * LANES), lambda r: (0, 0)),
        ],
        out_specs=[row_spec, row_spec],
        out_shape=[jax.ShapeDtypeStruct((t, d), BF16)] * 2,
        compiler_params=_cparams("parallel"),
        name="chan_dft",
    )(x, mod, cs)


def _seq_dft_kernel(c_ref, s_ref, a_ref, b_ref, o_ref, acc_ref):
    kk = pl.program_id(2)

    @pl.when(kk == 0)
    def _():
        acc_ref[...] = jnp.zeros_like(acc_ref)

    acc_ref[...] += (jnp.dot(c_ref[...], a_ref[...], preferred_element_type=F32)
                     - jnp.dot(s_ref[...], b_ref[...], preferred_element_type=F32))

    @pl.when(kk == pl.num_programs(2) - 1)
    def _():
        o_ref[...] = acc_ref[...].astype(o_ref.dtype)


def _seq_dft(a3, b3, length, row_off):
    b, s, d = a3.shape
    tile = min(DFT_TILE, length)
    cm, sm = _dft_matrices(length, length ** -0.5)
    nt = length // tile
    off = row_off // tile
    mat_spec = pl.BlockSpec((tile, tile), lambda bi, m, k: (m, k))
    in_spec = pl.BlockSpec((None, tile, d), lambda bi, m, k: (bi, k + off, 0))
    return pl.pallas_call(
        _seq_dft_kernel,
        grid=(b, nt, nt),
        in_specs=[mat_spec, mat_spec, in_spec, in_spec],
        out_specs=pl.BlockSpec((None, tile, d), lambda bi, m, k: (bi, m, 0)),
        out_shape=jax.ShapeDtypeStruct((b, length, d), BF16),
        scratch_shapes=[pltpu.VMEM((tile, d), F32)],
        compiler_params=_cparams("parallel", "parallel", "arbitrary"),
        name="seq_dft",
    )(cm, sm, a3, b3)


def _glu_kernel(x_ref, mod_ref, w_ref, b_ref, o_ref):
    d = x_ref.shape[1]
    u = (x_ref[...] * (1.0 + mod_ref[1:2, :]) + mod_ref[0:1, :]).astype(BF16)
    a = jnp.dot(u, w_ref[:, :d], preferred_element_type=F32) + b_ref[:, :d]
    g = jnp.dot(u, w_ref[:, d:], preferred_element_type=F32) + b_ref[:, d:]
    o_ref[...] = a * _sigmoid(g)


def _glu(x, mod, w, bias, nl, nc):
    t, d = x.shape
    tm = ROW_BLOCK
    row_spec = pl.BlockSpec((tm, d), lambda r: (r, 0))
    return pl.pallas_call(
        _glu_kernel,
        grid=(t // tm,),
        in_specs=[
            row_spec,
            pl.BlockSpec((None, 6, d), lambda r, f=_mod_row_map(nl, nc): (f(r), 0, 0)),
            pl.BlockSpec((d, 2 * d), lambda r: (0, 0)),
            pl.BlockSpec((1, 2 * d), lambda r: (0, 0)),
        ],
        out_specs=row_spec,
        out_shape=jax.ShapeDtypeStruct((t, d), F32),
        compiler_params=_cparams("parallel"),
        name="pw1_glu",
    )(x, mod, w.astype(BF16), bias.reshape(1, 2 * d))


def _dwconv_kernel(nl, nc, prev_ref, cur_ref, next_ref, w_ref, b_ref, g_ref, beta_ref, o_ref, win_ref, h_ref):
    tm, d = cur_ref.shape
    rr = pl.program_id(0) % (nl + nc)
    has_prev = jnp.logical_and(rr != 0, rr != nl)
    has_next = jnp.logical_and(rr != nl - 1, rr != nl + nc - 1)
    win_ref[0, 0:HALO, :] = jnp.where(has_prev, prev_ref[...], 0.0)
    win_ref[0, HALO:HALO + tm, :] = cur_ref[...]
    win_ref[0, HALO + tm:, :] = jnp.where(has_next, next_ref[...], 0.0)
    span = tm + 2 * HALO - SUBLANES
    for s in range(1, SUBLANES):
        win_ref[s, 0:span, :] = win_ref[0, s:s + span, :]
    rows = 64
    for c in range(d // LANES):
        cs = slice(c * LANES, (c + 1) * LANES)
        wc = w_ref[:, cs]
        for r0 in range(0, tm, rows):
            acc = jnp.broadcast_to(b_ref[:, cs], (rows, LANES))
            for j in range(CONV_WIDTH):
                off = HALO - CONV_PAD + j
                start = r0 + off - off % SUBLANES
                acc = acc + wc[j:j + 1, :] * win_ref[off % SUBLANES, start:start + rows, cs]
            h_ref[r0:r0 + rows, cs] = acc
    y = _layer_norm(h_ref[...], g_ref[...], beta_ref[...])
    o_ref[...] = (y * _sigmoid(y)).astype(BF16)


def _dwconv_ln_silu(hid, w_dw, b_dw, ln_g, ln_b, nl, nc):
    t, d = hid.shape
    tm = ROW_BLOCK
    per = tm // HALO
    n_halo = t // HALO
    vec = pl.BlockSpec((1, d), lambda r: (0, 0))
    return pl.pallas_call(
        functools.partial(_dwconv_kernel, nl, nc),
        grid=(t // tm,),
        in_specs=[
            pl.BlockSpec((HALO, d), lambda r: (jnp.maximum(r * per - 1, 0), 0)),
            pl.BlockSpec((tm, d), lambda r: (r, 0)),
            pl.BlockSpec((HALO, d), lambda r: (jnp.minimum((r + 1) * per, n_halo - 1), 0)),
            pl.BlockSpec((CONV_WIDTH, d), lambda r: (0, 0)),
            vec, vec, vec,
        ],
        out_specs=pl.BlockSpec((tm, d), lambda r: (r, 0)),
        out_shape=jax.ShapeDtypeStruct((t, d), BF16),
        scratch_shapes=[pltpu.VMEM((SUBLANES, tm + 2 * HALO, d), F32), pltpu.VMEM((tm, d), F32)],
        compiler_params=_cparams("parallel"),
        name="dwconv_ln_silu",
    )(hid, hid, hid, w_dw, b_dw.reshape(1, d), ln_g.reshape(1, d), ln_b.reshape(1, d))


def _pack_cols(cols, dtype):
    rows = cols[0].shape[0]
    lane = lax.broadcasted_iota(jnp.int32, (rows, len(cols)), 1)
    out = jnp.zeros((rows, len(cols)), dtype)
    for i, col in enumerate(cols):
        out = jnp.where(lane == i, col.astype(dtype), out)
    return out


def _proj_route_kernel(alpha, a_ref, w_ref, bias_ref, x_ref, mod_ref, g_ref, beta_ref, wr_ref, br_ref,
                       x1_ref, v_ref, idx_ref, gate_ref, rank_ref, cnt_ref):
    tm = x_ref.shape[0]
    n_exp = wr_ref.shape[2]

    y = jnp.dot(a_ref[...], w_ref[...], preferred_element_type=F32) + bias_ref[...]
    x1 = _layer_norm(alpha * x_ref[...] + mod_ref[2:3, :] * y, g_ref[...], beta_ref[...])
    x1_ref[...] = x1
    v = x1 * (1.0 + mod_ref[4:5, :]) + mod_ref[3:4, :]
    v_ref[...] = v

    v_hi = v.astype(BF16)
    v_lo = (v - v_hi.astype(F32)).astype(BF16)
    logits = (jnp.dot(v_hi, wr_ref[0], preferred_element_type=F32)
              + jnp.dot(v_lo, wr_ref[0], preferred_element_type=F32)
              + jnp.dot(v_hi, wr_ref[1], preferred_element_type=F32)) + br_ref[...]
    col = lax.broadcasted_iota(jnp.int32, logits.shape, 1).astype(F32)
    work = logits
    vals, idxs, sels = [], [], []
    for _ in range(TOP_K):
        mx = jnp.max(work, axis=1, keepdims=True)
        first = jnp.min(jnp.where(work == mx, col, float(n_exp)), axis=1, keepdims=True)
        sel = col == first
        work = jnp.where(sel, -jnp.inf, work)
        vals.append(mx)
        idxs.append(first)
        sels.append(sel)
    es = [jnp.exp(val - vals[0]) for val in vals]
    inv = 1.0 / (es[0] + es[1] + es[2] + es[3])

    onehot = jnp.zeros(logits.shape, F32)
    for sel in sels:
        onehot = onehot + jnp.where(sel, 1.0, 0.0)
    ri = lax.broadcasted_iota(jnp.int32, (tm, tm), 0)
    ci = lax.broadcasted_iota(jnp.int32, (tm, tm), 1)
    tri = jnp.where(ci < ri, 1.0, 0.0).astype(BF16)
    before = jnp.dot(tri, onehot.astype(BF16), preferred_element_type=F32)
    ranks = [jnp.sum(jnp.where(sel, before, 0.0), axis=1, keepdims=True) for sel in sels]

    idx_ref[...] = _pack_cols(idxs, jnp.int32)
    gate_ref[...] = _pack_cols([e * inv for e in es], F32)
    rank_ref[...] = _pack_cols(ranks, jnp.int32)
    cnt_ref[...] = jnp.sum(onehot, axis=0, keepdims=True).astype(jnp.int32)


def _proj_route(a, w, bias, x, mod, ln_g, ln_b, w_router, b_router, alpha, nl, nc):
    t, d = x.shape
    n_exp = w_router.shape[1]
    tm = ROW_BLOCK
    row_spec = pl.BlockSpec((tm, d), lambda r: (r, 0))
    vec = pl.BlockSpec((1, d), lambda r: (0, 0))
    k_spec = pl.BlockSpec((tm, TOP_K), lambda r: (r, 0))
    wr_hi = w_router.astype(BF16)
    return pl.pallas_call(
        functools.partial(_proj_route_kernel, alpha),
        grid=(t // tm,),
        in_specs=[
            row_spec,
            pl.BlockSpec((d, d), lambda r: (0, 0)),
            vec,
            row_spec,
            pl.BlockSpec((None, 6, d), lambda r, f=_mod_row_map(nl, nc): (f(r), 0, 0)),
            vec, vec,
            pl.BlockSpec((2, d, n_exp), lambda r: (0, 0, 0)),
            pl.BlockSpec((1, n_exp), lambda r: (0, 0)),
        ],
        out_specs=[row_spec, row_spec, k_spec, k_spec, k_spec,
                   pl.BlockSpec((None, 1, n_exp), lambda r: (r, 0, 0))],
        out_shape=[
            jax.ShapeDtypeStruct((t, d), F32),
            jax.ShapeDtypeStruct((t, d), F32),
            jax.ShapeDtypeStruct((t, TOP_K), jnp.int32),
            jax.ShapeDtypeStruct((t, TOP_K), F32),
            jax.ShapeDtypeStruct((t, TOP_K), jnp.int32),
            jax.ShapeDtypeStruct((t // tm, 1, n_exp), jnp.int32),
        ],
        compiler_params=_cparams("parallel"),
        name="proj_ln_route",
    )(a, w.astype(BF16), bias.reshape(1, d), x, mod, ln_g.reshape(1, d), ln_b.reshape(1, d),
      jnp.stack([wr_hi, (w_router - wr_hi.astype(F32)).astype(BF16)]), b_router.reshape(1, n_exp))


def _local_rows(n_exp):
    return TOP_K * ROW_BLOCK + n_exp * SUBLANES


def _local_pos(idx_ref, rank_ref, off_ref):
    tm = idx_ref.shape[0]
    col = lax.broadcasted_iota(jnp.int32, (tm, off_ref.shape[1]), 1)
    idx = idx_ref[...]
    rank = rank_ref[...]
    pos = []
    for k in range(TOP_K):
        start = jnp.sum(jnp.where(col == idx[:, k:k + 1], off_ref[...], 0.0), axis=1, keepdims=True)
        pos.append(start + rank[:, k:k + 1].astype(F32))
    return pos


def _dispatch_kernel(tail_ref, pad_ref, nu_ref, tab_ref, v_ref, idx_ref, rank_ref, off_ref, xb_ref,
                     srt_ref, zero_ref, sem, fill_sem):
    tm = v_ref.shape[0]
    rows = srt_ref.shape[0]
    bm = zero_ref.shape[0]
    n_blocks = xb_ref.shape[0] // bm

    @pl.when(pl.program_id(0) == 0)
    def _():
        zero_ref[...] = jnp.zeros_like(zero_ref)

        def padding(do):
            for e in range(tail_ref.shape[0]):
                pad = pad_ref[e]
                for bit in reversed(range(SUBLANES.bit_length() - 1, bm.bit_length() - 1)):
                    size = 1 << bit
                    off = pl.multiple_of(tail_ref[e] + ((pad >> (bit + 1)) << (bit + 1)), SUBLANES)
                    copy = pltpu.make_async_copy(zero_ref.at[pl.ds(0, size)], xb_ref.at[pl.ds(off, size)], fill_sem)
                    pl.when((pad & size) != 0)(functools.partial(do, copy))

        def blocks(do):
            def body(j, carry):
                do(pltpu.make_async_copy(zero_ref, xb_ref.at[pl.ds(pl.multiple_of(j * bm, bm), bm)], fill_sem))
                return carry
            lax.fori_loop(nu_ref[0], n_blocks, body, 0)

        padding(lambda copy: copy.start())
        blocks(lambda copy: copy.start())
        padding(lambda copy: copy.wait())
        blocks(lambda copy: copy.wait())

    pos = _local_pos(idx_ref, rank_ref, off_ref)
    lane = lax.broadcasted_iota(jnp.int32, (tm, LANES), 1)
    pos_cols = jnp.zeros((tm, LANES), F32)
    for k in range(TOP_K):
        pos_cols = jnp.where(lane == k, pos[k], pos_cols)
    pos_rows = pos_cols.T
    row = lax.broadcasted_iota(jnp.int32, (rows, tm), 0).astype(F32)
    hit = row == pos_rows[0:1, :]
    for k in range(1, TOP_K):
        hit = jnp.logical_or(hit, row == pos_rows[k:k + 1, :])
    perm = jnp.where(hit, 1.0, 0.0).astype(BF16)
    srt_ref[...] = jnp.dot(perm, v_ref[...].astype(BF16), preferred_element_type=F32)

    n_tiles = tab_ref[TILE_TABLE - 1]

    def tile_copy(i):
        return pltpu.make_async_copy(
            srt_ref.at[pl.ds(pl.multiple_of(i * SUBLANES, SUBLANES), SUBLANES)],
            xb_ref.at[pl.ds(pl.multiple_of(tab_ref[i] * SUBLANES, SUBLANES), SUBLANES)], sem)

    def issue(i, carry):
        tile_copy(i).start()
        return carry

    def drain(i, carry):
        tile_copy(i).wait()
        return carry

    lax.fori_loop(0, n_tiles, issue, 0)
    lax.fori_loop(0, n_tiles, drain, 0)


def _dispatch(v, idx, rank, plan, p_rows):
    t, d = v.shape
    tm = ROW_BLOCK
    n_exp = plan["loc_off"].shape[2]
    k_spec = pl.BlockSpec((tm, TOP_K), lambda r, *_: (r, 0))
    grid_spec = pltpu.PrefetchScalarGridSpec(
        num_scalar_prefetch=3,
        grid=(t // tm,),
        in_specs=[
            pl.BlockSpec((TILE_TABLE,), lambda r, *_: (r,), memory_space=pltpu.SMEM),
            pl.BlockSpec((tm, d), lambda r, *_: (r, 0)),
            k_spec, k_spec,
            pl.BlockSpec((None, 1, n_exp), lambda r, *_: (r, 0, 0)),
        ],
        out_specs=pl.BlockSpec(memory_space=pl.ANY),
        scratch_shapes=[pltpu.VMEM((_local_rows(n_exp), d), F32), pltpu.VMEM((EXPERT_BLOCK, d), F32),
                        pltpu.SemaphoreType.DMA, pltpu.SemaphoreType.DMA],
    )
    return pl.pallas_call(
        _dispatch_kernel,
        grid_spec=grid_spec,
        out_shape=jax.ShapeDtypeStruct((p_rows, d), F32),
        compiler_params=_cparams("arbitrary"),
        name="moe_dispatch",
    )(plan["tail_start"], plan["pad_len"], plan["n_used"], plan["tile_table"], v, idx, rank, plan["loc_off"])


def _expert_kernel(be_ref, nu_ref, x_ref, w1_ref, b1_ref, w2_ref, b2_ref, o_ref, w1b_ref, w2b_ref):
    j = pl.program_id(0)
    f = w2_ref.shape[0]
    prev = be_ref[jnp.maximum(j - 1, 0)]

    @pl.when(jnp.logical_and(j < nu_ref[0], jnp.logical_or(j == 0, be_ref[j] != prev)))
    def _():
        w1b_ref[...] = w1_ref[...].astype(BF16)
        w2b_ref[...] = w2_ref[...].astype(BF16)

    @pl.when(j < nu_ref[0])
    def _():
        h = jnp.dot(x_ref[...].astype(BF16), w1b_ref[...], preferred_element_type=F32) + b1_ref[...]
        glu = jnp.minimum(h[:, :f], SWIGLU_LIMIT)
        lin = jnp.clip(h[:, f:], -SWIGLU_LIMIT, SWIGLU_LIMIT)
        act = glu * _sigmoid(SWIGLU_ALPHA * glu) * (lin + 1.0)
        o_ref[...] = jnp.dot(act.astype(BF16), w2b_ref[...], preferred_element_type=F32) + b2_ref[...]

    @pl.when(j >= nu_ref[0])
    def _():
        o_ref[...] = jnp.zeros_like(o_ref)


def _experts(xb, block_e, n_used, layer, w1, b1, w2, b2):
    d = xb.shape[1]
    depth, n_exp, _, f2 = w1.shape
    f = f2 // 2
    bm = EXPERT_BLOCK
    n_blocks = block_e.shape[0]
    last = lambda j, nu: jnp.minimum(j, nu[0] - 1)
    grid_spec = pltpu.PrefetchScalarGridSpec(
        num_scalar_prefetch=2,
        grid=(n_blocks,),
        in_specs=[
            pl.BlockSpec((bm, d), lambda j, be, nu: (last(j, nu), 0)),
            pl.BlockSpec((None, None, d, f2), lambda j, be, nu: (layer, be[last(j, nu)], 0, 0)),
            pl.BlockSpec((None, None, 1, f2), lambda j, be, nu: (layer, be[last(j, nu)], 0, 0)),
            pl.BlockSpec((None, None, f, d), lambda j, be, nu: (layer, be[last(j, nu)], 0, 0)),
            pl.BlockSpec((None, None, 1, d), lambda j, be, nu: (layer, be[last(j, nu)], 0, 0)),
        ],
        out_specs=pl.BlockSpec((bm, d), lambda j, be, nu: (j, 0)),
        scratch_shapes=[pltpu.VMEM((d, f2), BF16), pltpu.VMEM((f, d), BF16)],
    )
    return pl.pallas_call(
        _expert_kernel,
        grid_spec=grid_spec,
        out_shape=jax.ShapeDtypeStruct((n_blocks * bm, d), F32),
        compiler_params=_cparams("arbitrary"),
        name="moe_experts",
    )(block_e, n_used, xb, w1, b1.reshape(depth, n_exp, 1, f2), w2, b2.reshape(depth, n_exp, 1, d))


def _combine_kernel(alpha, tab_ref, next_tab_ref, idx_ref, rank_ref, off_ref, gate_ref, x1_ref, mod_ref,
                    g_ref, beta_ref, yb_ref, o_ref, rows_ref, sems):
    r = pl.program_id(0)
    tm = x1_ref.shape[0]
    rows = rows_ref.shape[1]
    slot = r % 2

    def tile_copy(t_ref, into, i):
        return pltpu.make_async_copy(
            yb_ref.at[pl.ds(pl.multiple_of(t_ref[i] * SUBLANES, SUBLANES), SUBLANES)],
            rows_ref.at[into, pl.ds(pl.multiple_of(i * SUBLANES, SUBLANES), SUBLANES)], sems.at[into])

    def gather(t_ref, into):
        def issue(i, carry):
            tile_copy(t_ref, into, i).start()
            return carry
        lax.fori_loop(0, t_ref[TILE_TABLE - 1], issue, 0)

    @pl.when(r == 0)
    def _():
        gather(tab_ref, 0)

    @pl.when(r + 1 < pl.num_programs(0))
    def _():
        gather(next_tab_ref, 1 - slot)

    n_tiles = tab_ref[TILE_TABLE - 1]

    def drain(i, carry):
        tile_copy(tab_ref, slot, i).wait()
        return carry

    lax.fori_loop(0, n_tiles, drain, 0)

    pos = _local_pos(idx_ref, rank_ref, off_ref)
    lane = lax.broadcasted_iota(jnp.int32, (tm, rows), 1).astype(F32)
    gates = gate_ref[...]
    unperm = jnp.where(lane == pos[0], gates[:, 0:1], 0.0)
    for k in range(1, TOP_K):
        unperm = unperm + jnp.where(lane == pos[k], gates[:, k:k + 1], 0.0)
    row = lax.broadcasted_iota(jnp.int32, (rows, 1), 0)
    y = jnp.where(row < n_tiles * SUBLANES, rows_ref[slot], 0.0).astype(BF16)
    f = jnp.dot(unperm.astype(BF16), y, preferred_element_type=F32)
    o_ref[...] = _layer_norm(alpha * x1_ref[...] + mod_ref[5:6, :] * f, g_ref[...], beta_ref[...])


def _combine(idx, rank, gates, x1, mod, ln_g, ln_b, yb, plan, alpha, nl, nc):
    t, d = x1.shape
    tm = ROW_BLOCK
    n_exp = plan["loc_off"].shape[2]
    row_spec = pl.BlockSpec((tm, d), lambda r: (r, 0))
    vec = pl.BlockSpec((1, d), lambda r: (0, 0))
    k_spec = pl.BlockSpec((tm, TOP_K), lambda r: (r, 0))
    return pl.pallas_call(
        functools.partial(_combine_kernel, alpha),
        grid=(t // tm,),
        in_specs=[
            pl.BlockSpec((TILE_TABLE,), lambda r: (r,), memory_space=pltpu.SMEM),
            pl.BlockSpec((TILE_TABLE,), lambda r: (jnp.minimum(r + 1, t // tm - 1),), memory_space=pltpu.SMEM),
            k_spec, k_spec,
            pl.BlockSpec((None, 1, n_exp), lambda r: (r, 0, 0)),
            k_spec,
            row_spec,
            pl.BlockSpec((None, 6, d), lambda r, f=_mod_row_map(nl, nc): (f(r), 0, 0)),
            vec, vec,
            pl.BlockSpec(memory_space=pl.ANY),
        ],
        out_specs=row_spec,
        out_shape=jax.ShapeDtypeStruct((t, d), F32),
        scratch_shapes=[pltpu.VMEM((2, _local_rows(n_exp), d), F32), pltpu.SemaphoreType.DMA((2,))],
        compiler_params=_cparams("arbitrary"),
        name="moe_combine_ln",
    )(plan["tile_table"], plan["tile_table"], idx, rank, plan["loc_off"], gates, x1, mod,
      ln_g.reshape(1, d), ln_b.reshape(1, d), yb)


def _moe_plan(cnt, n_blocks):
    nblk, n_exp = cnt.shape
    bm = EXPERT_BLOCK
    n_tiles_max = _local_rows(n_exp) // SUBLANES
    experts = jnp.arange(n_exp)
    blocks = jnp.arange(nblk)
    seg = (cnt + SUBLANES - 1) // SUBLANES * SUBLANES
    loc_off = jnp.sum(jnp.where(experts[None, None, :] < experts[None, :, None], seg[:, None, :], 0), axis=2)
    before = jnp.sum(jnp.where(blocks[None, :, None] < blocks[:, None, None], seg[None, :, :], 0), axis=1)
    rows_e = jnp.sum(seg, axis=0)
    padded = (rows_e + bm - 1) // bm * bm
    pend = jnp.sum(jnp.where(experts[None, :] <= experts[:, None], padded[None, :], 0), axis=1)
    pstart = pend - padded
    base = pstart[None, :] + before
    first_row = (jnp.arange(n_tiles_max) * SUBLANES)[None, :, None]
    e_of = jnp.sum(((loc_off + seg)[:, None, :] <= first_row).astype(jnp.int32), axis=2)
    pick = jnp.minimum(e_of, n_exp - 1)[:, :, None] == experts
    delta = jnp.sum(jnp.where(pick, (base - loc_off)[:, None, :], 0), axis=2)
    tile_of = (delta + first_row[:, :, 0]) // SUBLANES
    table = jnp.concatenate([tile_of.astype(jnp.int32),
                             jnp.zeros((nblk, TILE_TABLE - 1 - n_tiles_max), jnp.int32),
                             (jnp.sum(seg, axis=1, keepdims=True) // SUBLANES).astype(jnp.int32)], axis=1)
    block_e = jnp.sum((pend[None, :] <= (jnp.arange(n_blocks) * bm)[:, None]).astype(jnp.int32), axis=1)
    return dict(
        loc_off=loc_off.astype(F32).reshape(nblk, 1, n_exp),
        tile_table=table.reshape(-1),
        tail_start=(pstart + rows_e).astype(jnp.int32),
        pad_len=(padded - rows_e).astype(jnp.int32),
        block_e=jnp.minimum(block_e, n_exp - 1).astype(jnp.int32),
        n_used=(pend[-1:] // bm).astype(jnp.int32),
    )


def kernel(x, c, ctx, c_ctx, w_mod, b_mod, ln1_g, ln1_b, ln2_g, ln2_b, attn_w_qkv, attn_w_o, attn_lam_q1, attn_lam_k1, attn_lam_q2, attn_lam_k2, attn_subln_g, fnet_w, fnet_b, conv_w_pw1, conv_b_pw1, conv_w_dw, conv_b_dw, conv_ln_g, conv_ln_b, conv_w_pw2, conv_b_pw2, moe_w_router, moe_b_router, moe_w1, moe_b1, moe_w2, moe_b2):
    b, n, d = x.shape
    cl = ctx.shape[1]
    s = n + cl
    t = b * s
    depth = w_mod.shape[0]
    n_exp = moe_w_router.shape[2]
    nl, nc = n // ROW_BLOCK, cl // ROW_BLOCK
    alpha = (2 * depth) ** 0.25
    worst_rows = t * TOP_K + (t // ROW_BLOCK) * n_exp * (SUBLANES - 1)
    n_blocks = -(-worst_rows // EXPERT_BLOCK) + n_exp
    p_rows = n_blocks * EXPERT_BLOCK

    xs = jnp.concatenate([x, ctx], axis=1).reshape(t, d)
    cond = jnp.stack([c, jnp.broadcast_to(c_ctx, c.shape)], axis=1).reshape(2 * b, d)
    mod = _modulation(cond, w_mod, b_mod)
    cos, sin = _rope_tables(n, cl)

    for i in range(depth):
        kind, j = i % 3, i // 3
        if kind == 0:
            lam_init = 0.8 - 0.6 * float(np.exp(-0.3 * i))
            q, k, v = _qkv(xs, mod[i], attn_w_qkv[j], cos, sin, nl, nc)
            q3, k3 = q.reshape(b, s, d), k.reshape(b, s, d)
            vt = v.reshape(b, s, d // HEAD_W, HEAD_W).transpose(0, 2, 3, 1)
            vt = jnp.concatenate([vt, jnp.ones((b, d // HEAD_W, SUM_ROWS, s), BF16)], axis=2)
            lam_params = (attn_lam_q1[j], attn_lam_k1[j], attn_lam_q2[j], attn_lam_k2[j])
            o_lat = _attention(q3, k3, vt, lam_params, attn_subln_g[j], lam_init, n, cl, context=False)
            o_ctx = _attention(q3, k3, vt, lam_params, attn_subln_g[j], lam_init, n, cl, context=True)
            a = jnp.concatenate([o_lat, o_ctx], axis=1).reshape(t, d)
            w_out, b_out = attn_w_o[j], jnp.zeros((d,), F32)
        elif kind == 1:
            fa, fb = _chan_dft(xs, mod[i], nl, nc)
            fa3, fb3 = fa.reshape(b, s, d), fb.reshape(b, s, d)
            a = jnp.concatenate([_seq_dft(fa3, fb3, n, 0), _seq_dft(fa3, fb3, cl, n)], axis=1).reshape(t, d)
            w_out, b_out = fnet_w[j], fnet_b[j]
        else:
            hid = _glu(xs, mod[i], conv_w_pw1[j], conv_b_pw1[j], nl, nc)
            a = _dwconv_ln_silu(hid, conv_w_dw[j], conv_b_dw[j], conv_ln_g[j], conv_ln_b[j], nl, nc)
            w_out, b_out = conv_w_pw2[j], conv_b_pw2[j]

        x1, v, idx, gates, rank, cnt = _proj_route(
            a, w_out, b_out, xs, mod[i], ln1_g[i], ln1_b[i], moe_w_router[i], moe_b_router[i], alpha, nl, nc)
        plan = _moe_plan(cnt[:, 0, :], n_blocks)
        xb = _dispatch(v, idx, rank, plan, p_rows)
        yb = _experts(xb, plan["block_e"], plan["n_used"], i, moe_w1, moe_b1, moe_w2, moe_b2)
        xs = _combine(idx, rank, gates, x1, mod[i], ln2_g[i], ln2_b[i], yb, plan, alpha, nl, nc)

    return xs.reshape(b, s, d)[:, :n]
```

```python
import functools

import jax
import jax.numpy as jnp
import numpy as np
from jax import lax
from jax.experimental import pallas as pl
from jax.experimental.pallas import tpu as pltpu

F32 = jnp.float32
BF16 = jnp.bfloat16
HIGHEST = lax.Precision.HIGHEST

GRID_W = 64
HEAD_W = 128
MAP_W = HEAD_W // 2
ROPE_FREQS = MAP_W // 4
ROPE_BASE = 10000.0
CONV_WIDTH = 31
CONV_PAD = CONV_WIDTH // 2
TOP_K = 4
SWIGLU_ALPHA = 1.702
SWIGLU_LIMIT = 7.0
LN_EPS = 1e-5

LANES = 128
SUBLANES = 8
ROW_BLOCK = 256
SUM_ROWS = 8
HALO = 16
ATTN_TQ = 512
ATTN_TK = 512
ATTN_QC = 256
DFT_TILE = 1024
EXPERT_BLOCK = 512
TILE_TABLE = 1024
VMEM_LIMIT = 56 * 1024 * 1024


def _cparams(*sem):
    return pltpu.CompilerParams(dimension_semantics=sem, vmem_limit_bytes=VMEM_LIMIT)


def _mod_row_map(nl, nc):
    rb = nl + nc

    def f(r):
        return 2 * (r // rb) + jnp.where((r % rb) >= nl, 1, 0)

    return f


def _layer_norm(z, g, b):
    mu = jnp.mean(z, axis=-1, keepdims=True)
    zc = z - mu
    var = jnp.mean(zc * zc, axis=-1, keepdims=True)
    return zc * lax.rsqrt(var + LN_EPS) * g + b


def _sigmoid(x):
    return 1.0 / (1.0 + jnp.exp(-x))


def _modulation_kernel(cond_ref, w_ref, b_ref, o_ref):
    cnd = cond_ref[...]
    s = cnd * _sigmoid(cnd)
    o_ref[...] = jnp.dot(s, w_ref[...], precision=HIGHEST, preferred_element_type=F32) + b_ref[...]


def _modulation(cond, w_mod, b_mod):
    depth, d, d6 = w_mod.shape
    rows = cond.shape[0]
    out = pl.pallas_call(
        _modulation_kernel,
        grid=(depth, d6 // d),
        in_specs=[
            pl.BlockSpec((rows, d), lambda i, j: (0, 0)),
            pl.BlockSpec((None, d, d), lambda i, j: (i, 0, j)),
            pl.BlockSpec((None, 1, d), lambda i, j: (i, 0, j)),
        ],
        out_specs=pl.BlockSpec((None, rows, d), lambda i, j: (i, 0, j)),
        out_shape=jax.ShapeDtypeStruct((depth, rows, d6), F32),
        compiler_params=_cparams("parallel", "parallel"),
        name="modulation",
    )(cond, w_mod, b_mod.reshape(depth, 1, d6))
    return out.reshape(depth, rows, d6 // d, d)


def _rope_tables(n, c):
    pos = jnp.arange(n)
    rows = (pos // GRID_W).astype(F32)
    cols = (pos % GRID_W).astype(F32)
    inv_freq = ROPE_BASE ** (-jnp.arange(ROPE_FREQS, dtype=F32) / ROPE_FREQS)
    lane = np.arange(HEAD_W)
    dim = lane % MAP_W
    freq = dim % ROPE_FREQS
    use_row = (dim // (MAP_W // 2)) == 0
    first_half = (dim % (MAP_W // 2)) < ROPE_FREQS
    ang = jnp.where(use_row[None, :], rows[:, None] * inv_freq[freq][None, :],
                    cols[:, None] * inv_freq[freq][None, :])
    cos = jnp.cos(ang)
    sin = jnp.sin(ang)
    sin = jnp.where(first_half[None, :], -sin, sin)
    cos = jnp.concatenate([cos, jnp.ones((c, HEAD_W), F32)], axis=0)
    sin = jnp.concatenate([sin, jnp.zeros((c, HEAD_W), F32)], axis=0)
    return cos, sin


def _qkv_kernel(x_ref, mod_ref, w_ref, cos_ref, sin_ref, q_ref, k_ref, v_ref):
    d = x_ref.shape[1]
    u = (x_ref[...] * (1.0 + mod_ref[1:2, :]) + mod_ref[0:1, :]).astype(BF16)
    cos = cos_ref[...]
    sin = sin_ref[...]
    lane = lax.broadcasted_iota(jnp.int32, cos.shape, 1)
    first_half = (lane % (MAP_W // 2)) < ROPE_FREQS

    def rope(t):
        partner = jnp.where(first_half, pltpu.roll(t, HEAD_W - ROPE_FREQS, 1), pltpu.roll(t, ROPE_FREQS, 1))
        return t * cos + partner * sin

    q = jnp.dot(u, w_ref[:, :d], preferred_element_type=F32)
    k = jnp.dot(u, w_ref[:, d:2 * d], preferred_element_type=F32)
    for h in range(d // HEAD_W):
        cs = slice(h * HEAD_W, (h + 1) * HEAD_W)
        q_ref[:, cs] = rope(q[:, cs]).astype(BF16)
        k_ref[:, cs] = rope(k[:, cs]).astype(BF16)
    v_ref[...] = jnp.dot(u, w_ref[:, 2 * d:], preferred_element_type=F32).astype(BF16)


def _qkv(x, mod, w_qkv, cos, sin, nl, nc):
    t, d = x.shape
    rb = nl + nc
    tm = ROW_BLOCK
    scale = jnp.concatenate([jnp.full((d,), np.log2(np.e) * MAP_W ** -0.5, F32), jnp.ones((2 * d,), F32)])
    w = (w_qkv * scale[None, :]).astype(BF16)
    row_spec = pl.BlockSpec((tm, d), lambda r: (r, 0))
    tab_spec = pl.BlockSpec((tm, HEAD_W), lambda r: (r % rb, 0))
    return pl.pallas_call(
        _qkv_kernel,
        grid=(t // tm,),
        in_specs=[
            row_spec,
            pl.BlockSpec((None, 6, d), lambda r, f=_mod_row_map(nl, nc): (f(r), 0, 0)),
            pl.BlockSpec((d, 3 * d), lambda r: (0, 0)),
            tab_spec, tab_spec,
        ],
        out_specs=[row_spec, row_spec, row_spec],
        out_shape=[jax.ShapeDtypeStruct((t, d), BF16)] * 3,
        compiler_params=_cparams("parallel"),
        name="qkv_rope",
    )(x, mod, w, cos, sin)


def _attn_kernel(lam_init, tiles, q_ref, k_ref, vt_ref, lq1_ref, lk1_ref, lq2_ref, lk2_ref, g_ref,
                 o_ref, acc1_ref, acc2_ref, sa1_ref, sa2_ref, sb1_ref, sb2_ref):
    tq = q_ref.shape[0]
    q = q_ref[...].astype(F32)
    lane = lax.broadcasted_iota(jnp.int32, q.shape, 1)
    q1 = jnp.where(lane < MAP_W, q, 0.0).astype(BF16)
    q2 = jnp.where(lane >= MAP_W, q, 0.0).astype(BF16)
    acc1_ref[...] = jnp.zeros_like(acc1_ref)
    acc2_ref[...] = jnp.zeros_like(acc2_ref)
    acc_refs = (acc1_ref, acc2_ref)
    s_refs = ((sa1_ref, sa2_ref), (sb1_ref, sb2_ref))

    qc = min(tq, ATTN_QC)
    pieces = [(mp, c) for c in range(tq // qc) for mp in range(2)]
    qms = (q1, q2)

    def score_piece(i, mp, c):
        off, size = tiles[i]
        s = lax.dot_general(k_ref[off:off + size, :], qms[mp][c * qc:(c + 1) * qc, :],
                            (((1,), (1,)), ((), ())), preferred_element_type=F32)
        s_refs[i % 2][mp][0:size, c * qc:(c + 1) * qc] = s
        return jnp.max(s, axis=0, keepdims=True)

    def acc_piece(i, mp, c, cm, m):
        off, size = tiles[i]
        cs = slice(c * qc, (c + 1) * qc)
        m_new = jnp.maximum(m, cm)
        p = jnp.exp2(s_refs[i % 2][mp][0:size, cs] - m_new).astype(BF16)
        acc_refs[mp][:, cs] = (jnp.exp2(m - m_new) * acc_refs[mp][:, cs]
                               + jnp.dot(vt_ref[:, off:off + size], p, preferred_element_type=F32))
        return m_new

    def stage(i_score, i_acc, cm_acc, ms):
        cm_new, ms_new = [], []
        for n_piece, (mp, c) in enumerate(pieces):
            if i_score is not None:
                cm_new.append(score_piece(i_score, mp, c))
            if i_acc is not None:
                ms_new.append(acc_piece(i_acc, mp, c, cm_acc[n_piece], ms[n_piece]))
        return tuple(cm_new), tuple(ms_new) if i_acc is not None else ms

    ms = tuple(jnp.full((1, qc), -jnp.inf, F32) for _ in pieces)
    cm, _ = stage(0, None, None, ms)
    for i in range(len(tiles) - 1):
        cm, ms = stage(i + 1, i, cm, ms)
    stage(None, len(tiles) - 1, cm, ms)

    lam = (jnp.exp(jnp.sum(lq1_ref[...] * lk1_ref[...], axis=1, keepdims=True))
           - jnp.exp(jnp.sum(lq2_ref[...] * lk2_ref[...], axis=1, keepdims=True)) + lam_init)
    inv1 = 1.0 / acc1_ref[HEAD_W:HEAD_W + 1, :]
    inv2 = 1.0 / acc2_ref[HEAD_W:HEAD_W + 1, :]
    o = acc1_ref[0:HEAD_W, :] * inv1 - lam * (acc2_ref[0:HEAD_W, :] * inv2)
    ms = jnp.mean(o * o, axis=0, keepdims=True)
    o = o * lax.rsqrt(ms + LN_EPS) * (g_ref[...] * (1.0 - lam_init))
    o_ref[...] = o.T.astype(BF16)


def _attention(q3, k3, vt, lam_params, subln_g, lam_init, n, c, context):
    b, s, d = q3.shape
    h = d // HEAD_W
    if context:
        tq, kv_len, kv_blk, q_off, nq = c, c, n // c, n // c, 1
    else:
        tq, kv_len, kv_blk, q_off, nq = ATTN_TQ, s, 0, 0, n // ATTN_TQ
    tk = min(ATTN_TK, kv_len)
    tiles = tuple((off, min(tk, kv_len - off)) for off in range(0, kv_len, tk))
    lam_spec = pl.BlockSpec((1, MAP_W), lambda bi, hi, qi: (0, 0))
    return pl.pallas_call(
        functools.partial(_attn_kernel, lam_init, tiles),
        grid=(b, h, nq),
        in_specs=[
            pl.BlockSpec((None, tq, HEAD_W), lambda bi, hi, qi: (bi, qi + q_off, hi)),
            pl.BlockSpec((None, kv_len, HEAD_W), lambda bi, hi, qi: (bi, kv_blk, hi)),
            pl.BlockSpec((None, None, HEAD_W + SUM_ROWS, kv_len), lambda bi, hi, qi: (bi, hi, 0, kv_blk)),
            lam_spec, lam_spec, lam_spec, lam_spec,
            pl.BlockSpec((HEAD_W, 1), lambda bi, hi, qi: (0, 0)),
        ],
        out_specs=pl.BlockSpec((None, tq, HEAD_W), lambda bi, hi, qi: (bi, qi, hi)),
        out_shape=jax.ShapeDtypeStruct((b, tq * nq, d), BF16),
        scratch_shapes=[pltpu.VMEM((HEAD_W + SUM_ROWS, tq), F32)] * 2 + [pltpu.VMEM((tk, tq), F32)] * 4,
        compiler_params=_cparams("parallel", "parallel", "parallel"),
        name="attn_ctx" if context else "attn_lat",
    )(q3, k3, vt, *[p.reshape(1, MAP_W) for p in lam_params], subln_g.reshape(HEAD_W, 1))


def _dft_matrices(n, scale):
    blk = min(ROW_BLOCK, n)
    k = jnp.arange(n, dtype=jnp.int32)

    def cs(j):
        r = (j[:, None] * k[None, :]) % n
        ang = r.astype(F32) * (2.0 * np.pi / n)
        return jnp.cos(ang), jnp.sin(ang)

    c0, s0 = cs(jnp.arange(blk, dtype=jnp.int32))
    cj, sj = cs(jnp.arange(0, n, blk, dtype=jnp.int32))
    cm = cj[:, None, :] * c0[None] - sj[:, None, :] * s0[None]
    sm = sj[:, None, :] * c0[None] + cj[:, None, :] * s0[None]
    return (cm * scale).reshape(n, n).astype(BF16), (sm * scale).reshape(n, n).astype(BF16)


def _chan_dft_kernel(x_ref, mod_ref, cs_ref, a_ref, b_ref):
    d = x_ref.shape[1]
    u = (x_ref[...] * (1.0 + mod_ref[1:2, :]) + mod_ref[0:1, :]).astype(BF16)
    for g in range(d // LANES):
        sl = slice(g * LANES, (g + 1) * LANES)
        ab = jnp.dot(u[:, sl], cs_ref[...], preferred_element_type=F32)
        a_ref[:, sl] = ab[:, :LANES].astype(BF16)
        b_ref[:, sl] = ab[:, LANES:].astype(BF16)


def _chan_dft(x, mod, nl, nc):
    t, d = x.shape
    tm = ROW_BLOCK
    j = np.arange(LANES)
    ang = 2.0 * np.pi * ((j[:, None] * j[None, :]) % LANES) / LANES
    cs = jnp.asarray(np.concatenate([np.cos(ang), np.sin(ang)], axis=1) / np.sqrt(LANES), BF16)
    row_spec = pl.BlockSpec((tm, d), lambda r: (r, 0))
    return pl.pallas_call(
        _chan_dft_kernel,
        grid=(t // tm,),
        in_specs=[
            row_spec,
            pl.BlockSpec((None, 6, d), lambda r, f=_mod_row_map(nl, nc): (f(r), 0, 0)),
            pl.BlockSpec((LANES, 2 * LANES), lambda r: (0, 0)),
        ],
        out_specs=[row_spec, row_spec],
        out_shape=[jax.ShapeDtypeStruct((t, d), BF16)] * 2,
        compiler_params=_cparams("parallel"),
        name="chan_dft",
    )(x, mod, cs)


def _seq_dft_kernel(c_ref, s_ref, a_ref, b_ref, o_ref, acc_ref):
    kk = pl.program_id(2)

    @pl.when(kk == 0)
    def _():
        acc_ref[...] = jnp.zeros_like(acc_ref)

    acc_ref[...] += (jnp.dot(c_ref[...], a_ref[...], preferred_element_type=F32)
                     - jnp.dot(s_ref[...], b_ref[...], preferred_element_type=F32))

    @pl.when(kk == pl.num_programs(2) - 1)
    def _():
        o_ref[...] = acc_ref[...].astype(o_ref.dtype)


def _seq_dft(a3, b3, length, row_off):
    b, s, d = a3.shape
    tile = min(DFT_TILE, length)
    cm, sm = _dft_matrices(length, length ** -0.5)
    nt = length // tile
    off = row_off // tile
    mat_spec = pl.BlockSpec((tile, tile), lambda bi, m, k: (m, k))
    in_spec = pl.BlockSpec((None, tile, d), lambda bi, m, k: (bi, k + off, 0))
    return pl.pallas_call(
        _seq_dft_kernel,
        grid=(b, nt, nt),
        in_specs=[mat_spec, mat_spec, in_spec, in_spec],
        out_specs=pl.BlockSpec((None, tile, d), lambda bi, m, k: (bi, m, 0)),
        out_shape=jax.ShapeDtypeStruct((b, length, d), BF16),
        scratch_shapes=[pltpu.VMEM((tile, d), F32)],
        compiler_params=_cparams("parallel", "parallel", "arbitrary"),
        name="seq_dft",
    )(cm, sm, a3, b3)


def _glu_kernel(x_ref, mod_ref, w_ref, b_ref, o_ref):
    d = x_ref.shape[1]
    u = (x_ref[...] * (1.0 + mod_ref[1:2, :]) + mod_ref[0:1, :]).astype(BF16)
    a = jnp.dot(u, w_ref[:, :d], preferred_element_type=F32) + b_ref[:, :d]
    g = jnp.dot(u, w_ref[:, d:], preferred_element_type=F32) + b_ref[:, d:]
    o_ref[...] = a * _sigmoid(g)


def _glu(x, mod, w, bias, nl, nc):
    t, d = x.shape
    tm = ROW_BLOCK
    row_spec = pl.BlockSpec((tm, d), lambda r: (r, 0))
    return pl.pallas_call(
        _glu_kernel,
        grid=(t // tm,),
        in_specs=[
            row_spec,
            pl.BlockSpec((None, 6, d), lambda r, f=_mod_row_map(nl, nc): (f(r), 0, 0)),
            pl.BlockSpec((d, 2 * d), lambda r: (0, 0)),
            pl.BlockSpec((1, 2 * d), lambda r: (0, 0)),
        ],
        out_specs=row_spec,
        out_shape=jax.ShapeDtypeStruct((t, d), F32),
        compiler_params=_cparams("parallel"),
        name="pw1_glu",
    )(x, mod, w.astype(BF16), bias.reshape(1, 2 * d))


def _dwconv_kernel(nl, nc, prev_ref, cur_ref, next_ref, w_ref, b_ref, g_ref, beta_ref, o_ref, win_ref, h_ref):
    tm, d = cur_ref.shape
    rr = pl.program_id(0) % (nl + nc)
    has_prev = jnp.logical_and(rr != 0, rr != nl)
    has_next = jnp.logical_and(rr != nl - 1, rr != nl + nc - 1)
    win_ref[0, 0:HALO, :] = jnp.where(has_prev, prev_ref[...], 0.0)
    win_ref[0, HALO:HALO + tm, :] = cur_ref[...]
    win_ref[0, HALO + tm:, :] = jnp.where(has_next, next_ref[...], 0.0)
    span = tm + 2 * HALO - SUBLANES
    for s in range(1, SUBLANES):
        win_ref[s, 0:span, :] = win_ref[0, s:s + span, :]
    rows = 64
    for c in range(d // LANES):
        cs = slice(c * LANES, (c + 1) * LANES)
        wc = w_ref[:, cs]
        for r0 in range(0, tm, rows):
            acc = jnp.broadcast_to(b_ref[:, cs], (rows, LANES))
            for j in range(CONV_WIDTH):
                off = HALO - CONV_PAD + j
                start = r0 + off - off % SUBLANES
                acc = acc + wc[j:j + 1, :] * win_ref[off % SUBLANES, start:start + rows, cs]
            h_ref[r0:r0 + rows, cs] = acc
    y = _layer_norm(h_ref[...], g_ref[...], beta_ref[...])
    o_ref[...] = (y * _sigmoid(y)).astype(BF16)


def _dwconv_ln_silu(hid, w_dw, b_dw, ln_g, ln_b, nl, nc):
    t, d = hid.shape
    tm = ROW_BLOCK
    per = tm // HALO
    n_halo = t // HALO
    vec = pl.BlockSpec((1, d), lambda r: (0, 0))
    return pl.pallas_call(
        functools.partial(_dwconv_kernel, nl, nc),
        grid=(t // tm,),
        in_specs=[
            pl.BlockSpec((HALO, d), lambda r: (jnp.maximum(r * per - 1, 0), 0)),
            pl.BlockSpec((tm, d), lambda r: (r, 0)),
            pl.BlockSpec((HALO, d), lambda r: (jnp.minimum((r + 1) * per, n_halo - 1), 0)),
            pl.BlockSpec((CONV_WIDTH, d), lambda r: (0, 0)),
            vec, vec, vec,
        ],
        out_specs=pl.BlockSpec((tm, d), lambda r: (r, 0)),
        out_shape=jax.ShapeDtypeStruct((t, d), BF16),
        scratch_shapes=[pltpu.VMEM((SUBLANES, tm + 2 * HALO, d), F32), pltpu.VMEM((tm, d), F32)],
        compiler_params=_cparams("parallel"),
        name="dwconv_ln_silu",
    )(hid, hid, hid, w_dw, b_dw.reshape(1, d), ln_g.reshape(1, d), ln_b.reshape(1, d))


def _pack_cols(cols, dtype):
    rows = cols[0].shape[0]
    lane = lax.broadcasted_iota(jnp.int32, (rows, len(cols)), 1)
    out = jnp.zeros((rows, len(cols)), dtype)
    for i, col in enumerate(cols):
        out = jnp.where(lane == i, col.astype(dtype), out)
    return out


def _proj_route_kernel(alpha, a_ref, w_ref, bias_ref, x_ref, mod_ref, g_ref, beta_ref, wr_ref, br_ref,
                       x1_ref, v_ref, idx_ref, gate_ref, rank_ref, cnt_ref):
    tm = x_ref.shape[0]
    n_exp = wr_ref.shape[2]

    y = jnp.dot(a_ref[...], w_ref[...], preferred_element_type=F32) + bias_ref[...]
    x1 = _layer_norm(alpha * x_ref[...] + mod_ref[2:3, :] * y, g_ref[...], beta_ref[...])
    x1_ref[...] = x1
    v = x1 * (1.0 + mod_ref[4:5, :]) + mod_ref[3:4, :]
    v_ref[...] = v

    v_hi = v.astype(BF16)
    v_lo = (v - v_hi.astype(F32)).astype(BF16)
    logits = (jnp.dot(v_hi, wr_ref[0], preferred_element_type=F32)
              + jnp.dot(v_lo, wr_ref[0], preferred_element_type=F32)
              + jnp.dot(v_hi, wr_ref[1], preferred_element_type=F32)) + br_ref[...]
    col = lax.broadcasted_iota(jnp.int32, logits.shape, 1).astype(F32)
    work = logits
    vals, idxs, sels = [], [], []
    for _ in range(TOP_K):
        mx = jnp.max(work, axis=1, keepdims=True)
        first = jnp.min(jnp.where(work == mx, col, float(n_exp)), axis=1, keepdims=True)
        sel = col == first
        work = jnp.where(sel, -jnp.inf, work)
        vals.append(mx)
        idxs.append(first)
        sels.append(sel)
    es = [jnp.exp(val - vals[0]) for val in vals]
    inv = 1.0 / (es[0] + es[1] + es[2] + es[3])

    onehot = jnp.zeros(logits.shape, F32)
    for sel in sels:
        onehot = onehot + jnp.where(sel, 1.0, 0.0)
    ri = lax.broadcasted_iota(jnp.int32, (tm, tm), 0)
    ci = lax.broadcasted_iota(jnp.int32, (tm, tm), 1)
    tri = jnp.where(ci < ri, 1.0, 0.0).astype(BF16)
    before = jnp.dot(tri, onehot.astype(BF16), preferred_element_type=F32)
    ranks = [jnp.sum(jnp.where(sel, before, 0.0), axis=1, keepdims=True) for sel in sels]

    idx_ref[...] = _pack_cols(idxs, jnp.int32)
    gate_ref[...] = _pack_cols([e * inv for e in es], F32)
    rank_ref[...] = _pack_cols(ranks, jnp.int32)
    cnt_ref[...] = jnp.sum(onehot, axis=0, keepdims=True).astype(jnp.int32)


def _proj_route(a, w, bias, x, mod, ln_g, ln_b, w_router, b_router, alpha, nl, nc):
    t, d = x.shape
    n_exp = w_router.shape[1]
    tm = ROW_BLOCK
    row_spec = pl.BlockSpec((tm, d), lambda r: (r, 0))
    vec = pl.BlockSpec((1, d), lambda r: (0, 0))
    k_spec = pl.BlockSpec((tm, TOP_K), lambda r: (r, 0))
    wr_hi = w_router.astype(BF16)
    return pl.pallas_call(
        functools.partial(_proj_route_kernel, alpha),
        grid=(t // tm,),
        in_specs=[
            row_spec,
            pl.BlockSpec((d, d), lambda r: (0, 0)),
            vec,
            row_spec,
            pl.BlockSpec((None, 6, d), lambda r, f=_mod_row_map(nl, nc): (f(r), 0, 0)),
            vec, vec,
            pl.BlockSpec((2, d, n_exp), lambda r: (0, 0, 0)),
            pl.BlockSpec((1, n_exp), lambda r: (0, 0)),
        ],
        out_specs=[row_spec, row_spec, k_spec, k_spec, k_spec,
                   pl.BlockSpec((None, 1, n_exp), lambda r: (r, 0, 0))],
        out_shape=[
            jax.ShapeDtypeStruct((t, d), F32),
            jax.ShapeDtypeStruct((t, d), F32),
            jax.ShapeDtypeStruct((t, TOP_K), jnp.int32),
            jax.ShapeDtypeStruct((t, TOP_K), F32),
            jax.ShapeDtypeStruct((t, TOP_K), jnp.int32),
            jax.ShapeDtypeStruct((t // tm, 1, n_exp), jnp.int32),
        ],
        compiler_params=_cparams("parallel"),
        name="proj_ln_route",
    )(a, w.astype(BF16), bias.reshape(1, d), x, mod, ln_g.reshape(1, d), ln_b.reshape(1, d),
      jnp.stack([wr_hi, (w_router - wr_hi.astype(F32)).astype(BF16)]), b_router.reshape(1, n_exp))


def _local_rows(n_exp):
    return TOP_K * ROW_BLOCK + n_exp * SUBLANES


def _local_pos(idx_ref, rank_ref, off_ref):
    tm = idx_ref.shape[0]
    col = lax.broadcasted_iota(jnp.int32, (tm, off_ref.shape[1]), 1)
    idx = idx_ref[...]
    rank = rank_ref[...]
    pos = []
    for k in range(TOP_K):
        start = jnp.sum(jnp.where(col == idx[:, k:k + 1], off_ref[...], 0.0), axis=1, keepdims=True)
        pos.append(start + rank[:, k:k + 1].astype(F32))
    return pos


def _dispatch_kernel(tail_ref, pad_ref, nu_ref, tab_ref, prev_tab_ref, v_ref, idx_ref, rank_ref, off_ref,
                     xb_ref, srt_ref, zero_ref, sems, fill_sem):
    r = pl.program_id(0)
    slot = r % 2
    tm = v_ref.shape[0]
    rows = srt_ref.shape[1]
    bm = zero_ref.shape[0]
    n_blocks = xb_ref.shape[0] // bm

    @pl.when(pl.program_id(0) == 0)
    def _():
        zero_ref[...] = jnp.zeros_like(zero_ref)

        def padding(do):
            for e in range(tail_ref.shape[0]):
                pad = pad_ref[e]
                for bit in reversed(range(SUBLANES.bit_length() - 1, bm.bit_length() - 1)):
                    size = 1 << bit
                    off = pl.multiple_of(tail_ref[e] + ((pad >> (bit + 1)) << (bit + 1)), SUBLANES)
                    copy = pltpu.make_async_copy(zero_ref.at[pl.ds(0, size)], xb_ref.at[pl.ds(off, size)], fill_sem)
                    pl.when((pad & size) != 0)(functools.partial(do, copy))

        def blocks(do):
            def body(j, carry):
                do(pltpu.make_async_copy(zero_ref, xb_ref.at[pl.ds(pl.multiple_of(j * bm, bm), bm)], fill_sem))
                return carry
            lax.fori_loop(nu_ref[0], n_blocks, body, 0)

        padding(lambda copy: copy.start())
        blocks(lambda copy: copy.start())
        padding(lambda copy: copy.wait())
        blocks(lambda copy: copy.wait())

    pos = _local_pos(idx_ref, rank_ref, off_ref)
    lane = lax.broadcasted_iota(jnp.int32, (tm, LANES), 1)
    pos_cols = jnp.zeros((tm, LANES), F32)
    for k in range(TOP_K):
        pos_cols = jnp.where(lane == k, pos[k], pos_cols)
    pos_rows = pos_cols.T
    row = lax.broadcasted_iota(jnp.int32, (rows, tm), 0).astype(F32)
    hit = row == pos_rows[0:1, :]
    for k in range(1, TOP_K):
        hit = jnp.logical_or(hit, row == pos_rows[k:k + 1, :])
    perm = jnp.where(hit, 1.0, 0.0).astype(BF16)
    srt_ref[slot] = jnp.dot(perm, v_ref[...].astype(BF16), preferred_element_type=F32)

    def tile_copy(t_ref, buf, i):
        return pltpu.make_async_copy(
            srt_ref.at[buf, pl.ds(pl.multiple_of(i * SUBLANES, SUBLANES), SUBLANES)],
            xb_ref.at[pl.ds(pl.multiple_of(t_ref[i] * SUBLANES, SUBLANES), SUBLANES)], sems.at[buf])

    def issue(i, carry):
        tile_copy(tab_ref, slot, i).start()
        return carry

    def drain(t_ref, buf):
        def body(i, carry):
            tile_copy(t_ref, buf, i).wait()
            return carry
        lax.fori_loop(0, t_ref[TILE_TABLE - 1], body, 0)

    lax.fori_loop(0, tab_ref[TILE_TABLE - 1], issue, 0)

    @pl.when(r > 0)
    def _():
        drain(prev_tab_ref, 1 - slot)

    @pl.when(r == pl.num_programs(0) - 1)
    def _():
        drain(tab_ref, slot)


def _dispatch(v, idx, rank, plan, p_rows):
    t, d = v.shape
    tm = ROW_BLOCK
    n_exp = plan["loc_off"].shape[2]
    k_spec = pl.BlockSpec((tm, TOP_K), lambda r, *_: (r, 0))
    grid_spec = pltpu.PrefetchScalarGridSpec(
        num_scalar_prefetch=3,
        grid=(t // tm,),
        in_specs=[
            pl.BlockSpec((TILE_TABLE,), lambda r, *_: (r,), memory_space=pltpu.SMEM),
            pl.BlockSpec((TILE_TABLE,), lambda r, *_: (jnp.maximum(r - 1, 0),), memory_space=pltpu.SMEM),
            pl.BlockSpec((tm, d), lambda r, *_: (r, 0)),
            k_spec, k_spec,
            pl.BlockSpec((None, 1, n_exp), lambda r, *_: (r, 0, 0)),
        ],
        out_specs=pl.BlockSpec(memory_space=pl.ANY),
        scratch_shapes=[pltpu.VMEM((2, _local_rows(n_exp), d), F32), pltpu.VMEM((EXPERT_BLOCK, d), F32),
                        pltpu.SemaphoreType.DMA((2,)), pltpu.SemaphoreType.DMA],
    )
    return pl.pallas_call(
        _dispatch_kernel,
        grid_spec=grid_spec,
        out_shape=jax.ShapeDtypeStruct((p_rows, d), F32),
        compiler_params=_cparams("arbitrary"),
        name="moe_dispatch",
    )(plan["tail_start"], plan["pad_len"], plan["n_used"], plan["tile_table"], plan["tile_table"],
      v, idx, rank, plan["loc_off"])


def _expert_kernel(be_ref, nu_ref, x_ref, w1_ref, b1_ref, w2_ref, b2_ref, o_ref, w1b_ref, w2b_ref):
    j = pl.program_id(0)
    f = w2_ref.shape[0]
    prev = be_ref[jnp.maximum(j - 1, 0)]

    @pl.when(jnp.logical_and(j < nu_ref[0], jnp.logical_or(j == 0, be_ref[j] != prev)))
    def _():
        w1b_ref[...] = w1_ref[...].astype(BF16)
        w2b_ref[...] = w2_ref[...].astype(BF16)

    @pl.when(j < nu_ref[0])
    def _():
        h = jnp.dot(x_ref[...].astype(BF16), w1b_ref[...], preferred_element_type=F32) + b1_ref[...]
        glu = jnp.minimum(h[:, :f], SWIGLU_LIMIT)
        lin = jnp.clip(h[:, f:], -SWIGLU_LIMIT, SWIGLU_LIMIT)
        act = glu * _sigmoid(SWIGLU_ALPHA * glu) * (lin + 1.0)
        o_ref[...] = jnp.dot(act.astype(BF16), w2b_ref[...], preferred_element_type=F32) + b2_ref[...]

    @pl.when(j >= nu_ref[0])
    def _():
        o_ref[...] = jnp.zeros_like(o_ref)


def _experts(xb, block_e, n_used, layer, w1, b1, w2, b2):
    d = xb.shape[1]
    depth, n_exp, _, f2 = w1.shape
    f = f2 // 2
    bm = EXPERT_BLOCK
    n_blocks = block_e.shape[0]
    last = lambda j, nu: jnp.minimum(j, nu[0] - 1)
    grid_spec = pltpu.PrefetchScalarGridSpec(
        num_scalar_prefetch=2,
        grid=(n_blocks,),
        in_specs=[
            pl.BlockSpec((bm, d), lambda j, be, nu: (last(j, nu), 0)),
            pl.BlockSpec((None, None, d, f2), lambda j, be, nu: (layer, be[last(j, nu)], 0, 0)),
            pl.BlockSpec((None, None, 1, f2), lambda j, be, nu: (layer, be[last(j, nu)], 0, 0)),
            pl.BlockSpec((None, None, f, d), lambda j, be, nu: (layer, be[last(j, nu)], 0, 0)),
            pl.BlockSpec((None, None, 1, d), lambda j, be, nu: (layer, be[last(j, nu)], 0, 0)),
        ],
        out_specs=pl.BlockSpec((bm, d), lambda j, be, nu: (j, 0)),
        scratch_shapes=[pltpu.VMEM((d, f2), BF16), pltpu.VMEM((f, d), BF16)],
    )
    return pl.pallas_call(
        _expert_kernel,
        grid_spec=grid_spec,
        out_shape=jax.ShapeDtypeStruct((n_blocks * bm, d), F32),
        compiler_params=_cparams("arbitrary"),
        name="moe_experts",
    )(block_e, n_used, xb, w1, b1.reshape(depth, n_exp, 1, f2), w2, b2.reshape(depth, n_exp, 1, d))


def _combine_kernel(alpha, tab_ref, next_tab_ref, idx_ref, rank_ref, off_ref, gate_ref, x1_ref, mod_ref,
                    g_ref, beta_ref, yb_ref, o_ref, rows_ref, sems):
    r = pl.program_id(0)
    tm = x1_ref.shape[0]
    rows = rows_ref.shape[1]
    slot = r % 2

    def tile_copy(t_ref, into, i):
        return pltpu.make_async_copy(
            yb_ref.at[pl.ds(pl.multiple_of(t_ref[i] * SUBLANES, SUBLANES), SUBLANES)],
            rows_ref.at[into, pl.ds(pl.multiple_of(i * SUBLANES, SUBLANES), SUBLANES)], sems.at[into])

    def gather(t_ref, into):
        def issue(i, carry):
            tile_copy(t_ref, into, i).start()
            return carry
        lax.fori_loop(0, t_ref[TILE_TABLE - 1], issue, 0)

    @pl.when(r == 0)
    def _():
        gather(tab_ref, 0)

    @pl.when(r + 1 < pl.num_programs(0))
    def _():
        gather(next_tab_ref, 1 - slot)

    n_tiles = tab_ref[TILE_TABLE - 1]

    def drain(i, carry):
        tile_copy(tab_ref, slot, i).wait()
        return carry

    lax.fori_loop(0, n_tiles, drain, 0)

    pos = _local_pos(idx_ref, rank_ref, off_ref)
    lane = lax.broadcasted_iota(jnp.int32, (tm, rows), 1).astype(F32)
    gates = gate_ref[...]
    unperm = jnp.where(lane == pos[0], gates[:, 0:1], 0.0)
    for k in range(1, TOP_K):
        unperm = unperm + jnp.where(lane == pos[k], gates[:, k:k + 1], 0.0)
    row = lax.broadcasted_iota(jnp.int32, (rows, 1), 0)
    y = jnp.where(row < n_tiles * SUBLANES, rows_ref[slot], 0.0).astype(BF16)
    f = jnp.dot(unperm.astype(BF16), y, preferred_element_type=F32)
    o_ref[...] = _layer_norm(alpha * x1_ref[...] + mod_ref[5:6, :] * f, g_ref[...], beta_ref[...])


def _combine(idx, rank, gates, x1, mod, ln_g, ln_b, yb, plan, alpha, nl, nc):
    t, d = x1.shape
    tm = ROW_BLOCK
    n_exp = plan["loc_off"].shape[2]
    row_spec = pl.BlockSpec((tm, d), lambda r: (r, 0))
    vec = pl.BlockSpec((1, d), lambda r: (0, 0))
    k_spec = pl.BlockSpec((tm, TOP_K), lambda r: (r, 0))
    return pl.pallas_call(
        functools.partial(_combine_kernel, alpha),
        grid=(t // tm,),
        in_specs=[
            pl.BlockSpec((TILE_TABLE,), lambda r: (r,), memory_space=pltpu.SMEM),
            pl.BlockSpec((TILE_TABLE,), lambda r: (jnp.minimum(r + 1, t // tm - 1),), memory_space=pltpu.SMEM),
            k_spec, k_spec,
            pl.BlockSpec((None, 1, n_exp), lambda r: (r, 0, 0)),
            k_spec,
            row_spec,
            pl.BlockSpec((None, 6, d), lambda r, f=_mod_row_map(nl, nc): (f(r), 0, 0)),
            vec, vec,
            pl.BlockSpec(memory_space=pl.ANY),
        ],
        out_specs=row_spec,
        out_shape=jax.ShapeDtypeStruct((t, d), F32),
        scratch_shapes=[pltpu.VMEM((2, _local_rows(n_exp), d), F32), pltpu.SemaphoreType.DMA((2,))],
        compiler_params=_cparams("arbitrary"),
        name="moe_combine_ln",
    )(plan["tile_table"], plan["tile_table"], idx, rank, plan["loc_off"], gates, x1, mod,
      ln_g.reshape(1, d), ln_b.reshape(1, d), yb)


def _moe_plan(cnt, n_blocks):
    nblk, n_exp = cnt.shape
    bm = EXPERT_BLOCK
    n_tiles_max = _local_rows(n_exp) // SUBLANES
    experts = jnp.arange(n_exp)
    blocks = jnp.arange(nblk)
    seg = (cnt + SUBLANES - 1) // SUBLANES * SUBLANES
    loc_off = jnp.sum(jnp.where(experts[None, None, :] < experts[None, :, None], seg[:, None, :], 0), axis=2)
    before = jnp.sum(jnp.where(blocks[None, :, None] < blocks[:, None, None], seg[None, :, :], 0), axis=1)
    rows_e = jnp.sum(seg, axis=0)
    padded = (rows_e + bm - 1) // bm * bm
    pend = jnp.sum(jnp.where(experts[None, :] <= experts[:, None], padded[None, :], 0), axis=1)
    pstart = pend - padded
    base = pstart[None, :] + before
    first_row = (jnp.arange(n_tiles_max) * SUBLANES)[None, :, None]
    e_of = jnp.sum(((loc_off + seg)[:, None, :] <= first_row).astype(jnp.int32), axis=2)
    pick = jnp.minimum(e_of, n_exp - 1)[:, :, None] == experts
    delta = jnp.sum(jnp.where(pick, (base - loc_off)[:, None, :], 0), axis=2)
    tile_of = (delta + first_row[:, :, 0]) // SUBLANES
    table = jnp.concatenate([tile_of.astype(jnp.int32),
                             jnp.zeros((nblk, TILE_TABLE - 1 - n_tiles_max), jnp.int32),
                             (jnp.sum(seg, axis=1, keepdims=True) // SUBLANES).astype(jnp.int32)], axis=1)
    block_e = jnp.sum((pend[None, :] <= (jnp.arange(n_blocks) * bm)[:, None]).astype(jnp.int32), axis=1)
    return dict(
        loc_off=loc_off.astype(F32).reshape(nblk, 1, n_exp),
        tile_table=table.reshape(-1),
        tail_start=(pstart + rows_e).astype(jnp.int32),
        pad_len=(padded - rows_e).astype(jnp.int32),
        block_e=jnp.minimum(block_e, n_exp - 1).astype(jnp.int32),
        n_used=(pend[-1:] // bm).astype(jnp.int32),
    )


def kernel(x, c, ctx, c_ctx, w_mod, b_mod, ln1_g, ln1_b, ln2_g, ln2_b, attn_w_qkv, attn_w_o, attn_lam_q1, attn_lam_k1, attn_lam_q2, attn_lam_k2, attn_subln_g, fnet_w, fnet_b, conv_w_pw1, conv_b_pw1, conv_w_dw, conv_b_dw, conv_ln_g, conv_ln_b, conv_w_pw2, conv_b_pw2, moe_w_router, moe_b_router, moe_w1, moe_b1, moe_w2, moe_b2):
    b, n, d = x.shape
    cl = ctx.shape[1]
    s = n + cl
    t = b * s
    depth = w_mod.shape[0]
    n_exp = moe_w_router.shape[2]
    nl, nc = n // ROW_BLOCK, cl // ROW_BLOCK
    alpha = (2 * depth) ** 0.25
    worst_rows = t * TOP_K + (t // ROW_BLOCK) * n_exp * (SUBLANES - 1)
    n_blocks = -(-worst_rows // EXPERT_BLOCK) + n_exp
    p_rows = n_blocks * EXPERT_BLOCK

    xs = jnp.concatenate([x, ctx], axis=1).reshape(t, d)
    cond = jnp.stack([c, jnp.broadcast_to(c_ctx, c.shape)], axis=1).reshape(2 * b, d)
    mod = _modulation(cond, w_mod, b_mod)
    cos, sin = _rope_tables(n, cl)

    for i in range(depth):
        kind, j = i % 3, i // 3
        if kind == 0:
            lam_init = 0.8 - 0.6 * float(np.exp(-0.3 * i))
            q, k, v = _qkv(xs, mod[i], attn_w_qkv[j], cos, sin, nl, nc)
            q3, k3 = q.reshape(b, s, d), k.reshape(b, s, d)
            vt = v.reshape(b, s, d // HEAD_W, HEAD_W).transpose(0, 2, 3, 1)
            vt = jnp.concatenate([vt, jnp.ones((b, d // HEAD_W, SUM_ROWS, s), BF16)], axis=2)
            lam_params = (attn_lam_q1[j], attn_lam_k1[j], attn_lam_q2[j], attn_lam_k2[j])
            o_lat = _attention(q3, k3, vt, lam_params, attn_subln_g[j], lam_init, n, cl, context=False)
            o_ctx = _attention(q3, k3, vt, lam_params, attn_subln_g[j], lam_init, n, cl, context=True)
            a = jnp.concatenate([o_lat, o_ctx], axis=1).reshape(t, d)
            w_out, b_out = attn_w_o[j], jnp.zeros((d,), F32)
        elif kind == 1:
            fa, fb = _chan_dft(xs, mod[i], nl, nc)
            fa3, fb3 = fa.reshape(b, s, d), fb.reshape(b, s, d)
            a = jnp.concatenate([_seq_dft(fa3, fb3, n, 0), _seq_dft(fa3, fb3, cl, n)], axis=1).reshape(t, d)
            w_out, b_out = fnet_w[j], fnet_b[j]
        else:
            hid = _glu(xs, mod[i], conv_w_pw1[j], conv_b_pw1[j], nl, nc)
            a = _dwconv_ln_silu(hid, conv_w_dw[j], conv_b_dw[j], conv_ln_g[j], conv_ln_b[j], nl, nc)
            w_out, b_out = conv_w_pw2[j], conv_b_pw2[j]

        x1, v, idx, gates, rank, cnt = _proj_route(
            a, w_out, b_out, xs, mod[i], ln1_g[i], ln1_b[i], moe_w_router[i], moe_b_router[i], alpha, nl, nc)
        plan = _moe_plan(cnt[:, 0, :], n_blocks)
        xb = _dispatch(v, idx, rank, plan, p_rows)
        yb = _experts(xb, plan["block_e"], plan["n_used"], i, moe_w1, moe_b1, moe_w2, moe_b2)
        xs = _combine(idx, rank, gates, x1, mod[i], ln2_g[i], ln2_b[i], yb, plan, alpha, nl, nc)

    return xs.reshape(b, s, d)[:, :n]
```

```python
import functools

import jax
import jax.numpy as jnp
import numpy as np
from jax import lax
from jax.experimental import pallas as pl
from jax.experimental.pallas import tpu as pltpu

F32 = jnp.float32
BF16 = jnp.bfloat16
HIGHEST = lax.Precision.HIGHEST

GRID_W = 64
HEAD_W = 128
MAP_W = HEAD_W // 2
ROPE_FREQS = MAP_W // 4
ROPE_BASE = 10000.0
CONV_WIDTH = 31
CONV_PAD = CONV_WIDTH // 2
TOP_K = 4
SWIGLU_ALPHA = 1.702
SWIGLU_LIMIT = 7.0
LN_EPS = 1e-5

LANES = 128
SUBLANES = 8
ROW_BLOCK = 256
SUM_ROWS = 8
HALO = 16
ATTN_TQ = 512
ATTN_TK = 512
ATTN_QC = 256
DFT_TILE = 1024
EXPERT_BLOCK = 512
TILE_TABLE = 1024
VMEM_LIMIT = 56 * 1024 * 1024


def _cparams(*sem):
    return pltpu.CompilerParams(dimension_semantics=sem, vmem_limit_bytes=VMEM_LIMIT)


def _mod_row_map(nl, nc):
    rb = nl + nc

    def f(r):
        return 2 * (r // rb) + jnp.where((r % rb) >= nl, 1, 0)

    return f


def _layer_norm(z, g, b):
    mu = jnp.mean(z, axis=-1, keepdims=True)
    zc = z - mu
    var = jnp.mean(zc * zc, axis=-1, keepdims=True)
    return zc * lax.rsqrt(var + LN_EPS) * g + b


def _sigmoid(x):
    return 1.0 / (1.0 + jnp.exp(-x))


def _modulation_kernel(cond_ref, w_ref, b_ref, o_ref):
    cnd = cond_ref[...]
    s = cnd * _sigmoid(cnd)
    o_ref[...] = jnp.dot(s, w_ref[...], precision=HIGHEST, preferred_element_type=F32) + b_ref[...]


def _modulation(cond, w_mod, b_mod):
    depth, d, d6 = w_mod.shape
    rows = cond.shape[0]
    out = pl.pallas_call(
        _modulation_kernel,
        grid=(depth, d6 // d),
        in_specs=[
            pl.BlockSpec((rows, d), lambda i, j: (0, 0)),
            pl.BlockSpec((None, d, d), lambda i, j: (i, 0, j)),
            pl.BlockSpec((None, 1, d), lambda i, j: (i, 0, j)),
        ],
        out_specs=pl.BlockSpec((None, rows, d), lambda i, j: (i, 0, j)),
        out_shape=jax.ShapeDtypeStruct((depth, rows, d6), F32),
        compiler_params=_cparams("parallel", "parallel"),
        name="modulation",
    )(cond, w_mod, b_mod.reshape(depth, 1, d6))
    return out.reshape(depth, rows, d6 // d, d)


def _rope_tables(n, c):
    pos = jnp.arange(n)
    rows = (pos // GRID_W).astype(F32)
    cols = (pos % GRID_W).astype(F32)
    inv_freq = ROPE_BASE ** (-jnp.arange(ROPE_FREQS, dtype=F32) / ROPE_FREQS)
    lane = np.arange(HEAD_W)
    dim = lane % MAP_W
    freq = dim % ROPE_FREQS
    use_row = (dim // (MAP_W // 2)) == 0
    first_half = (dim % (MAP_W // 2)) < ROPE_FREQS
    ang = jnp.where(use_row[None, :], rows[:, None] * inv_freq[freq][None, :],
                    cols[:, None] * inv_freq[freq][None, :])
    cos = jnp.cos(ang)
    sin = jnp.sin(ang)
    sin = jnp.where(first_half[None, :], -sin, sin)
    cos = jnp.concatenate([cos, jnp.ones((c, HEAD_W), F32)], axis=0)
    sin = jnp.concatenate([sin, jnp.zeros((c, HEAD_W), F32)], axis=0)
    return cos, sin


def _qkv_kernel(x_ref, mod_ref, w_ref, cos_ref, sin_ref, q_ref, k_ref, v_ref):
    d = x_ref.shape[1]
    u = (x_ref[...] * (1.0 + mod_ref[1:2, :]) + mod_ref[0:1, :]).astype(BF16)
    cos = cos_ref[...]
    sin = sin_ref[...]
    lane = lax.broadcasted_iota(jnp.int32, cos.shape, 1)
    first_half = (lane % (MAP_W // 2)) < ROPE_FREQS

    def rope(t):
        partner = jnp.where(first_half, pltpu.roll(t, HEAD_W - ROPE_FREQS, 1), pltpu.roll(t, ROPE_FREQS, 1))
        return t * cos + partner * sin

    q = jnp.dot(u, w_ref[:, :d], preferred_element_type=F32)
    k = jnp.dot(u, w_ref[:, d:2 * d], preferred_element_type=F32)
    for h in range(d // HEAD_W):
        cs = slice(h * HEAD_W, (h + 1) * HEAD_W)
        q_ref[:, cs] = rope(q[:, cs]).astype(BF16)
        k_ref[:, cs] = rope(k[:, cs]).astype(BF16)
    v_ref[...] = jnp.dot(u, w_ref[:, 2 * d:], preferred_element_type=F32).astype(BF16)


def _qkv(x, mod, w_qkv, cos, sin, nl, nc):
    t, d = x.shape
    rb = nl + nc
    tm = ROW_BLOCK
    scale = jnp.concatenate([jnp.full((d,), np.log2(np.e) * MAP_W ** -0.5, F32), jnp.ones((2 * d,), F32)])
    w = (w_qkv * scale[None, :]).astype(BF16)
    row_spec = pl.BlockSpec((tm, d), lambda r: (r, 0))
    tab_spec = pl.BlockSpec((tm, HEAD_W), lambda r: (r % rb, 0))
    return pl.pallas_call(
        _qkv_kernel,
        grid=(t // tm,),
        in_specs=[
            row_spec,
            pl.BlockSpec((None, 6, d), lambda r, f=_mod_row_map(nl, nc): (f(r), 0, 0)),
            pl.BlockSpec((d, 3 * d), lambda r: (0, 0)),
            tab_spec, tab_spec,
        ],
        out_specs=[row_spec, row_spec, row_spec],
        out_shape=[jax.ShapeDtypeStruct((t, d), BF16)] * 3,
        compiler_params=_cparams("parallel"),
        name="qkv_rope",
    )(x, mod, w, cos, sin)


def _attn_kernel(lam_init, tiles, q_ref, k_ref, vt_ref, lq1_ref, lk1_ref, lq2_ref, lk2_ref, g_ref,
                 o_ref, acc1_ref, acc2_ref, sa1_ref, sa2_ref, sb1_ref, sb2_ref):
    tq = q_ref.shape[0]
    q = q_ref[...].astype(F32)
    lane = lax.broadcasted_iota(jnp.int32, q.shape, 1)
    q1 = jnp.where(lane < MAP_W, q, 0.0).astype(BF16)
    q2 = jnp.where(lane >= MAP_W, q, 0.0).astype(BF16)
    acc1_ref[...] = jnp.zeros_like(acc1_ref)
    acc2_ref[...] = jnp.zeros_like(acc2_ref)
    acc_refs = (acc1_ref, acc2_ref)
    s_refs = ((sa1_ref, sa2_ref), (sb1_ref, sb2_ref))

    qc = min(tq, ATTN_QC)
    pieces = [(mp, c) for c in range(tq // qc) for mp in range(2)]
    qms = (q1, q2)

    def score_piece(i, mp, c):
        off, size = tiles[i]
        s = lax.dot_general(k_ref[off:off + size, :], qms[mp][c * qc:(c + 1) * qc, :],
                            (((1,), (1,)), ((), ())), preferred_element_type=F32)
        s_refs[i % 2][mp][0:size, c * qc:(c + 1) * qc] = s
        return jnp.max(s, axis=0, keepdims=True)

    def acc_piece(i, mp, c, cm, m):
        off, size = tiles[i]
        cs = slice(c * qc, (c + 1) * qc)
        m_new = jnp.maximum(m, cm)
        p = jnp.exp2(s_refs[i % 2][mp][0:size, cs] - m_new).astype(BF16)
        acc_refs[mp][:, cs] = (jnp.exp2(m - m_new) * acc_refs[mp][:, cs]
                               + jnp.dot(vt_ref[:, off:off + size], p, preferred_element_type=F32))
        return m_new

    def stage(i_score, i_acc, cm_acc, ms):
        cm_new, ms_new = [], []
        for n_piece, (mp, c) in enumerate(pieces):
            if i_score is not None:
                cm_new.append(score_piece(i_score, mp, c))
            if i_acc is not None:
                ms_new.append(acc_piece(i_acc, mp, c, cm_acc[n_piece], ms[n_piece]))
        return tuple(cm_new), tuple(ms_new) if i_acc is not None else ms

    ms = tuple(jnp.full((1, qc), -jnp.inf, F32) for _ in pieces)
    cm, _ = stage(0, None, None, ms)
    for i in range(len(tiles) - 1):
        cm, ms = stage(i + 1, i, cm, ms)
    stage(None, len(tiles) - 1, cm, ms)

    lam = (jnp.exp(jnp.sum(lq1_ref[...] * lk1_ref[...], axis=1, keepdims=True))
           - jnp.exp(jnp.sum(lq2_ref[...] * lk2_ref[...], axis=1, keepdims=True)) + lam_init)
    inv1 = 1.0 / acc1_ref[HEAD_W:HEAD_W + 1, :]
    inv2 = 1.0 / acc2_ref[HEAD_W:HEAD_W + 1, :]
    o = acc1_ref[0:HEAD_W, :] * inv1 - lam * (acc2_ref[0:HEAD_W, :] * inv2)
    ms = jnp.mean(o * o, axis=0, keepdims=True)
    o = o * lax.rsqrt(ms + LN_EPS) * (g_ref[...] * (1.0 - lam_init))
    o_ref[...] = o.T.astype(BF16)


def _attention(q3, k3, vt, lam_params, subln_g, lam_init, n, c, context):
    b, s, d = q3.shape
    h = d // HEAD_W
    if context:
        tq, kv_len, kv_blk, q_off, nq = c, c, n // c, n // c, 1
    else:
        tq, kv_len, kv_blk, q_off, nq = ATTN_TQ, s, 0, 0, n // ATTN_TQ
    tk = min(ATTN_TK, kv_len)
    tiles = tuple((off, min(tk, kv_len - off)) for off in range(0, kv_len, tk))
    lam_spec = pl.BlockSpec((1, MAP_W), lambda bi, hi, qi: (0, 0))
    return pl.pallas_call(
        functools.partial(_attn_kernel, lam_init, tiles),
        grid=(b, h, nq),
        in_specs=[
            pl.BlockSpec((None, tq, HEAD_W), lambda bi, hi, qi: (bi, qi + q_off, hi)),
            pl.BlockSpec((None, kv_len, HEAD_W), lambda bi, hi, qi: (bi, kv_blk, hi)),
            pl.BlockSpec((None, None, HEAD_W + SUM_ROWS, kv_len), lambda bi, hi, qi: (bi, hi, 0, kv_blk)),
            lam_spec, lam_spec, lam_spec, lam_spec,
            pl.BlockSpec((HEAD_W, 1), lambda bi, hi, qi: (0, 0)),
        ],
        out_specs=pl.BlockSpec((None, tq, HEAD_W), lambda bi, hi, qi: (bi, qi, hi)),
        out_shape=jax.ShapeDtypeStruct((b, tq * nq, d), BF16),
        scratch_shapes=[pltpu.VMEM((HEAD_W + SUM_ROWS, tq), F32)] * 2 + [pltpu.VMEM((tk, tq), F32)] * 4,
        compiler_params=_cparams("parallel", "parallel", "parallel"),
        name="attn_ctx" if context else "attn_lat",
    )(q3, k3, vt, *[p.reshape(1, MAP_W) for p in lam_params], subln_g.reshape(HEAD_W, 1))


def _dft_matrices(n, scale):
    blk = min(ROW_BLOCK, n)
    k = jnp.arange(n, dtype=jnp.int32)

    def cs(j):
        r = (j[:, None] * k[None, :]) % n
        ang = r.astype(F32) * (2.0 * np.pi / n)
        return jnp.cos(ang), jnp.sin(ang)

    c0, s0 = cs(jnp.arange(blk, dtype=jnp.int32))
    cj, sj = cs(jnp.arange(0, n, blk, dtype=jnp.int32))
    cm = cj[:, None, :] * c0[None] - sj[:, None, :] * s0[None]
    sm = sj[:, None, :] * c0[None] + cj[:, None, :] * s0[None]
    return (cm * scale).reshape(n, n).astype(BF16), (sm * scale).reshape(n, n).astype(BF16)


def _chan_dft_kernel(x_ref, mod_ref, cs_ref, a_ref, b_ref):
    d = x_ref.shape[1]
    u = (x_ref[...] * (1.0 + mod_ref[1:2, :]) + mod_ref[0:1, :]).astype(BF16)
    for g in range(d // LANES):
        sl = slice(g * LANES, (g + 1) * LANES)
        ab = jnp.dot(u[:, sl], cs_ref[...], preferred_element_type=F32)
        a_ref[:, sl] = ab[:, :LANES].astype(BF16)
        b_ref[:, sl] = ab[:, LANES:].astype(BF16)


def _chan_dft(x, mod, nl, nc):
    t, d = x.shape
    tm = ROW_BLOCK
    j = np.arange(LANES)
    ang = 2.0 * np.pi * ((j[:, None] * j[None, :]) % LANES) / LANES
    cs = jnp.asarray(np.concatenate([np.cos(ang), np.sin(ang)], axis=1) / np.sqrt(LANES), BF16)
    row_spec = pl.BlockSpec((tm, d), lambda r: (r, 0))
    return pl.pallas_call(
        _chan_dft_kernel,
        grid=(t // tm,),
        in_specs=[
            row_spec,
            pl.BlockSpec((None, 6, d), lambda r, f=_mod_row_map(nl, nc): (f(r), 0, 0)),
            pl.BlockSpec((LANES, 2 * LANES), lambda r: (0, 0)),
        ],
        out_specs=[row_spec, row_spec],
        out_shape=[jax.ShapeDtypeStruct((t, d), BF16)] * 2,
        compiler_params=_cparams("parallel"),
        name="chan_dft",
    )(x, mod, cs)


def _seq_dft_kernel(c_ref, s_ref, a_ref, b_ref, o_ref, acc_ref):
    kk = pl.program_id(2)

    @pl.when(kk == 0)
    def _():
        acc_ref[...] = jnp.zeros_like(acc_ref)

    acc_ref[...] += (jnp.dot(c_ref[...], a_ref[...], preferred_element_type=F32)
                     - jnp.dot(s_ref[...], b_ref[...], preferred_element_type=F32))

    @pl.when(kk == pl.num_programs(2) - 1)
    def _():
        o_ref[...] = acc_ref[...].astype(o_ref.dtype)


def _seq_dft(a3, b3, length, row_off):
    b, s, d = a3.shape
    tile = min(DFT_TILE, length)
    cm, sm = _dft_matrices(length, length ** -0.5)
    nt = length // tile
    off = row_off // tile
    mat_spec = pl.BlockSpec((tile, tile), lambda bi, m, k: (m, k))
    in_spec = pl.BlockSpec((None, tile, d), lambda bi, m, k: (bi, k + off, 0))
    return pl.pallas_call(
        _seq_dft_kernel,
        grid=(b, nt, nt),
        in_specs=[mat_spec, mat_spec, in_spec, in_spec],
        out_specs=pl.BlockSpec((None, tile, d), lambda bi, m, k: (bi, m, 0)),
        out_shape=jax.ShapeDtypeStruct((b, length, d), BF16),
        scratch_shapes=[pltpu.VMEM((tile, d), F32)],
        compiler_params=_cparams("parallel", "parallel", "arbitrary"),
        name="seq_dft",
    )(cm, sm, a3, b3)


def _glu_kernel(x_ref, mod_ref, w_ref, b_ref, o_ref):
    d = x_ref.shape[1]
    u = (x_ref[...] * (1.0 + mod_ref[1:2, :]) + mod_ref[0:1, :]).astype(BF16)
    a = jnp.dot(u, w_ref[:, :d], preferred_element_type=F32) + b_ref[:, :d]
    g = jnp.dot(u, w_ref[:, d:], preferred_element_type=F32) + b_ref[:, d:]
    o_ref[...] = a * _sigmoid(g)


def _glu(x, mod, w, bias, nl, nc):
    t, d = x.shape
    tm = ROW_BLOCK
    row_spec = pl.BlockSpec((tm, d), lambda r: (r, 0))
    return pl.pallas_call(
        _glu_kernel,
        grid=(t // tm,),
        in_specs=[
            row_spec,
            pl.BlockSpec((None, 6, d), lambda r, f=_mod_row_map(nl, nc): (f(r), 0, 0)),
            pl.BlockSpec((d, 2 * d), lambda r: (0, 0)),
            pl.BlockSpec((1, 2 * d), lambda r: (0, 0)),
        ],
        out_specs=row_spec,
        out_shape=jax.ShapeDtypeStruct((t, d), F32),
        compiler_params=_cparams("parallel"),
        name="pw1_glu",
    )(x, mod, w.astype(BF16), bias.reshape(1, 2 * d))


def _dwconv_kernel(nl, nc, prev_ref, cur_ref, next_ref, w_ref, b_ref, g_ref, beta_ref, o_ref, win_ref, h_ref):
    tm, d = cur_ref.shape
    rr = pl.program_id(0) % (nl + nc)
    has_prev = jnp.logical_and(rr != 0, rr != nl)
    has_next = jnp.logical_and(rr != nl - 1, rr != nl + nc - 1)
    win_ref[0, 0:HALO, :] = jnp.where(has_prev, prev_ref[...], 0.0)
    win_ref[0, HALO:HALO + tm, :] = cur_ref[...]
    win_ref[0, HALO + tm:, :] = jnp.where(has_next, next_ref[...], 0.0)
    span = tm + 2 * HALO - SUBLANES
    for s in range(1, SUBLANES):
        win_ref[s, 0:span, :] = win_ref[0, s:s + span, :]
    rows = 64
    for c in range(d // LANES):
        cs = slice(c * LANES, (c + 1) * LANES)
        wc = w_ref[:, cs]
        for r0 in range(0, tm, rows):
            acc = jnp.broadcast_to(b_ref[:, cs], (rows, LANES))
            for j in range(CONV_WIDTH):
                off = HALO - CONV_PAD + j
                start = r0 + off - off % SUBLANES
                acc = acc + wc[j:j + 1, :] * win_ref[off % SUBLANES, start:start + rows, cs]
            h_ref[r0:r0 + rows, cs] = acc
    y = _layer_norm(h_ref[...], g_ref[...], beta_ref[...])
    o_ref[...] = (y * _sigmoid(y)).astype(BF16)


def _dwconv_ln_silu(hid, w_dw, b_dw, ln_g, ln_b, nl, nc):
    t, d = hid.shape
    tm = ROW_BLOCK
    per = tm // HALO
    n_halo = t // HALO
    vec = pl.BlockSpec((1, d), lambda r: (0, 0))
    return pl.pallas_call(
        functools.partial(_dwconv_kernel, nl, nc),
        grid=(t // tm,),
        in_specs=[
            pl.BlockSpec((HALO, d), lambda r: (jnp.maximum(r * per - 1, 0), 0)),
            pl.BlockSpec((tm, d), lambda r: (r, 0)),
            pl.BlockSpec((HALO, d), lambda r: (jnp.minimum((r + 1) * per, n_halo - 1), 0)),
            pl.BlockSpec((CONV_WIDTH, d), lambda r: (0, 0)),
            vec, vec, vec,
        ],
        out_specs=pl.BlockSpec((tm, d), lambda r: (r, 0)),
        out_shape=jax.ShapeDtypeStruct((t, d), BF16),
        scratch_shapes=[pltpu.VMEM((SUBLANES, tm + 2 * HALO, d), F32), pltpu.VMEM((tm, d), F32)],
        compiler_params=_cparams("parallel"),
        name="dwconv_ln_silu",
    )(hid, hid, hid, w_dw, b_dw.reshape(1, d), ln_g.reshape(1, d), ln_b.reshape(1, d))


def _pack_cols(cols, dtype):
    rows = cols[0].shape[0]
    lane = lax.broadcasted_iota(jnp.int32, (rows, len(cols)), 1)
    out = jnp.zeros((rows, len(cols)), dtype)
    for i, col in enumerate(cols):
        out = jnp.where(lane == i, col.astype(dtype), out)
    return out


def _proj_route_kernel(alpha, a_ref, w_ref, bias_ref, x_ref, mod_ref, g_ref, beta_ref, wr_ref, br_ref,
                       x1_ref, v_ref, idx_ref, gate_ref, rank_ref, cnt_ref):
    tm = x_ref.shape[0]
    n_exp = wr_ref.shape[2]

    y = jnp.dot(a_ref[...], w_ref[...], preferred_element_type=F32) + bias_ref[...]
    x1 = _layer_norm(alpha * x_ref[...] + mod_ref[2:3, :] * y, g_ref[...], beta_ref[...])
    x1_ref[...] = x1
    v = x1 * (1.0 + mod_ref[4:5, :]) + mod_ref[3:4, :]
    v_ref[...] = v

    v_hi = v.astype(BF16)
    v_lo = (v - v_hi.astype(F32)).astype(BF16)
    logits = (jnp.dot(v_hi, wr_ref[0], preferred_element_type=F32)
              + jnp.dot(v_lo, wr_ref[0], preferred_element_type=F32)
              + jnp.dot(v_hi, wr_ref[1], preferred_element_type=F32)) + br_ref[...]
    col = lax.broadcasted_iota(jnp.int32, logits.shape, 1).astype(F32)
    work = logits
    vals, idxs, sels = [], [], []
    for _ in range(TOP_K):
        mx = jnp.max(work, axis=1, keepdims=True)
        first = jnp.min(jnp.where(work == mx, col, float(n_exp)), axis=1, keepdims=True)
        sel = col == first
        work = jnp.where(sel, -jnp.inf, work)
        vals.append(mx)
        idxs.append(first)
        sels.append(sel)
    es = [jnp.exp(val - vals[0]) for val in vals]
    inv = 1.0 / (es[0] + es[1] + es[2] + es[3])

    onehot = jnp.zeros(logits.shape, F32)
    for sel in sels:
        onehot = onehot + jnp.where(sel, 1.0, 0.0)
    ri = lax.broadcasted_iota(jnp.int32, (tm, tm), 0)
    ci = lax.broadcasted_iota(jnp.int32, (tm, tm), 1)
    tri = jnp.where(ci < ri, 1.0, 0.0).astype(BF16)
    before = jnp.dot(tri, onehot.astype(BF16), preferred_element_type=F32)
    ranks = [jnp.sum(jnp.where(sel, before, 0.0), axis=1, keepdims=True) for sel in sels]

    idx_ref[...] = _pack_cols(idxs, jnp.int32)
    gate_ref[...] = _pack_cols([e * inv for e in es], F32)
    rank_ref[...] = _pack_cols(ranks, jnp.int32)
    cnt_ref[...] = jnp.sum(onehot, axis=0, keepdims=True).astype(jnp.int32)


def _proj_route(a, w, bias, x, mod, ln_g, ln_b, w_router, b_router, alpha, nl, nc):
    t, d = x.shape
    n_exp = w_router.shape[1]
    tm = ROW_BLOCK
    row_spec = pl.BlockSpec((tm, d), lambda r: (r, 0))
    vec = pl.BlockSpec((1, d), lambda r: (0, 0))
    k_spec = pl.BlockSpec((tm, TOP_K), lambda r: (r, 0))
    wr_hi = w_router.astype(BF16)
    return pl.pallas_call(
        functools.partial(_proj_route_kernel, alpha),
        grid=(t // tm,),
        in_specs=[
            row_spec,
            pl.BlockSpec((d, d), lambda r: (0, 0)),
            vec,
            row_spec,
            pl.BlockSpec((None, 6, d), lambda r, f=_mod_row_map(nl, nc): (f(r), 0, 0)),
            vec, vec,
            pl.BlockSpec((2, d, n_exp), lambda r: (0, 0, 0)),
            pl.BlockSpec((1, n_exp), lambda r: (0, 0)),
        ],
        out_specs=[row_spec, row_spec, k_spec, k_spec, k_spec,
                   pl.BlockSpec((None, 1, n_exp), lambda r: (r, 0, 0))],
        out_shape=[
            jax.ShapeDtypeStruct((t, d), F32),
            jax.ShapeDtypeStruct((t, d), F32),
            jax.ShapeDtypeStruct((t, TOP_K), jnp.int32),
            jax.ShapeDtypeStruct((t, TOP_K), F32),
            jax.ShapeDtypeStruct((t, TOP_K), jnp.int32),
            jax.ShapeDtypeStruct((t // tm, 1, n_exp), jnp.int32),
        ],
        compiler_params=_cparams("parallel"),
        name="proj_ln_route",
    )(a, w.astype(BF16), bias.reshape(1, d), x, mod, ln_g.reshape(1, d), ln_b.reshape(1, d),
      jnp.stack([wr_hi, (w_router - wr_hi.astype(F32)).astype(BF16)]), b_router.reshape(1, n_exp))


def _local_rows(n_exp):
    return TOP_K * ROW_BLOCK + n_exp * SUBLANES


def _pack_halves(x):
    half = x.shape[1] // 2
    hi = lax.bitcast_convert_type(x[:, :half], jnp.uint32)
    lo = lax.bitcast_convert_type(x[:, half:], jnp.uint32)
    return (hi & jnp.uint32(0xFFFF0000)) | (lo >> 16)


def _unpack_halves(w):
    hi = lax.bitcast_convert_type(w & jnp.uint32(0xFFFF0000), F32)
    lo = lax.bitcast_convert_type(w << 16, F32)
    return hi.astype(BF16), lo.astype(BF16)


def _local_pos(idx_ref, rank_ref, off_ref):
    tm = idx_ref.shape[0]
    col = lax.broadcasted_iota(jnp.int32, (tm, off_ref.shape[1]), 1)
    idx = idx_ref[...]
    rank = rank_ref[...]
    pos = []
    for k in range(TOP_K):
        start = jnp.sum(jnp.where(col == idx[:, k:k + 1], off_ref[...], 0.0), axis=1, keepdims=True)
        pos.append(start + rank[:, k:k + 1].astype(F32))
    return pos


def _dispatch_kernel(tail_ref, pad_ref, nu_ref, tab_ref, prev_tab_ref, v_ref, idx_ref, rank_ref, off_ref,
                     xb_ref, srt_ref, zero_ref, sems, fill_sem):
    r = pl.program_id(0)
    slot = r % 2
    tm = v_ref.shape[0]
    rows = srt_ref.shape[1]
    bm = zero_ref.shape[0]
    n_blocks = xb_ref.shape[0] // bm

    @pl.when(pl.program_id(0) == 0)
    def _():
        zero_ref[...] = jnp.zeros_like(zero_ref)

        def padding(do):
            for e in range(tail_ref.shape[0]):
                pad = pad_ref[e]
                for bit in reversed(range(SUBLANES.bit_length() - 1, bm.bit_length() - 1)):
                    size = 1 << bit
                    off = pl.multiple_of(tail_ref[e] + ((pad >> (bit + 1)) << (bit + 1)), SUBLANES)
                    copy = pltpu.make_async_copy(zero_ref.at[pl.ds(0, size)], xb_ref.at[pl.ds(off, size)], fill_sem)
                    pl.when((pad & size) != 0)(functools.partial(do, copy))

        def blocks(do):
            def body(j, carry):
                do(pltpu.make_async_copy(zero_ref, xb_ref.at[pl.ds(pl.multiple_of(j * bm, bm), bm)], fill_sem))
                return carry
            lax.fori_loop(nu_ref[0], n_blocks, body, 0)

        padding(lambda copy: copy.start())
        blocks(lambda copy: copy.start())
        padding(lambda copy: copy.wait())
        blocks(lambda copy: copy.wait())

    pos = _local_pos(idx_ref, rank_ref, off_ref)
    lane = lax.broadcasted_iota(jnp.int32, (tm, LANES), 1)
    pos_cols = jnp.zeros((tm, LANES), F32)
    for k in range(TOP_K):
        pos_cols = jnp.where(lane == k, pos[k], pos_cols)
    pos_rows = pos_cols.T
    row = lax.broadcasted_iota(jnp.int32, (rows, tm), 0).astype(F32)
    hit = row == pos_rows[0:1, :]
    for k in range(1, TOP_K):
        hit = jnp.logical_or(hit, row == pos_rows[k:k + 1, :])
    perm = jnp.where(hit, 1.0, 0.0).astype(BF16)
    srt_ref[slot] = _pack_halves(jnp.dot(perm, v_ref[...].astype(BF16), preferred_element_type=F32))

    def tile_copy(t_ref, buf, i):
        return pltpu.make_async_copy(
            srt_ref.at[buf, pl.ds(pl.multiple_of(i * SUBLANES, SUBLANES), SUBLANES)],
            xb_ref.at[pl.ds(pl.multiple_of(t_ref[i] * SUBLANES, SUBLANES), SUBLANES)], sems.at[buf])

    def issue(i, carry):
        tile_copy(tab_ref, slot, i).start()
        return carry

    def drain(t_ref, buf):
        def body(i, carry):
            tile_copy(t_ref, buf, i).wait()
            return carry
        lax.fori_loop(0, t_ref[TILE_TABLE - 1], body, 0)

    lax.fori_loop(0, tab_ref[TILE_TABLE - 1], issue, 0)

    @pl.when(r > 0)
    def _():
        drain(prev_tab_ref, 1 - slot)

    @pl.when(r == pl.num_programs(0) - 1)
    def _():
        drain(tab_ref, slot)


def _dispatch(v, idx, rank, plan, p_rows):
    t, d = v.shape
    tm = ROW_BLOCK
    n_exp = plan["loc_off"].shape[2]
    k_spec = pl.BlockSpec((tm, TOP_K), lambda r, *_: (r, 0))
    grid_spec = pltpu.PrefetchScalarGridSpec(
        num_scalar_prefetch=3,
        grid=(t // tm,),
        in_specs=[
            pl.BlockSpec((TILE_TABLE,), lambda r, *_: (r,), memory_space=pltpu.SMEM),
            pl.BlockSpec((TILE_TABLE,), lambda r, *_: (jnp.maximum(r - 1, 0),), memory_space=pltpu.SMEM),
            pl.BlockSpec((tm, d), lambda r, *_: (r, 0)),
            k_spec, k_spec,
            pl.BlockSpec((None, 1, n_exp), lambda r, *_: (r, 0, 0)),
        ],
        out_specs=pl.BlockSpec(memory_space=pl.ANY),
        scratch_shapes=[pltpu.VMEM((2, _local_rows(n_exp), d // 2), jnp.uint32),
                        pltpu.VMEM((EXPERT_BLOCK, d // 2), jnp.uint32),
                        pltpu.SemaphoreType.DMA((2,)), pltpu.SemaphoreType.DMA],
    )
    return pl.pallas_call(
        _dispatch_kernel,
        grid_spec=grid_spec,
        out_shape=jax.ShapeDtypeStruct((p_rows, d // 2), jnp.uint32),
        compiler_params=_cparams("arbitrary"),
        name="moe_dispatch",
    )(plan["tail_start"], plan["pad_len"], plan["n_used"], plan["tile_table"], plan["tile_table"],
      v, idx, rank, plan["loc_off"])


def _expert_kernel(be_ref, nu_ref, x_ref, w1_ref, b1_ref, w2_ref, b2_ref, o_ref, w1b_ref, w2b_ref):
    j = pl.program_id(0)
    f = w2_ref.shape[0]
    prev = be_ref[jnp.maximum(j - 1, 0)]

    @pl.when(jnp.logical_and(j < nu_ref[0], jnp.logical_or(j == 0, be_ref[j] != prev)))
    def _():
        w1b_ref[...] = w1_ref[...].astype(BF16)
        w2b_ref[...] = w2_ref[...].astype(BF16)

    @pl.when(j < nu_ref[0])
    def _():
        half = x_ref.shape[1]
        x_hi, x_lo = _unpack_halves(x_ref[...])
        h = (jnp.dot(x_hi, w1b_ref[:half, :], preferred_element_type=F32)
             + jnp.dot(x_lo, w1b_ref[half:, :], preferred_element_type=F32)) + b1_ref[...]
        glu = jnp.minimum(h[:, :f], SWIGLU_LIMIT)
        lin = jnp.clip(h[:, f:], -SWIGLU_LIMIT, SWIGLU_LIMIT)
        act = glu * _sigmoid(SWIGLU_ALPHA * glu) * (lin + 1.0)
        y = jnp.dot(act.astype(BF16), w2b_ref[...], preferred_element_type=F32) + b2_ref[...]
        o_ref[...] = _pack_halves(y.astype(BF16).astype(F32))

    @pl.when(j >= nu_ref[0])
    def _():
        o_ref[...] = jnp.zeros_like(o_ref)


def _experts(xb, block_e, n_used, layer, w1, b1, w2, b2):
    depth, n_exp, d, f2 = w1.shape
    f = f2 // 2
    bm = EXPERT_BLOCK
    n_blocks = block_e.shape[0]
    last = lambda j, nu: jnp.minimum(j, nu[0] - 1)
    grid_spec = pltpu.PrefetchScalarGridSpec(
        num_scalar_prefetch=2,
        grid=(n_blocks,),
        in_specs=[
            pl.BlockSpec((bm, d // 2), lambda j, be, nu: (last(j, nu), 0)),
            pl.BlockSpec((None, None, d, f2), lambda j, be, nu: (layer, be[last(j, nu)], 0, 0)),
            pl.BlockSpec((None, None, 1, f2), lambda j, be, nu: (layer, be[last(j, nu)], 0, 0)),
            pl.BlockSpec((None, None, f, d), lambda j, be, nu: (layer, be[last(j, nu)], 0, 0)),
            pl.BlockSpec((None, None, 1, d), lambda j, be, nu: (layer, be[last(j, nu)], 0, 0)),
        ],
        out_specs=pl.BlockSpec((bm, d // 2), lambda j, be, nu: (j, 0)),
        scratch_shapes=[pltpu.VMEM((d, f2), BF16), pltpu.VMEM((f, d), BF16)],
    )
    return pl.pallas_call(
        _expert_kernel,
        grid_spec=grid_spec,
        out_shape=jax.ShapeDtypeStruct((n_blocks * bm, d // 2), jnp.uint32),
        compiler_params=_cparams("arbitrary"),
        name="moe_experts",
    )(block_e, n_used, xb, w1, b1.reshape(depth, n_exp, 1, f2), w2, b2.reshape(depth, n_exp, 1, d))


def _combine_kernel(alpha, tab_ref, next_tab_ref, idx_ref, rank_ref, off_ref, gate_ref, x1_ref, mod_ref,
                    g_ref, beta_ref, yb_ref, o_ref, rows_ref, sems):
    r = pl.program_id(0)
    tm = x1_ref.shape[0]
    rows = rows_ref.shape[1]
    slot = r % 2

    def tile_copy(t_ref, into, i):
        return pltpu.make_async_copy(
            yb_ref.at[pl.ds(pl.multiple_of(t_ref[i] * SUBLANES, SUBLANES), SUBLANES)],
            rows_ref.at[into, pl.ds(pl.multiple_of(i * SUBLANES, SUBLANES), SUBLANES)], sems.at[into])

    def gather(t_ref, into):
        def issue(i, carry):
            tile_copy(t_ref, into, i).start()
            return carry
        lax.fori_loop(0, t_ref[TILE_TABLE - 1], issue, 0)

    @pl.when(r == 0)
    def _():
        gather(tab_ref, 0)

    @pl.when(r + 1 < pl.num_programs(0))
    def _():
        gather(next_tab_ref, 1 - slot)

    n_tiles = tab_ref[TILE_TABLE - 1]

    def drain(i, carry):
        tile_copy(tab_ref, slot, i).wait()
        return carry

    lax.fori_loop(0, n_tiles, drain, 0)

    pos = _local_pos(idx_ref, rank_ref, off_ref)
    lane = lax.broadcasted_iota(jnp.int32, (tm, rows), 1).astype(F32)
    gates = gate_ref[...]
    unperm = jnp.where(lane == pos[0], gates[:, 0:1], 0.0)
    for k in range(1, TOP_K):
        unperm = unperm + jnp.where(lane == pos[k], gates[:, k:k + 1], 0.0)
    row = lax.broadcasted_iota(jnp.int32, (rows, 1), 0)
    y_hi, y_lo = _unpack_halves(jnp.where(row < n_tiles * SUBLANES, rows_ref[slot], jnp.uint32(0)))
    unperm = unperm.astype(BF16)
    f = jnp.concatenate([jnp.dot(unperm, y_hi, preferred_element_type=F32),
                         jnp.dot(unperm, y_lo, preferred_element_type=F32)], axis=1)
    o_ref[...] = _layer_norm(alpha * x1_ref[...] + mod_ref[5:6, :] * f, g_ref[...], beta_ref[...])


def _combine(idx, rank, gates, x1, mod, ln_g, ln_b, yb, plan, alpha, nl, nc):
    t, d = x1.shape
    tm = ROW_BLOCK
    n_exp = plan["loc_off"].shape[2]
    row_spec = pl.BlockSpec((tm, d), lambda r: (r, 0))
    vec = pl.BlockSpec((1, d), lambda r: (0, 0))
    k_spec = pl.BlockSpec((tm, TOP_K), lambda r: (r, 0))
    return pl.pallas_call(
        functools.partial(_combine_kernel, alpha),
        grid=(t // tm,),
        in_specs=[
            pl.BlockSpec((TILE_TABLE,), lambda r: (r,), memory_space=pltpu.SMEM),
            pl.BlockSpec((TILE_TABLE,), lambda r: (jnp.minimum(r + 1, t // tm - 1),), memory_space=pltpu.SMEM),
            k_spec, k_spec,
            pl.BlockSpec((None, 1, n_exp), lambda r: (r, 0, 0)),
            k_spec,
            row_spec,
            pl.BlockSpec((None, 6, d), lambda r, f=_mod_row_map(nl, nc): (f(r), 0, 0)),
            vec, vec,
            pl.BlockSpec(memory_space=pl.ANY),
        ],
        out_specs=row_spec,
        out_shape=jax.ShapeDtypeStruct((t, d), F32),
        scratch_shapes=[pltpu.VMEM((2, _local_rows(n_exp), d // 2), jnp.uint32), pltpu.SemaphoreType.DMA((2,))],
        compiler_params=_cparams("arbitrary"),
        name="moe_combine_ln",
    )(plan["tile_table"], plan["tile_table"], idx, rank, plan["loc_off"], gates, x1, mod,
      ln_g.reshape(1, d), ln_b.reshape(1, d), yb)


def _moe_plan(cnt, n_blocks):
    nblk, n_exp = cnt.shape
    bm = EXPERT_BLOCK
    n_tiles_max = _local_rows(n_exp) // SUBLANES
    experts = jnp.arange(n_exp)
    blocks = jnp.arange(nblk)
    seg = (cnt + SUBLANES - 1) // SUBLANES * SUBLANES
    loc_off = jnp.sum(jnp.where(experts[None, None, :] < experts[None, :, None], seg[:, None, :], 0), axis=2)
    before = jnp.sum(jnp.where(blocks[None, :, None] < blocks[:, None, None], seg[None, :, :], 0), axis=1)
    rows_e = jnp.sum(seg, axis=0)
    padded = (rows_e + bm - 1) // bm * bm
    pend = jnp.sum(jnp.where(experts[None, :] <= experts[:, None], padded[None, :], 0), axis=1)
    pstart = pend - padded
    base = pstart[None, :] + before
    first_row = (jnp.arange(n_tiles_max) * SUBLANES)[None, :, None]
    e_of = jnp.sum(((loc_off + seg)[:, None, :] <= first_row).astype(jnp.int32), axis=2)
    pick = jnp.minimum(e_of, n_exp - 1)[:, :, None] == experts
    delta = jnp.sum(jnp.where(pick, (base - loc_off)[:, None, :], 0), axis=2)
    tile_of = (delta + first_row[:, :, 0]) // SUBLANES
    table = jnp.concatenate([tile_of.astype(jnp.int32),
                             jnp.zeros((nblk, TILE_TABLE - 1 - n_tiles_max), jnp.int32),
                             (jnp.sum(seg, axis=1, keepdims=True) // SUBLANES).astype(jnp.int32)], axis=1)
    block_e = jnp.sum((pend[None, :] <= (jnp.arange(n_blocks) * bm)[:, None]).astype(jnp.int32), axis=1)
    return dict(
        loc_off=loc_off.astype(F32).reshape(nblk, 1, n_exp),
        tile_table=table.reshape(-1),
        tail_start=(pstart + rows_e).astype(jnp.int32),
        pad_len=(padded - rows_e).astype(jnp.int32),
        block_e=jnp.minimum(block_e, n_exp - 1).astype(jnp.int32),
        n_used=(pend[-1:] // bm).astype(jnp.int32),
    )


def kernel(x, c, ctx, c_ctx, w_mod, b_mod, ln1_g, ln1_b, ln2_g, ln2_b, attn_w_qkv, attn_w_o, attn_lam_q1, attn_lam_k1, attn_lam_q2, attn_lam_k2, attn_subln_g, fnet_w, fnet_b, conv_w_pw1, conv_b_pw1, conv_w_dw, conv_b_dw, conv_ln_g, conv_ln_b, conv_w_pw2, conv_b_pw2, moe_w_router, moe_b_router, moe_w1, moe_b1, moe_w2, moe_b2):
    b, n, d = x.shape
    cl = ctx.shape[1]
    s = n + cl
    t = b * s
    depth = w_mod.shape[0]
    n_exp = moe_w_router.shape[2]
    nl, nc = n // ROW_BLOCK, cl // ROW_BLOCK
    alpha = (2 * depth) ** 0.25
    worst_rows = t * TOP_K + (t // ROW_BLOCK) * n_exp * (SUBLANES - 1)
    n_blocks = -(-worst_rows // EXPERT_BLOCK) + n_exp
    p_rows = n_blocks * EXPERT_BLOCK

    xs = jnp.concatenate([x, ctx], axis=1).reshape(t, d)
    cond = jnp.stack([c, jnp.broadcast_to(c_ctx, c.shape)], axis=1).reshape(2 * b, d)
    mod = _modulation(cond, w_mod, b_mod)
    cos, sin = _rope_tables(n, cl)

    for i in range(depth):
        kind, j = i % 3, i // 3
        if kind == 0:
            lam_init = 0.8 - 0.6 * float(np.exp(-0.3 * i))
            q, k, v = _qkv(xs, mod[i], attn_w_qkv[j], cos, sin, nl, nc)
            q3, k3 = q.reshape(b, s, d), k.reshape(b, s, d)
            vt = v.reshape(b, s, d // HEAD_W, HEAD_W).transpose(0, 2, 3, 1)
            vt = jnp.concatenate([vt, jnp.ones((b, d // HEAD_W, SUM_ROWS, s), BF16)], axis=2)
            lam_params = (attn_lam_q1[j], attn_lam_k1[j], attn_lam_q2[j], attn_lam_k2[j])
            o_lat = _attention(q3, k3, vt, lam_params, attn_subln_g[j], lam_init, n, cl, context=False)
            o_ctx = _attention(q3, k3, vt, lam_params, attn_subln_g[j], lam_init, n, cl, context=True)
            a = jnp.concatenate([o_lat, o_ctx], axis=1).reshape(t, d)
            w_out, b_out = attn_w_o[j], jnp.zeros((d,), F32)
        elif kind == 1:
            fa, fb = _chan_dft(xs, mod[i], nl, nc)
            fa3, fb3 = fa.reshape(b, s, d), fb.reshape(b, s, d)
            a = jnp.concatenate([_seq_dft(fa3, fb3, n, 0), _seq_dft(fa3, fb3, cl, n)], axis=1).reshape(t, d)
            w_out, b_out = fnet_w[j], fnet_b[j]
        else:
            hid = _glu(xs, mod[i], conv_w_pw1[j], conv_b_pw1[j], nl, nc)
            a = _dwconv_ln_silu(hid, conv_w_dw[j], conv_b_dw[j], conv_ln_g[j], conv_ln_b[j], nl, nc)
            w_out, b_out = conv_w_pw2[j], conv_b_pw2[j]

        x1, v, idx, gates, rank, cnt = _proj_route(
            a, w_out, b_out, xs, mod[i], ln1_g[i], ln1_b[i], moe_w_router[i], moe_b_router[i], alpha, nl, nc)
        plan = _moe_plan(cnt[:, 0, :], n_blocks)
        xb = _dispatch(v, idx, rank, plan, p_rows)
        yb = _experts(xb, plan["block_e"], plan["n_used"], i, moe_w1, moe_b1, moe_w2, moe_b2)
        xs = _combine(idx, rank, gates, x1, mod[i], ln2_g[i], ln2_b[i], yb, plan, alpha, nl, nc)

    return xs.reshape(b, s, d)[:, :n]
```

```python
import functools

import jax
import jax.numpy as jnp
import numpy as np
from jax import lax
from jax.experimental import pallas as pl
from jax.experimental.pallas import tpu as pltpu

F32 = jnp.float32
BF16 = jnp.bfloat16
HIGHEST = lax.Precision.HIGHEST

GRID_W = 64
HEAD_W = 128
MAP_W = HEAD_W // 2
ROPE_FREQS = MAP_W // 4
ROPE_BASE = 10000.0
CONV_WIDTH = 31
CONV_PAD = CONV_WIDTH // 2
TOP_K = 4
SWIGLU_ALPHA = 1.702
SWIGLU_LIMIT = 7.0
LN_EPS = 1e-5

LANES = 128
SUBLANES = 8
ROW_BLOCK = 256
SUM_ROWS = 8
HALO = 16
ATTN_TQ = 512
ATTN_TK = 512
ATTN_QC = 256
DFT_TILE = 1024
EXPERT_BLOCK = 512
TILE_TABLE = 1024
ISSUE_UNROLL = 4
WAIT_TILES = 16
VMEM_LIMIT = 56 * 1024 * 1024


def _cparams(*sem):
    return pltpu.CompilerParams(dimension_semantics=sem, vmem_limit_bytes=VMEM_LIMIT)


def _mod_row_map(nl, nc):
    rb = nl + nc

    def f(r):
        return 2 * (r // rb) + jnp.where((r % rb) >= nl, 1, 0)

    return f


def _layer_norm(z, g, b):
    mu = jnp.mean(z, axis=-1, keepdims=True)
    zc = z - mu
    var = jnp.mean(zc * zc, axis=-1, keepdims=True)
    return zc * lax.rsqrt(var + LN_EPS) * g + b


def _sigmoid(x):
    return 1.0 / (1.0 + jnp.exp(-x))


def _modulation_kernel(cond_ref, w_ref, b_ref, o_ref):
    cnd = cond_ref[...]
    s = cnd * _sigmoid(cnd)
    o_ref[...] = jnp.dot(s, w_ref[...], precision=HIGHEST, preferred_element_type=F32) + b_ref[...]


def _modulation(cond, w_mod, b_mod):
    depth, d, d6 = w_mod.shape
    rows = cond.shape[0]
    out = pl.pallas_call(
        _modulation_kernel,
        grid=(depth, d6 // d),
        in_specs=[
            pl.BlockSpec((rows, d), lambda i, j: (0, 0)),
            pl.BlockSpec((None, d, d), lambda i, j: (i, 0, j)),
            pl.BlockSpec((None, 1, d), lambda i, j: (i, 0, j)),
        ],
        out_specs=pl.BlockSpec((None, rows, d), lambda i, j: (i, 0, j)),
        out_shape=jax.ShapeDtypeStruct((depth, rows, d6), F32),
        compiler_params=_cparams("parallel", "parallel"),
        name="modulation",
    )(cond, w_mod, b_mod.reshape(depth, 1, d6))
    return out.reshape(depth, rows, d6 // d, d)


def _rope_tables(n, c):
    pos = jnp.arange(n)
    rows = (pos // GRID_W).astype(F32)
    cols = (pos % GRID_W).astype(F32)
    inv_freq = ROPE_BASE ** (-jnp.arange(ROPE_FREQS, dtype=F32) / ROPE_FREQS)
    lane = np.arange(HEAD_W)
    dim = lane % MAP_W
    freq = dim % ROPE_FREQS
    use_row = (dim // (MAP_W // 2)) == 0
    first_half = (dim % (MAP_W // 2)) < ROPE_FREQS
    ang = jnp.where(use_row[None, :], rows[:, None] * inv_freq[freq][None, :],
                    cols[:, None] * inv_freq[freq][None, :])
    cos = jnp.cos(ang)
    sin = jnp.sin(ang)
    sin = jnp.where(first_half[None, :], -sin, sin)
    cos = jnp.concatenate([cos, jnp.ones((c, HEAD_W), F32)], axis=0)
    sin = jnp.concatenate([sin, jnp.zeros((c, HEAD_W), F32)], axis=0)
    return cos, sin


def _qkv_kernel(x_ref, mod_ref, w_ref, cos_ref, sin_ref, q_ref, k_ref, v_ref):
    d = x_ref.shape[1]
    u = (x_ref[...] * (1.0 + mod_ref[1:2, :]) + mod_ref[0:1, :]).astype(BF16)
    cos = cos_ref[...]
    sin = sin_ref[...]
    lane = lax.broadcasted_iota(jnp.int32, cos.shape, 1)
    first_half = (lane % (MAP_W // 2)) < ROPE_FREQS

    def rope(t):
        partner = jnp.where(first_half, pltpu.roll(t, HEAD_W - ROPE_FREQS, 1), pltpu.roll(t, ROPE_FREQS, 1))
        return t * cos + partner * sin

    q = jnp.dot(u, w_ref[:, :d], preferred_element_type=F32)
    k = jnp.dot(u, w_ref[:, d:2 * d], preferred_element_type=F32)
    for h in range(d // HEAD_W):
        cs = slice(h * HEAD_W, (h + 1) * HEAD_W)
        q_ref[:, cs] = rope(q[:, cs]).astype(BF16)
        k_ref[:, cs] = rope(k[:, cs]).astype(BF16)
    v_ref[...] = jnp.dot(u, w_ref[:, 2 * d:], preferred_element_type=F32).astype(BF16)


def _qkv(x, mod, w_qkv, cos, sin, nl, nc):
    t, d = x.shape
    rb = nl + nc
    tm = ROW_BLOCK
    scale = jnp.concatenate([jnp.full((d,), np.log2(np.e) * MAP_W ** -0.5, F32), jnp.ones((2 * d,), F32)])
    w = (w_qkv * scale[None, :]).astype(BF16)
    row_spec = pl.BlockSpec((tm, d), lambda r: (r, 0))
    tab_spec = pl.BlockSpec((tm, HEAD_W), lambda r: (r % rb, 0))
    return pl.pallas_call(
        _qkv_kernel,
        grid=(t // tm,),
        in_specs=[
            row_spec,
            pl.BlockSpec((None, 6, d), lambda r, f=_mod_row_map(nl, nc): (f(r), 0, 0)),
            pl.BlockSpec((d, 3 * d), lambda r: (0, 0)),
            tab_spec, tab_spec,
        ],
        out_specs=[row_spec, row_spec, row_spec],
        out_shape=[jax.ShapeDtypeStruct((t, d), BF16)] * 3,
        compiler_params=_cparams("parallel"),
        name="qkv_rope",
    )(x, mod, w, cos, sin)


def _attn_kernel(lam_init, tiles, q_ref, k_ref, vt_ref, lq1_ref, lk1_ref, lq2_ref, lk2_ref, g_ref,
                 o_ref, acc1_ref, acc2_ref, sa1_ref, sa2_ref, sb1_ref, sb2_ref):
    tq = q_ref.shape[0]
    q = q_ref[...].astype(F32)
    lane = lax.broadcasted_iota(jnp.int32, q.shape, 1)
    q1 = jnp.where(lane < MAP_W, q, 0.0).astype(BF16)
    q2 = jnp.where(lane >= MAP_W, q, 0.0).astype(BF16)
    acc1_ref[...] = jnp.zeros_like(acc1_ref)
    acc2_ref[...] = jnp.zeros_like(acc2_ref)
    acc_refs = (acc1_ref, acc2_ref)
    s_refs = ((sa1_ref, sa2_ref), (sb1_ref, sb2_ref))

    qc = min(tq, ATTN_QC)
    pieces = [(mp, c) for c in range(tq // qc) for mp in range(2)]
    qms = (q1, q2)

    def score_piece(i, mp, c):
        off, size = tiles[i]
        s = lax.dot_general(k_ref[off:off + size, :], qms[mp][c * qc:(c + 1) * qc, :],
                            (((1,), (1,)), ((), ())), preferred_element_type=F32)
        s_refs[i % 2][mp][0:size, c * qc:(c + 1) * qc] = s
        return jnp.max(s, axis=0, keepdims=True)

    def acc_piece(i, mp, c, cm, m):
        off, size = tiles[i]
        cs = slice(c * qc, (c + 1) * qc)
        m_new = jnp.maximum(m, cm)
        p = jnp.exp2(s_refs[i % 2][mp][0:size, cs] - m_new).astype(BF16)
        acc_refs[mp][:, cs] = (jnp.exp2(m - m_new) * acc_refs[mp][:, cs]
                               + jnp.dot(vt_ref[:, off:off + size], p, preferred_element_type=F32))
        return m_new

    def stage(i_score, i_acc, cm_acc, ms):
        cm_new, ms_new = [], []
        for n_piece, (mp, c) in enumerate(pieces):
            if i_score is not None:
                cm_new.append(score_piece(i_score, mp, c))
            if i_acc is not None:
                ms_new.append(acc_piece(i_acc, mp, c, cm_acc[n_piece], ms[n_piece]))
        return tuple(cm_new), tuple(ms_new) if i_acc is not None else ms

    ms = tuple(jnp.full((1, qc), -jnp.inf, F32) for _ in pieces)
    cm, _ = stage(0, None, None, ms)
    for i in range(len(tiles) - 1):
        cm, ms = stage(i + 1, i, cm, ms)
    stage(None, len(tiles) - 1, cm, ms)

    lam = (jnp.exp(jnp.sum(lq1_ref[...] * lk1_ref[...], axis=1, keepdims=True))
           - jnp.exp(jnp.sum(lq2_ref[...] * lk2_ref[...], axis=1, keepdims=True)) + lam_init)
    inv1 = 1.0 / acc1_ref[HEAD_W:HEAD_W + 1, :]
    inv2 = 1.0 / acc2_ref[HEAD_W:HEAD_W + 1, :]
    o = acc1_ref[0:HEAD_W, :] * inv1 - lam * (acc2_ref[0:HEAD_W, :] * inv2)
    ms = jnp.mean(o * o, axis=0, keepdims=True)
    o = o * lax.rsqrt(ms + LN_EPS) * (g_ref[...] * (1.0 - lam_init))
    o_ref[...] = o.T.astype(BF16)


def _attention(q3, k3, vt, lam_params, subln_g, lam_init, n, c, context):
    b, s, d = q3.shape
    h = d // HEAD_W
    if context:
        tq, kv_len, kv_blk, q_off, nq = c, c, n // c, n // c, 1
    else:
        tq, kv_len, kv_blk, q_off, nq = ATTN_TQ, s, 0, 0, n // ATTN_TQ
    tk = min(ATTN_TK, kv_len)
    tiles = tuple((off, min(tk, kv_len - off)) for off in range(0, kv_len, tk))
    lam_spec = pl.BlockSpec((1, MAP_W), lambda bi, hi, qi: (0, 0))
    return pl.pallas_call(
        functools.partial(_attn_kernel, lam_init, tiles),
        grid=(b, h, nq),
        in_specs=[
            pl.BlockSpec((None, tq, HEAD_W), lambda bi, hi, qi: (bi, qi + q_off, hi)),
            pl.BlockSpec((None, kv_len, HEAD_W), lambda bi, hi, qi: (bi, kv_blk, hi)),
            pl.BlockSpec((None, None, HEAD_W + SUM_ROWS, kv_len), lambda bi, hi, qi: (bi, hi, 0, kv_blk)),
            lam_spec, lam_spec, lam_spec, lam_spec,
            pl.BlockSpec((HEAD_W, 1), lambda bi, hi, qi: (0, 0)),
        ],
        out_specs=pl.BlockSpec((None, tq, HEAD_W), lambda bi, hi, qi: (bi, qi, hi)),
        out_shape=jax.ShapeDtypeStruct((b, tq * nq, d), BF16),
        scratch_shapes=[pltpu.VMEM((HEAD_W + SUM_ROWS, tq), F32)] * 2 + [pltpu.VMEM((tk, tq), F32)] * 4,
        compiler_params=_cparams("parallel", "parallel", "parallel"),
        name="attn_ctx" if context else "attn_lat",
    )(q3, k3, vt, *[p.reshape(1, MAP_W) for p in lam_params], subln_g.reshape(HEAD_W, 1))


def _dft_matrices(n, scale):
    blk = min(ROW_BLOCK, n)
    k = jnp.arange(n, dtype=jnp.int32)

    def cs(j):
        r = (j[:, None] * k[None, :]) % n
        ang = r.astype(F32) * (2.0 * np.pi / n)
        return jnp.cos(ang), jnp.sin(ang)

    c0, s0 = cs(jnp.arange(blk, dtype=jnp.int32))
    cj, sj = cs(jnp.arange(0, n, blk, dtype=jnp.int32))
    cm = cj[:, None, :] * c0[None] - sj[:, None, :] * s0[None]
    sm = sj[:, None, :] * c0[None] + cj[:, None, :] * s0[None]
    return (cm * scale).reshape(n, n).astype(BF16), (sm * scale).reshape(n, n).astype(BF16)


def _chan_dft_kernel(x_ref, mod_ref, cs_ref, a_ref, b_ref):
    d = x_ref.shape[1]
    u = (x_ref[...] * (1.0 + mod_ref[1:2, :]) + mod_ref[0:1, :]).astype(BF16)
    for g in range(d // LANES):
        sl = slice(g * LANES, (g + 1) * LANES)
        ab = jnp.dot(u[:, sl], cs_ref[...], preferred_element_type=F32)
        a_ref[:, sl] = ab[:, :LANES].astype(BF16)
        b_ref[:, sl] = ab[:, LANES:].astype(BF16)


def _chan_dft(x, mod, nl, nc):
    t, d = x.shape
    tm = ROW_BLOCK
    j = np.arange(LANES)
    ang = 2.0 * np.pi * ((j[:, None] * j[None, :]) % LANES) / LANES
    cs = jnp.asarray(np.concatenate([np.cos(ang), np.sin(ang)], axis=1) / np.sqrt(LANES), BF16)
    row_spec = pl.BlockSpec((tm, d), lambda r: (r, 0))
    return pl.pallas_call(
        _chan_dft_kernel,
        grid=(t // tm,),
        in_specs=[
            row_spec,
            pl.BlockSpec((None, 6, d), lambda r, f=_mod_row_map(nl, nc): (f(r), 0, 0)),
            pl.BlockSpec((LANES, 2 * LANES), lambda r: (0, 0)),
        ],
        out_specs=[row_spec, row_spec],
        out_shape=[jax.ShapeDtypeStruct((t, d), BF16)] * 2,
        compiler_params=_cparams("parallel"),
        name="chan_dft",
    )(x, mod, cs)


def _seq_dft_kernel(c_ref, s_ref, a_ref, b_ref, o_ref, acc_ref):
    kk = pl.program_id(2)

    @pl.when(kk == 0)
    def _():
        acc_ref[...] = jnp.zeros_like(acc_ref)

    acc_ref[...] += (jnp.dot(c_ref[...], a_ref[...], preferred_element_type=F32)
                     - jnp.dot(s_ref[...], b_ref[...], preferred_element_type=F32))

    @pl.when(kk == pl.num_programs(2) - 1)
    def _():
        o_ref[...] = acc_ref[...].astype(o_ref.dtype)


def _seq_dft(a3, b3, length, row_off):
    b, s, d = a3.shape
    tile = min(DFT_TILE, length)
    cm, sm = _dft_matrices(length, length ** -0.5)
    nt = length // tile
    off = row_off // tile
    mat_spec = pl.BlockSpec((tile, tile), lambda bi, m, k: (m, k))
    in_spec = pl.BlockSpec((None, tile, d), lambda bi, m, k: (bi, k + off, 0))
    return pl.pallas_call(
        _seq_dft_kernel,
        grid=(b, nt, nt),
        in_specs=[mat_spec, mat_spec, in_spec, in_spec],
        out_specs=pl.BlockSpec((None, tile, d), lambda bi, m, k: (bi, m, 0)),
        out_shape=jax.ShapeDtypeStruct((b, length, d), BF16),
        scratch_shapes=[pltpu.VMEM((tile, d), F32)],
        compiler_params=_cparams("parallel", "parallel", "arbitrary"),
        name="seq_dft",
    )(cm, sm, a3, b3)


def _glu_kernel(x_ref, mod_ref, w_ref, b_ref, o_ref):
    d = x_ref.shape[1]
    u = (x_ref[...] * (1.0 + mod_ref[1:2, :]) + mod_ref[0:1, :]).astype(BF16)
    a = jnp.dot(u, w_ref[:, :d], preferred_element_type=F32) + b_ref[:, :d]
    g = jnp.dot(u, w_ref[:, d:], preferred_element_type=F32) + b_ref[:, d:]
    o_ref[...] = a * _sigmoid(g)


def _glu(x, mod, w, bias, nl, nc):
    t, d = x.shape
    tm = ROW_BLOCK
    row_spec = pl.BlockSpec((tm, d), lambda r: (r, 0))
    return pl.pallas_call(
        _glu_kernel,
        grid=(t // tm,),
        in_specs=[
            row_spec,
            pl.BlockSpec((None, 6, d), lambda r, f=_mod_row_map(nl, nc): (f(r), 0, 0)),
            pl.BlockSpec((d, 2 * d), lambda r: (0, 0)),
            pl.BlockSpec((1, 2 * d), lambda r: (0, 0)),
        ],
        out_specs=row_spec,
        out_shape=jax.ShapeDtypeStruct((t, d), F32),
        compiler_params=_cparams("parallel"),
        name="pw1_glu",
    )(x, mod, w.astype(BF16), bias.reshape(1, 2 * d))


def _dwconv_kernel(nl, nc, prev_ref, cur_ref, next_ref, w_ref, b_ref, g_ref, beta_ref, o_ref, win_ref, h_ref):
    tm, d = cur_ref.shape
    rr = pl.program_id(0) % (nl + nc)
    has_prev = jnp.logical_and(rr != 0, rr != nl)
    has_next = jnp.logical_and(rr != nl - 1, rr != nl + nc - 1)
    win_ref[0, 0:HALO, :] = jnp.where(has_prev, prev_ref[...], 0.0)
    win_ref[0, HALO:HALO + tm, :] = cur_ref[...]
    win_ref[0, HALO + tm:, :] = jnp.where(has_next, next_ref[...], 0.0)
    span = tm + 2 * HALO - SUBLANES
    for s in range(1, SUBLANES):
        win_ref[s, 0:span, :] = win_ref[0, s:s + span, :]
    rows = 64
    for c in range(d // LANES):
        cs = slice(c * LANES, (c + 1) * LANES)
        wc = w_ref[:, cs]
        for r0 in range(0, tm, rows):
            acc = jnp.broadcast_to(b_ref[:, cs], (rows, LANES))
            for j in range(CONV_WIDTH):
                off = HALO - CONV_PAD + j
                start = r0 + off - off % SUBLANES
                acc = acc + wc[j:j + 1, :] * win_ref[off % SUBLANES, start:start + rows, cs]
            h_ref[r0:r0 + rows, cs] = acc
    y = _layer_norm(h_ref[...], g_ref[...], beta_ref[...])
    o_ref[...] = (y * _sigmoid(y)).astype(BF16)


def _dwconv_ln_silu(hid, w_dw, b_dw, ln_g, ln_b, nl, nc):
    t, d = hid.shape
    tm = ROW_BLOCK
    per = tm // HALO
    n_halo = t // HALO
    vec = pl.BlockSpec((1, d), lambda r: (0, 0))
    return pl.pallas_call(
        functools.partial(_dwconv_kernel, nl, nc),
        grid=(t // tm,),
        in_specs=[
            pl.BlockSpec((HALO, d), lambda r: (jnp.maximum(r * per - 1, 0), 0)),
            pl.BlockSpec((tm, d), lambda r: (r, 0)),
            pl.BlockSpec((HALO, d), lambda r: (jnp.minimum((r + 1) * per, n_halo - 1), 0)),
            pl.BlockSpec((CONV_WIDTH, d), lambda r: (0, 0)),
            vec, vec, vec,
        ],
        out_specs=pl.BlockSpec((tm, d), lambda r: (r, 0)),
        out_shape=jax.ShapeDtypeStruct((t, d), BF16),
        scratch_shapes=[pltpu.VMEM((SUBLANES, tm + 2 * HALO, d), F32), pltpu.VMEM((tm, d), F32)],
        compiler_params=_cparams("parallel"),
        name="dwconv_ln_silu",
    )(hid, hid, hid, w_dw, b_dw.reshape(1, d), ln_g.reshape(1, d), ln_b.reshape(1, d))


def _pack_cols(cols, dtype):
    rows = cols[0].shape[0]
    lane = lax.broadcasted_iota(jnp.int32, (rows, len(cols)), 1)
    out = jnp.zeros((rows, len(cols)), dtype)
    for i, col in enumerate(cols):
        out = jnp.where(lane == i, col.astype(dtype), out)
    return out


def _proj_route_kernel(alpha, a_ref, w_ref, bias_ref, x_ref, mod_ref, g_ref, beta_ref, wr_ref, br_ref,
                       x1_ref, v_ref, idx_ref, gate_ref, rank_ref, cnt_ref):
    tm = x_ref.shape[0]
    n_exp = wr_ref.shape[2]

    y = jnp.dot(a_ref[...], w_ref[...], preferred_element_type=F32) + bias_ref[...]
    x1 = _layer_norm(alpha * x_ref[...] + mod_ref[2:3, :] * y, g_ref[...], beta_ref[...])
    x1_ref[...] = x1
    v = x1 * (1.0 + mod_ref[4:5, :]) + mod_ref[3:4, :]
    v_ref[...] = v

    v_hi = v.astype(BF16)
    v_lo = (v - v_hi.astype(F32)).astype(BF16)
    logits = (jnp.dot(v_hi, wr_ref[0], preferred_element_type=F32)
              + jnp.dot(v_lo, wr_ref[0], preferred_element_type=F32)
              + jnp.dot(v_hi, wr_ref[1], preferred_element_type=F32)) + br_ref[...]
    col = lax.broadcasted_iota(jnp.int32, logits.shape, 1).astype(F32)
    work = logits
    vals, idxs, sels = [], [], []
    for _ in range(TOP_K):
        mx = jnp.max(work, axis=1, keepdims=True)
        first = jnp.min(jnp.where(work == mx, col, float(n_exp)), axis=1, keepdims=True)
        sel = col == first
        work = jnp.where(sel, -jnp.inf, work)
        vals.append(mx)
        idxs.append(first)
        sels.append(sel)
    es = [jnp.exp(val - vals[0]) for val in vals]
    inv = 1.0 / (es[0] + es[1] + es[2] + es[3])

    onehot = jnp.zeros(logits.shape, F32)
    for sel in sels:
        onehot = onehot + jnp.where(sel, 1.0, 0.0)
    ri = lax.broadcasted_iota(jnp.int32, (tm, tm), 0)
    ci = lax.broadcasted_iota(jnp.int32, (tm, tm), 1)
    tri = jnp.where(ci < ri, 1.0, 0.0).astype(BF16)
    before = jnp.dot(tri, onehot.astype(BF16), preferred_element_type=F32)
    ranks = [jnp.sum(jnp.where(sel, before, 0.0), axis=1, keepdims=True) for sel in sels]

    idx_ref[...] = _pack_cols(idxs, jnp.int32)
    gate_ref[...] = _pack_cols([e * inv for e in es], F32)
    rank_ref[...] = _pack_cols(ranks, jnp.int32)
    cnt_ref[...] = jnp.sum(onehot, axis=0, keepdims=True).astype(jnp.int32)


def _proj_route(a, w, bias, x, mod, ln_g, ln_b, w_router, b_router, alpha, nl, nc):
    t, d = x.shape
    n_exp = w_router.shape[1]
    tm = ROW_BLOCK
    row_spec = pl.BlockSpec((tm, d), lambda r: (r, 0))
    vec = pl.BlockSpec((1, d), lambda r: (0, 0))
    k_spec = pl.BlockSpec((tm, TOP_K), lambda r: (r, 0))
    wr_hi = w_router.astype(BF16)
    return pl.pallas_call(
        functools.partial(_proj_route_kernel, alpha),
        grid=(t // tm,),
        in_specs=[
            row_spec,
            pl.BlockSpec((d, d), lambda r: (0, 0)),
            vec,
            row_spec,
            pl.BlockSpec((None, 6, d), lambda r, f=_mod_row_map(nl, nc): (f(r), 0, 0)),
            vec, vec,
            pl.BlockSpec((2, d, n_exp), lambda r: (0, 0, 0)),
            pl.BlockSpec((1, n_exp), lambda r: (0, 0)),
        ],
        out_specs=[row_spec, row_spec, k_spec, k_spec, k_spec,
                   pl.BlockSpec((None, 1, n_exp), lambda r: (r, 0, 0))],
        out_shape=[
            jax.ShapeDtypeStruct((t, d), F32),
            jax.ShapeDtypeStruct((t, d), F32),
            jax.ShapeDtypeStruct((t, TOP_K), jnp.int32),
            jax.ShapeDtypeStruct((t, TOP_K), F32),
            jax.ShapeDtypeStruct((t, TOP_K), jnp.int32),
            jax.ShapeDtypeStruct((t // tm, 1, n_exp), jnp.int32),
        ],
        compiler_params=_cparams("parallel"),
        name="proj_ln_route",
    )(a, w.astype(BF16), bias.reshape(1, d), x, mod, ln_g.reshape(1, d), ln_b.reshape(1, d),
      jnp.stack([wr_hi, (w_router - wr_hi.astype(F32)).astype(BF16)]), b_router.reshape(1, n_exp))


def _local_rows(n_exp):
    return TOP_K * ROW_BLOCK + n_exp * SUBLANES


def _for_tiles(n_tiles, body):
    def trip(g, carry):
        for u in range(ISSUE_UNROLL):
            body(g * ISSUE_UNROLL + u)
        return carry

    def single(i, carry):
        body(i)
        return carry

    full = n_tiles // ISSUE_UNROLL
    lax.fori_loop(0, full, trip, 0)
    lax.fori_loop(full * ISSUE_UNROLL, n_tiles, single, 0)


def _wait_tiles(n_tiles, many, one):
    def wait_many(g, carry):
        many().wait()
        return carry

    def wait_one(i, carry):
        one().wait()
        return carry

    lax.fori_loop(0, n_tiles // WAIT_TILES, wait_many, 0)
    lax.fori_loop(0, n_tiles % WAIT_TILES, wait_one, 0)


def _pack_halves(x):
    half = x.shape[1] // 2
    hi = lax.bitcast_convert_type(x[:, :half], jnp.uint32)
    lo = lax.bitcast_convert_type(x[:, half:], jnp.uint32)
    return (hi & jnp.uint32(0xFFFF0000)) | (lo >> 16)


def _unpack_halves(w):
    hi = lax.bitcast_convert_type(w & jnp.uint32(0xFFFF0000), F32)
    lo = lax.bitcast_convert_type(w << 16, F32)
    return hi.astype(BF16), lo.astype(BF16)


def _local_pos(idx_ref, rank_ref, off_ref):
    tm = idx_ref.shape[0]
    col = lax.broadcasted_iota(jnp.int32, (tm, off_ref.shape[1]), 1)
    idx = idx_ref[...]
    rank = rank_ref[...]
    pos = []
    for k in range(TOP_K):
        start = jnp.sum(jnp.where(col == idx[:, k:k + 1], off_ref[...], 0.0), axis=1, keepdims=True)
        pos.append(start + rank[:, k:k + 1].astype(F32))
    return pos


def _dispatch_kernel(tail_ref, pad_ref, nu_ref, tab_ref, prev_tab_ref, v_ref, idx_ref, rank_ref, off_ref,
                     xb_ref, srt_ref, zero_ref, sems, fill_sem):
    r = pl.program_id(0)
    slot = r % 2
    tm = v_ref.shape[0]
    rows = srt_ref.shape[1]
    bm = zero_ref.shape[0]
    n_blocks = xb_ref.shape[0] // bm

    @pl.when(pl.program_id(0) == 0)
    def _():
        zero_ref[...] = jnp.zeros_like(zero_ref)

        def padding(do):
            for e in range(tail_ref.shape[0]):
                pad = pad_ref[e]
                for bit in reversed(range(SUBLANES.bit_length() - 1, bm.bit_length() - 1)):
                    size = 1 << bit
                    off = pl.multiple_of(tail_ref[e] + ((pad >> (bit + 1)) << (bit + 1)), SUBLANES)
                    copy = pltpu.make_async_copy(zero_ref.at[pl.ds(0, size)], xb_ref.at[pl.ds(off, size)], fill_sem)
                    pl.when((pad & size) != 0)(functools.partial(do, copy))

        def blocks(do):
            def body(j, carry):
                do(pltpu.make_async_copy(zero_ref, xb_ref.at[pl.ds(pl.multiple_of(j * bm, bm), bm)], fill_sem))
                return carry
            lax.fori_loop(nu_ref[0], n_blocks, body, 0)

        padding(lambda copy: copy.start())
        blocks(lambda copy: copy.start())
        padding(lambda copy: copy.wait())
        blocks(lambda copy: copy.wait())

    pos = _local_pos(idx_ref, rank_ref, off_ref)
    lane = lax.broadcasted_iota(jnp.int32, (tm, LANES), 1)
    pos_cols = jnp.zeros((tm, LANES), F32)
    for k in range(TOP_K):
        pos_cols = jnp.where(lane == k, pos[k], pos_cols)
    pos_rows = pos_cols.T
    row = lax.broadcasted_iota(jnp.int32, (rows, tm), 0).astype(F32)
    hit = row == pos_rows[0:1, :]
    for k in range(1, TOP_K):
        hit = jnp.logical_or(hit, row == pos_rows[k:k + 1, :])
    perm = jnp.where(hit, 1.0, 0.0).astype(BF16)
    srt_ref[slot] = _pack_halves(jnp.dot(perm, v_ref[...].astype(BF16), preferred_element_type=F32))

    def tile_copy(t_ref, buf, i):
        return pltpu.make_async_copy(
            srt_ref.at[buf, pl.ds(pl.multiple_of(i * SUBLANES, SUBLANES), SUBLANES)],
            xb_ref.at[pl.ds(pl.multiple_of(t_ref[i] * SUBLANES, SUBLANES), SUBLANES)], sems.at[buf])

    def drain(t_ref, buf):
        def many():
            return pltpu.make_async_copy(srt_ref.at[buf, pl.ds(0, WAIT_TILES * SUBLANES)],
                                         xb_ref.at[pl.ds(0, WAIT_TILES * SUBLANES)], sems.at[buf])
        _wait_tiles(t_ref[TILE_TABLE - 1], many, lambda: tile_copy(t_ref, buf, 0))

    _for_tiles(tab_ref[TILE_TABLE - 1], lambda i: tile_copy(tab_ref, slot, i).start())

    @pl.when(r > 0)
    def _():
        drain(prev_tab_ref, 1 - slot)

    @pl.when(r == pl.num_programs(0) - 1)
    def _():
        drain(tab_ref, slot)


def _dispatch(v, idx, rank, plan, p_rows):
    t, d = v.shape
    tm = ROW_BLOCK
    n_exp = plan["loc_off"].shape[2]
    k_spec = pl.BlockSpec((tm, TOP_K), lambda r, *_: (r, 0))
    grid_spec = pltpu.PrefetchScalarGridSpec(
        num_scalar_prefetch=3,
        grid=(t // tm,),
        in_specs=[
            pl.BlockSpec((TILE_TABLE,), lambda r, *_: (r,), memory_space=pltpu.SMEM),
            pl.BlockSpec((TILE_TABLE,), lambda r, *_: (jnp.maximum(r - 1, 0),), memory_space=pltpu.SMEM),
            pl.BlockSpec((tm, d), lambda r, *_: (r, 0)),
            k_spec, k_spec,
            pl.BlockSpec((None, 1, n_exp), lambda r, *_: (r, 0, 0)),
        ],
        out_specs=pl.BlockSpec(memory_space=pl.ANY),
        scratch_shapes=[pltpu.VMEM((2, _local_rows(n_exp), d // 2), jnp.uint32),
                        pltpu.VMEM((EXPERT_BLOCK, d // 2), jnp.uint32),
                        pltpu.SemaphoreType.DMA((2,)), pltpu.SemaphoreType.DMA],
    )
    return pl.pallas_call(
        _dispatch_kernel,
        grid_spec=grid_spec,
        out_shape=jax.ShapeDtypeStruct((p_rows, d // 2), jnp.uint32),
        compiler_params=_cparams("arbitrary"),
        name="moe_dispatch",
    )(plan["tail_start"], plan["pad_len"], plan["n_used"], plan["tile_table"], plan["tile_table"],
      v, idx, rank, plan["loc_off"])


def _expert_kernel(be_ref, nu_ref, x_ref, w1_ref, b1_ref, w2_ref, b2_ref, o_ref, w1b_ref, w2b_ref):
    j = pl.program_id(0)
    f = w2_ref.shape[0]
    prev = be_ref[jnp.maximum(j - 1, 0)]

    @pl.when(jnp.logical_and(j < nu_ref[0], jnp.logical_or(j == 0, be_ref[j] != prev)))
    def _():
        w1b_ref[...] = w1_ref[...].astype(BF16)
        w2b_ref[...] = w2_ref[...].astype(BF16)

    @pl.when(j < nu_ref[0])
    def _():
        half = x_ref.shape[1]
        x_hi, x_lo = _unpack_halves(x_ref[...])
        h = (jnp.dot(x_hi, w1b_ref[:half, :], preferred_element_type=F32)
             + jnp.dot(x_lo, w1b_ref[half:, :], preferred_element_type=F32)) + b1_ref[...]
        glu = jnp.minimum(h[:, :f], SWIGLU_LIMIT)
        lin = jnp.clip(h[:, f:], -SWIGLU_LIMIT, SWIGLU_LIMIT)
        act = glu * _sigmoid(SWIGLU_ALPHA * glu) * (lin + 1.0)
        y = jnp.dot(act.astype(BF16), w2b_ref[...], preferred_element_type=F32) + b2_ref[...]
        o_ref[...] = _pack_halves(y.astype(BF16).astype(F32))

    @pl.when(j >= nu_ref[0])
    def _():
        o_ref[...] = jnp.zeros_like(o_ref)


def _experts(xb, block_e, n_used, layer, w1, b1, w2, b2):
    depth, n_exp, d, f2 = w1.shape
    f = f2 // 2
    bm = EXPERT_BLOCK
    n_blocks = block_e.shape[0]
    last = lambda j, nu: jnp.minimum(j, nu[0] - 1)
    grid_spec = pltpu.PrefetchScalarGridSpec(
        num_scalar_prefetch=2,
        grid=(n_blocks,),
        in_specs=[
            pl.BlockSpec((bm, d // 2), lambda j, be, nu: (last(j, nu), 0)),
            pl.BlockSpec((None, None, d, f2), lambda j, be, nu: (layer, be[last(j, nu)], 0, 0)),
            pl.BlockSpec((None, None, 1, f2), lambda j, be, nu: (layer, be[last(j, nu)], 0, 0)),
            pl.BlockSpec((None, None, f, d), lambda j, be, nu: (layer, be[last(j, nu)], 0, 0)),
            pl.BlockSpec((None, None, 1, d), lambda j, be, nu: (layer, be[last(j, nu)], 0, 0)),
        ],
        out_specs=pl.BlockSpec((bm, d // 2), lambda j, be, nu: (j, 0)),
        scratch_shapes=[pltpu.VMEM((d, f2), BF16), pltpu.VMEM((f, d), BF16)],
    )
    return pl.pallas_call(
        _expert_kernel,
        grid_spec=grid_spec,
        out_shape=jax.ShapeDtypeStruct((n_blocks * bm, d // 2), jnp.uint32),
        compiler_params=_cparams("arbitrary"),
        name="moe_experts",
    )(block_e, n_used, xb, w1, b1.reshape(depth, n_exp, 1, f2), w2, b2.reshape(depth, n_exp, 1, d))


def _combine_kernel(alpha, tab_ref, next_tab_ref, idx_ref, rank_ref, off_ref, gate_ref, x1_ref, mod_ref,
                    g_ref, beta_ref, yb_ref, o_ref, rows_ref, sems):
    r = pl.program_id(0)
    tm = x1_ref.shape[0]
    rows = rows_ref.shape[1]
    slot = r % 2

    def tile_copy(t_ref, into, i):
        return pltpu.make_async_copy(
            yb_ref.at[pl.ds(pl.multiple_of(t_ref[i] * SUBLANES, SUBLANES), SUBLANES)],
            rows_ref.at[into, pl.ds(pl.multiple_of(i * SUBLANES, SUBLANES), SUBLANES)], sems.at[into])

    def gather(t_ref, into):
        _for_tiles(t_ref[TILE_TABLE - 1], lambda i: tile_copy(t_ref, into, i).start())

    @pl.when(r == 0)
    def _():
        gather(tab_ref, 0)

    @pl.when(r + 1 < pl.num_programs(0))
    def _():
        gather(next_tab_ref, 1 - slot)

    n_tiles = tab_ref[TILE_TABLE - 1]

    def many():
        return pltpu.make_async_copy(yb_ref.at[pl.ds(0, WAIT_TILES * SUBLANES)],
                                     rows_ref.at[slot, pl.ds(0, WAIT_TILES * SUBLANES)], sems.at[slot])

    _wait_tiles(n_tiles, many, lambda: tile_copy(tab_ref, slot, 0))

    pos = _local_pos(idx_ref, rank_ref, off_ref)
    lane = lax.broadcasted_iota(jnp.int32, (tm, rows), 1).astype(F32)
    gates = gate_ref[...]
    unperm = jnp.where(lane == pos[0], gates[:, 0:1], 0.0)
    for k in range(1, TOP_K):
        unperm = unperm + jnp.where(lane == pos[k], gates[:, k:k + 1], 0.0)
    row = lax.broadcasted_iota(jnp.int32, (rows, 1), 0)
    y_hi, y_lo = _unpack_halves(jnp.where(row < n_tiles * SUBLANES, rows_ref[slot], jnp.uint32(0)))
    unperm = unperm.astype(BF16)
    f = jnp.concatenate([jnp.dot(unperm, y_hi, preferred_element_type=F32),
                         jnp.dot(unperm, y_lo, preferred_element_type=F32)], axis=1)
    o_ref[...] = _layer_norm(alpha * x1_ref[...] + mod_ref[5:6, :] * f, g_ref[...], beta_ref[...])


def _combine(idx, rank, gates, x1, mod, ln_g, ln_b, yb, plan, alpha, nl, nc):
    t, d = x1.shape
    tm = ROW_BLOCK
    n_exp = plan["loc_off"].shape[2]
    row_spec = pl.BlockSpec((tm, d), lambda r: (r, 0))
    vec = pl.BlockSpec((1, d), lambda r: (0, 0))
    k_spec = pl.BlockSpec((tm, TOP_K), lambda r: (r, 0))
    return pl.pallas_call(
        functools.partial(_combine_kernel, alpha),
        grid=(t // tm,),
        in_specs=[
            pl.BlockSpec((TILE_TABLE,), lambda r: (r,), memory_space=pltpu.SMEM),
            pl.BlockSpec((TILE_TABLE,), lambda r: (jnp.minimum(r + 1, t // tm - 1),), memory_space=pltpu.SMEM),
            k_spec, k_spec,
            pl.BlockSpec((None, 1, n_exp), lambda r: (r, 0, 0)),
            k_spec,
            row_spec,
            pl.BlockSpec((None, 6, d), lambda r, f=_mod_row_map(nl, nc): (f(r), 0, 0)),
            vec, vec,
            pl.BlockSpec(memory_space=pl.ANY),
        ],
        out_specs=row_spec,
        out_shape=jax.ShapeDtypeStruct((t, d), F32),
        scratch_shapes=[pltpu.VMEM((2, _local_rows(n_exp), d // 2), jnp.uint32), pltpu.SemaphoreType.DMA((2,))],
        compiler_params=_cparams("arbitrary"),
        name="moe_combine_ln",
    )(plan["tile_table"], plan["tile_table"], idx, rank, plan["loc_off"], gates, x1, mod,
      ln_g.reshape(1, d), ln_b.reshape(1, d), yb)


def _moe_plan(cnt, n_blocks):
    nblk, n_exp = cnt.shape
    bm = EXPERT_BLOCK
    n_tiles_max = _local_rows(n_exp) // SUBLANES
    experts = jnp.arange(n_exp)
    blocks = jnp.arange(nblk)
    seg = (cnt + SUBLANES - 1) // SUBLANES * SUBLANES
    loc_off = jnp.sum(jnp.where(experts[None, None, :] < experts[None, :, None], seg[:, None, :], 0), axis=2)
    before = jnp.sum(jnp.where(blocks[None, :, None] < blocks[:, None, None], seg[None, :, :], 0), axis=1)
    rows_e = jnp.sum(seg, axis=0)
    padded = (rows_e + bm - 1) // bm * bm
    pend = jnp.sum(jnp.where(experts[None, :] <= experts[:, None], padded[None, :], 0), axis=1)
    pstart = pend - padded
    base = pstart[None, :] + before
    first_row = (jnp.arange(n_tiles_max) * SUBLANES)[None, :, None]
    e_of = jnp.sum(((loc_off + seg)[:, None, :] <= first_row).astype(jnp.int32), axis=2)
    pick = jnp.minimum(e_of, n_exp - 1)[:, :, None] == experts
    delta = jnp.sum(jnp.where(pick, (base - loc_off)[:, None, :], 0), axis=2)
    tile_of = (delta + first_row[:, :, 0]) // SUBLANES
    table = jnp.concatenate([tile_of.astype(jnp.int32),
                             jnp.zeros((nblk, TILE_TABLE - 1 - n_tiles_max), jnp.int32),
                             (jnp.sum(seg, axis=1, keepdims=True) // SUBLANES).astype(jnp.int32)], axis=1)
    block_e = jnp.sum((pend[None, :] <= (jnp.arange(n_blocks) * bm)[:, None]).astype(jnp.int32), axis=1)
    return dict(
        loc_off=loc_off.astype(F32).reshape(nblk, 1, n_exp),
        tile_table=table.reshape(-1),
        tail_start=(pstart + rows_e).astype(jnp.int32),
        pad_len=(padded - rows_e).astype(jnp.int32),
        block_e=jnp.minimum(block_e, n_exp - 1).astype(jnp.int32),
        n_used=(pend[-1:] // bm).astype(jnp.int32),
    )


def kernel(x, c, ctx, c_ctx, w_mod, b_mod, ln1_g, ln1_b, ln2_g, ln2_b, attn_w_qkv, attn_w_o, attn_lam_q1, attn_lam_k1, attn_lam_q2, attn_lam_k2, attn_subln_g, fnet_w, fnet_b, conv_w_pw1, conv_b_pw1, conv_w_dw, conv_b_dw, conv_ln_g, conv_ln_b, conv_w_pw2, conv_b_pw2, moe_w_router, moe_b_router, moe_w1, moe_b1, moe_w2, moe_b2):
    b, n, d = x.shape
    cl = ctx.shape[1]
    s = n + cl
    t = b * s
    depth = w_mod.shape[0]
    n_exp = moe_w_router.shape[2]
    nl, nc = n // ROW_BLOCK, cl // ROW_BLOCK
    alpha = (2 * depth) ** 0.25
    worst_rows = t * TOP_K + (t // ROW_BLOCK) * n_exp * (SUBLANES - 1)
    n_blocks = -(-worst_rows // EXPERT_BLOCK) + n_exp
    p_rows = n_blocks * EXPERT_BLOCK

    xs = jnp.concatenate([x, ctx], axis=1).reshape(t, d)
    cond = jnp.stack([c, jnp.broadcast_to(c_ctx, c.shape)], axis=1).reshape(2 * b, d)
    mod = _modulation(cond, w_mod, b_mod)
    cos, sin = _rope_tables(n, cl)

    for i in range(depth):
        kind, j = i % 3, i // 3
        if kind == 0:
            lam_init = 0.8 - 0.6 * float(np.exp(-0.3 * i))
            q, k, v = _qkv(xs, mod[i], attn_w_qkv[j], cos, sin, nl, nc)
            q3, k3 = q.reshape(b, s, d), k.reshape(b, s, d)
            vt = v.reshape(b, s, d // HEAD_W, HEAD_W).transpose(0, 2, 3, 1)
            vt = jnp.concatenate([vt, jnp.ones((b, d // HEAD_W, SUM_ROWS, s), BF16)], axis=2)
            lam_params = (attn_lam_q1[j], attn_lam_k1[j], attn_lam_q2[j], attn_lam_k2[j])
            o_lat = _attention(q3, k3, vt, lam_params, attn_subln_g[j], lam_init, n, cl, context=False)
            o_ctx = _attention(q3, k3, vt, lam_params, attn_subln_g[j], lam_init, n, cl, context=True)
            a = jnp.concatenate([o_lat, o_ctx], axis=1).reshape(t, d)
            w_out, b_out = attn_w_o[j], jnp.zeros((d,), F32)
        elif kind == 1:
            fa, fb = _chan_dft(xs, mod[i], nl, nc)
            fa3, fb3 = fa.reshape(b, s, d), fb.reshape(b, s, d)
            a = jnp.concatenate([_seq_dft(fa3, fb3, n, 0), _seq_dft(fa3, fb3, cl, n)], axis=1).reshape(t, d)
            w_out, b_out = fnet_w[j], fnet_b[j]
        else:
            hid = _glu(xs, mod[i], conv_w_pw1[j], conv_b_pw1[j], nl, nc)
            a = _dwconv_ln_silu(hid, conv_w_dw[j], conv_b_dw[j], conv_ln_g[j], conv_ln_b[j], nl, nc)
            w_out, b_out = conv_w_pw2[j], conv_b_pw2[j]

        x1, v, idx, gates, rank, cnt = _proj_route(
            a, w_out, b_out, xs, mod[i], ln1_g[i], ln1_b[i], moe_w_router[i], moe_b_router[i], alpha, nl, nc)
        plan = _moe_plan(cnt[:, 0, :], n_blocks)
        xb = _dispatch(v, idx, rank, plan, p_rows)
        yb = _experts(xb, plan["block_e"], plan["n_used"], i, moe_w1, moe_b1, moe_w2, moe_b2)
        xs = _combine(idx, rank, gates, x1, mod[i], ln2_g[i], ln2_b[i], yb, plan, alpha, nl, nc)

    return xs.reshape(b, s, d)[:, :n]
```

```python
import functools

import jax
import jax.numpy as jnp
import numpy as np
from jax import lax
from jax.experimental import pallas as pl
from jax.experimental.pallas import tpu as pltpu

F32 = jnp.float32
BF16 = jnp.bfloat16
HIGHEST = lax.Precision.HIGHEST

GRID_W = 64
HEAD_W = 128
MAP_W = HEAD_W // 2
ROPE_FREQS = MAP_W // 4
ROPE_BASE = 10000.0
CONV_WIDTH = 31
CONV_PAD = CONV_WIDTH // 2
TOP_K = 4
SWIGLU_ALPHA = 1.702
SWIGLU_LIMIT = 7.0
LN_EPS = 1e-5

LANES = 128
SUBLANES = 8
ROW_BLOCK = 256
SUM_ROWS = 8
HALO = 16
ATTN_TQ = 512
ATTN_TK = 512
ATTN_QC = 256
DFT_TILE = 1024
EXPERT_BLOCK = 512
TILE_TABLE = 1024
ISSUE_UNROLL = 8
WAIT_TILES = 16
VMEM_LIMIT = 56 * 1024 * 1024


def _cparams(*sem):
    return pltpu.CompilerParams(dimension_semantics=sem, vmem_limit_bytes=VMEM_LIMIT)


def _mod_row_map(nl, nc):
    rb = nl + nc

    def f(r):
        return 2 * (r // rb) + jnp.where((r % rb) >= nl, 1, 0)

    return f


def _layer_norm(z, g, b):
    mu = jnp.mean(z, axis=-1, keepdims=True)
    zc = z - mu
    var = jnp.mean(zc * zc, axis=-1, keepdims=True)
    return zc * lax.rsqrt(var + LN_EPS) * g + b


def _sigmoid(x):
    return 1.0 / (1.0 + jnp.exp(-x))


def _modulation_kernel(cond_ref, w_ref, b_ref, o_ref):
    cnd = cond_ref[...]
    s = cnd * _sigmoid(cnd)
    o_ref[...] = jnp.dot(s, w_ref[...], precision=HIGHEST, preferred_element_type=F32) + b_ref[...]


def _modulation(cond, w_mod, b_mod):
    depth, d, d6 = w_mod.shape
    rows = cond.shape[0]
    out = pl.pallas_call(
        _modulation_kernel,
        grid=(depth, d6 // d),
        in_specs=[
            pl.BlockSpec((rows, d), lambda i, j: (0, 0)),
            pl.BlockSpec((None, d, d), lambda i, j: (i, 0, j)),
            pl.BlockSpec((None, 1, d), lambda i, j: (i, 0, j)),
        ],
        out_specs=pl.BlockSpec((None, rows, d), lambda i, j: (i, 0, j)),
        out_shape=jax.ShapeDtypeStruct((depth, rows, d6), F32),
        compiler_params=_cparams("parallel", "parallel"),
        name="modulation",
    )(cond, w_mod, b_mod.reshape(depth, 1, d6))
    return out.reshape(depth, rows, d6 // d, d)


def _rope_tables(n, c):
    pos = jnp.arange(n)
    rows = (pos // GRID_W).astype(F32)
    cols = (pos % GRID_W).astype(F32)
    inv_freq = ROPE_BASE ** (-jnp.arange(ROPE_FREQS, dtype=F32) / ROPE_FREQS)
    lane = np.arange(HEAD_W)
    dim = lane % MAP_W
    freq = dim % ROPE_FREQS
    use_row = (dim // (MAP_W // 2)) == 0
    first_half = (dim % (MAP_W // 2)) < ROPE_FREQS
    ang = jnp.where(use_row[None, :], rows[:, None] * inv_freq[freq][None, :],
                    cols[:, None] * inv_freq[freq][None, :])
    cos = jnp.cos(ang)
    sin = jnp.sin(ang)
    sin = jnp.where(first_half[None, :], -sin, sin)
    cos = jnp.concatenate([cos, jnp.ones((c, HEAD_W), F32)], axis=0)
    sin = jnp.concatenate([sin, jnp.zeros((c, HEAD_W), F32)], axis=0)
    return cos, sin


def _qkv_kernel(x_ref, mod_ref, w_ref, cos_ref, sin_ref, q_ref, k_ref, v_ref):
    d = x_ref.shape[1]
    u = (x_ref[...] * (1.0 + mod_ref[1:2, :]) + mod_ref[0:1, :]).astype(BF16)
    cos = cos_ref[...]
    sin = sin_ref[...]
    lane = lax.broadcasted_iota(jnp.int32, cos.shape, 1)
    first_half = (lane % (MAP_W // 2)) < ROPE_FREQS

    def rope(t):
        partner = jnp.where(first_half, pltpu.roll(t, HEAD_W - ROPE_FREQS, 1), pltpu.roll(t, ROPE_FREQS, 1))
        return t * cos + partner * sin

    q = jnp.dot(u, w_ref[:, :d], preferred_element_type=F32)
    k = jnp.dot(u, w_ref[:, d:2 * d], preferred_element_type=F32)
    for h in range(d // HEAD_W):
        cs = slice(h * HEAD_W, (h + 1) * HEAD_W)
        q_ref[:, cs] = rope(q[:, cs]).astype(BF16)
        k_ref[:, cs] = rope(k[:, cs]).astype(BF16)
    v_ref[...] = jnp.dot(u, w_ref[:, 2 * d:], preferred_element_type=F32).astype(BF16)


def _qkv(x, mod, w_qkv, cos, sin, nl, nc):
    t, d = x.shape
    rb = nl + nc
    tm = ROW_BLOCK
    scale = jnp.concatenate([jnp.full((d,), np.log2(np.e) * MAP_W ** -0.5, F32), jnp.ones((2 * d,), F32)])
    w = (w_qkv * scale[None, :]).astype(BF16)
    row_spec = pl.BlockSpec((tm, d), lambda r: (r, 0))
    tab_spec = pl.BlockSpec((tm, HEAD_W), lambda r: (r % rb, 0))
    return pl.pallas_call(
        _qkv_kernel,
        grid=(t // tm,),
        in_specs=[
            row_spec,
            pl.BlockSpec((None, 6, d), lambda r, f=_mod_row_map(nl, nc): (f(r), 0, 0)),
            pl.BlockSpec((d, 3 * d), lambda r: (0, 0)),
            tab_spec, tab_spec,
        ],
        out_specs=[row_spec, row_spec, row_spec],
        out_shape=[jax.ShapeDtypeStruct((t, d), BF16)] * 3,
        compiler_params=_cparams("parallel"),
        name="qkv_rope",
    )(x, mod, w, cos, sin)


def _attn_kernel(lam_init, tiles, q_ref, k_ref, vt_ref, lq1_ref, lk1_ref, lq2_ref, lk2_ref, g_ref,
                 o_ref, acc1_ref, acc2_ref, sa1_ref, sa2_ref, sb1_ref, sb2_ref):
    tq = q_ref.shape[0]
    q = q_ref[...].astype(F32)
    lane = lax.broadcasted_iota(jnp.int32, q.shape, 1)
    q1 = jnp.where(lane < MAP_W, q, 0.0).astype(BF16)
    q2 = jnp.where(lane >= MAP_W, q, 0.0).astype(BF16)
    acc1_ref[...] = jnp.zeros_like(acc1_ref)
    acc2_ref[...] = jnp.zeros_like(acc2_ref)
    acc_refs = (acc1_ref, acc2_ref)
    s_refs = ((sa1_ref, sa2_ref), (sb1_ref, sb2_ref))

    qc = min(tq, ATTN_QC)
    pieces = [(mp, c) for c in range(tq // qc) for mp in range(2)]
    qms = (q1, q2)

    def score_piece(i, mp, c):
        off, size = tiles[i]
        s = lax.dot_general(k_ref[off:off + size, :], qms[mp][c * qc:(c + 1) * qc, :],
                            (((1,), (1,)), ((), ())), preferred_element_type=F32)
        s_refs[i % 2][mp][0:size, c * qc:(c + 1) * qc] = s
        return jnp.max(s, axis=0, keepdims=True)

    def acc_piece(i, mp, c, cm, m):
        off, size = tiles[i]
        cs = slice(c * qc, (c + 1) * qc)
        m_new = jnp.maximum(m, cm)
        p = jnp.exp2(s_refs[i % 2][mp][0:size, cs] - m_new).astype(BF16)
        acc_refs[mp][:, cs] = (jnp.exp2(m - m_new) * acc_refs[mp][:, cs]
                               + jnp.dot(vt_ref[:, off:off + size], p, preferred_element_type=F32))
        return m_new

    def stage(i_score, i_acc, cm_acc, ms):
        cm_new, ms_new = [], []
        for n_piece, (mp, c) in enumerate(pieces):
            if i_score is not None:
                cm_new.append(score_piece(i_score, mp, c))
            if i_acc is not None:
                ms_new.append(acc_piece(i_acc, mp, c, cm_acc[n_piece], ms[n_piece]))
        return tuple(cm_new), tuple(ms_new) if i_acc is not None else ms

    ms = tuple(jnp.full((1, qc), -jnp.inf, F32) for _ in pieces)
    cm, _ = stage(0, None, None, ms)
    for i in range(len(tiles) - 1):
        cm, ms = stage(i + 1, i, cm, ms)
    stage(None, len(tiles) - 1, cm, ms)

    lam = (jnp.exp(jnp.sum(lq1_ref[...] * lk1_ref[...], axis=1, keepdims=True))
           - jnp.exp(jnp.sum(lq2_ref[...] * lk2_ref[...], axis=1, keepdims=True)) + lam_init)
    inv1 = 1.0 / acc1_ref[HEAD_W:HEAD_W + 1, :]
    inv2 = 1.0 / acc2_ref[HEAD_W:HEAD_W + 1, :]
    o = acc1_ref[0:HEAD_W, :] * inv1 - lam * (acc2_ref[0:HEAD_W, :] * inv2)
    ms = jnp.mean(o * o, axis=0, keepdims=True)
    o = o * lax.rsqrt(ms + LN_EPS) * (g_ref[...] * (1.0 - lam_init))
    o_ref[...] = o.T.astype(BF16)


def _attention(q3, k3, vt, lam_params, subln_g, lam_init, n, c, context):
    b, s, d = q3.shape
    h = d // HEAD_W
    if context:
        tq, kv_len, kv_blk, q_off, nq = c, c, n // c, n // c, 1
    else:
        tq, kv_len, kv_blk, q_off, nq = ATTN_TQ, s, 0, 0, n // ATTN_TQ
    tk = min(ATTN_TK, kv_len)
    tiles = tuple((off, min(tk, kv_len - off)) for off in range(0, kv_len, tk))
    lam_spec = pl.BlockSpec((1, MAP_W), lambda bi, hi, qi: (0, 0))
    return pl.pallas_call(
        functools.partial(_attn_kernel, lam_init, tiles),
        grid=(b, h, nq),
        in_specs=[
            pl.BlockSpec((None, tq, HEAD_W), lambda bi, hi, qi: (bi, qi + q_off, hi)),
            pl.BlockSpec((None, kv_len, HEAD_W), lambda bi, hi, qi: (bi, kv_blk, hi)),
            pl.BlockSpec((None, None, HEAD_W + SUM_ROWS, kv_len), lambda bi, hi, qi: (bi, hi, 0, kv_blk)),
            lam_spec, lam_spec, lam_spec, lam_spec,
            pl.BlockSpec((HEAD_W, 1), lambda bi, hi, qi: (0, 0)),
        ],
        out_specs=pl.BlockSpec((None, tq, HEAD_W), lambda bi, hi, qi: (bi, qi, hi)),
        out_shape=jax.ShapeDtypeStruct((b, tq * nq, d), BF16),
        scratch_shapes=[pltpu.VMEM((HEAD_W + SUM_ROWS, tq), F32)] * 2 + [pltpu.VMEM((tk, tq), F32)] * 4,
        compiler_params=_cparams("parallel", "parallel", "parallel"),
        name="attn_ctx" if context else "attn_lat",
    )(q3, k3, vt, *[p.reshape(1, MAP_W) for p in lam_params], subln_g.reshape(HEAD_W, 1))


def _dft_matrices(n, scale):
    blk = min(ROW_BLOCK, n)
    k = jnp.arange(n, dtype=jnp.int32)

    def cs(j):
        r = (j[:, None] * k[None, :]) % n
        ang = r.astype(F32) * (2.0 * np.pi / n)
        return jnp.cos(ang), jnp.sin(ang)

    c0, s0 = cs(jnp.arange(blk, dtype=jnp.int32))
    cj, sj = cs(jnp.arange(0, n, blk, dtype=jnp.int32))
    cm = cj[:, None, :] * c0[None] - sj[:, None, :] * s0[None]
    sm = sj[:, None, :] * c0[None] + cj[:, None, :] * s0[None]
    return (cm * scale).reshape(n, n).astype(BF16), (sm * scale).reshape(n, n).astype(BF16)


def _chan_dft_kernel(x_ref, mod_ref, cs_ref, a_ref, b_ref):
    d = x_ref.shape[1]
    u = (x_ref[...] * (1.0 + mod_ref[1:2, :]) + mod_ref[0:1, :]).astype(BF16)
    for g in range(d // LANES):
        sl = slice(g * LANES, (g + 1) * LANES)
        ab = jnp.dot(u[:, sl], cs_ref[...], preferred_element_type=F32)
        a_ref[:, sl] = ab[:, :LANES].astype(BF16)
        b_ref[:, sl] = ab[:, LANES:].astype(BF16)


def _chan_dft(x, mod, nl, nc):
    t, d = x.shape
    tm = ROW_BLOCK
    j = np.arange(LANES)
    ang = 2.0 * np.pi * ((j[:, None] * j[None, :]) % LANES) / LANES
    cs = jnp.asarray(np.concatenate([np.cos(ang), np.sin(ang)], axis=1) / np.sqrt(LANES), BF16)
    row_spec = pl.BlockSpec((tm, d), lambda r: (r, 0))
    return pl.pallas_call(
        _chan_dft_kernel,
        grid=(t // tm,),
        in_specs=[
            row_spec,
            pl.BlockSpec((None, 6, d), lambda r, f=_mod_row_map(nl, nc): (f(r), 0, 0)),
            pl.BlockSpec((LANES, 2 * LANES), lambda r: (0, 0)),
        ],
        out_specs=[row_spec, row_spec],
        out_shape=[jax.ShapeDtypeStruct((t, d), BF16)] * 2,
        compiler_params=_cparams("parallel"),
        name="chan_dft",
    )(x, mod, cs)


def _seq_dft_kernel(c_ref, s_ref, a_ref, b_ref, o_ref, acc_ref):
    kk = pl.program_id(2)

    @pl.when(kk == 0)
    def _():
        acc_ref[...] = jnp.zeros_like(acc_ref)

    acc_ref[...] += (jnp.dot(c_ref[...], a_ref[...], preferred_element_type=F32)
                     - jnp.dot(s_ref[...], b_ref[...], preferred_element_type=F32))

    @pl.when(kk == pl.num_programs(2) - 1)
    def _():
        o_ref[...] = acc_ref[...].astype(o_ref.dtype)


def _seq_dft(a3, b3, length, row_off):
    b, s, d = a3.shape
    tile = min(DFT_TILE, length)
    cm, sm = _dft_matrices(length, length ** -0.5)
    nt = length // tile
    off = row_off // tile
    mat_spec = pl.BlockSpec((tile, tile), lambda bi, m, k: (m, k))
    in_spec = pl.BlockSpec((None, tile, d), lambda bi, m, k: (bi, k + off, 0))
    return pl.pallas_call(
        _seq_dft_kernel,
        grid=(b, nt, nt),
        in_specs=[mat_spec, mat_spec, in_spec, in_spec],
        out_specs=pl.BlockSpec((None, tile, d), lambda bi, m, k: (bi, m, 0)),
        out_shape=jax.ShapeDtypeStruct((b, length, d), BF16),
        scratch_shapes=[pltpu.VMEM((tile, d), F32)],
        compiler_params=_cparams("parallel", "parallel", "arbitrary"),
        name="seq_dft",
    )(cm, sm, a3, b3)


def _glu_kernel(x_ref, mod_ref, w_ref, b_ref, o_ref):
    d = x_ref.shape[1]
    u = (x_ref[...] * (1.0 + mod_ref[1:2, :]) + mod_ref[0:1, :]).astype(BF16)
    a = jnp.dot(u, w_ref[:, :d], preferred_element_type=F32) + b_ref[:, :d]
    g = jnp.dot(u, w_ref[:, d:], preferred_element_type=F32) + b_ref[:, d:]
    o_ref[...] = a * _sigmoid(g)


def _glu(x, mod, w, bias, nl, nc):
    t, d = x.shape
    tm = ROW_BLOCK
    row_spec = pl.BlockSpec((tm, d), lambda r: (r, 0))
    return pl.pallas_call(
        _glu_kernel,
        grid=(t // tm,),
        in_specs=[
            row_spec,
            pl.BlockSpec((None, 6, d), lambda r, f=_mod_row_map(nl, nc): (f(r), 0, 0)),
            pl.BlockSpec((d, 2 * d), lambda r: (0, 0)),
            pl.BlockSpec((1, 2 * d), lambda r: (0, 0)),
        ],
        out_specs=row_spec,
        out_shape=jax.ShapeDtypeStruct((t, d), F32),
        compiler_params=_cparams("parallel"),
        name="pw1_glu",
    )(x, mod, w.astype(BF16), bias.reshape(1, 2 * d))


def _dwconv_kernel(nl, nc, prev_ref, cur_ref, next_ref, w_ref, b_ref, g_ref, beta_ref, o_ref, win_ref, h_ref):
    tm, d = cur_ref.shape
    rr = pl.program_id(0) % (nl + nc)
    has_prev = jnp.logical_and(rr != 0, rr != nl)
    has_next = jnp.logical_and(rr != nl - 1, rr != nl + nc - 1)
    win_ref[0, 0:HALO, :] = jnp.where(has_prev, prev_ref[...], 0.0)
    win_ref[0, HALO:HALO + tm, :] = cur_ref[...]
    win_ref[0, HALO + tm:, :] = jnp.where(has_next, next_ref[...], 0.0)
    span = tm + 2 * HALO - SUBLANES
    for s in range(1, SUBLANES):
        win_ref[s, 0:span, :] = win_ref[0, s:s + span, :]
    rows = 64
    for c in range(d // LANES):
        cs = slice(c * LANES, (c + 1) * LANES)
        wc = w_ref[:, cs]
        for r0 in range(0, tm, rows):
            acc = jnp.broadcast_to(b_ref[:, cs], (rows, LANES))
            for j in range(CONV_WIDTH):
                off = HALO - CONV_PAD + j
                start = r0 + off - off % SUBLANES
                acc = acc + wc[j:j + 1, :] * win_ref[off % SUBLANES, start:start + rows, cs]
            h_ref[r0:r0 + rows, cs] = acc
    y = _layer_norm(h_ref[...], g_ref[...], beta_ref[...])
    o_ref[...] = (y * _sigmoid(y)).astype(BF16)


def _dwconv_ln_silu(hid, w_dw, b_dw, ln_g, ln_b, nl, nc):
    t, d = hid.shape
    tm = ROW_BLOCK
    per = tm // HALO
    n_halo = t // HALO
    vec = pl.BlockSpec((1, d), lambda r: (0, 0))
    return pl.pallas_call(
        functools.partial(_dwconv_kernel, nl, nc),
        grid=(t // tm,),
        in_specs=[
            pl.BlockSpec((HALO, d), lambda r: (jnp.maximum(r * per - 1, 0), 0)),
            pl.BlockSpec((tm, d), lambda r: (r, 0)),
            pl.BlockSpec((HALO, d), lambda r: (jnp.minimum((r + 1) * per, n_halo - 1), 0)),
            pl.BlockSpec((CONV_WIDTH, d), lambda r: (0, 0)),
            vec, vec, vec,
        ],
        out_specs=pl.BlockSpec((tm, d), lambda r: (r, 0)),
        out_shape=jax.ShapeDtypeStruct((t, d), BF16),
        scratch_shapes=[pltpu.VMEM((SUBLANES, tm + 2 * HALO, d), F32), pltpu.VMEM((tm, d), F32)],
        compiler_params=_cparams("parallel"),
        name="dwconv_ln_silu",
    )(hid, hid, hid, w_dw, b_dw.reshape(1, d), ln_g.reshape(1, d), ln_b.reshape(1, d))


def _pack_cols(cols, dtype):
    rows = cols[0].shape[0]
    lane = lax.broadcasted_iota(jnp.int32, (rows, len(cols)), 1)
    out = jnp.zeros((rows, len(cols)), dtype)
    for i, col in enumerate(cols):
        out = jnp.where(lane == i, col.astype(dtype), out)
    return out


def _proj_route_kernel(alpha, a_ref, w_ref, bias_ref, x_ref, mod_ref, g_ref, beta_ref, wr_ref, br_ref,
                       x1_ref, v_ref, idx_ref, gate_ref, rank_ref, cnt_ref):
    tm = x_ref.shape[0]
    n_exp = wr_ref.shape[2]

    y = jnp.dot(a_ref[...], w_ref[...], preferred_element_type=F32) + bias_ref[...]
    x1 = _layer_norm(alpha * x_ref[...] + mod_ref[2:3, :] * y, g_ref[...], beta_ref[...])
    x1_ref[...] = x1
    v = x1 * (1.0 + mod_ref[4:5, :]) + mod_ref[3:4, :]
    v_ref[...] = v

    v_hi = v.astype(BF16)
    v_lo = (v - v_hi.astype(F32)).astype(BF16)
    logits = (jnp.dot(v_hi, wr_ref[0], preferred_element_type=F32)
              + jnp.dot(v_lo, wr_ref[0], preferred_element_type=F32)
              + jnp.dot(v_hi, wr_ref[1], preferred_element_type=F32)) + br_ref[...]
    col = lax.broadcasted_iota(jnp.int32, logits.shape, 1).astype(F32)
    work = logits
    vals, idxs, sels = [], [], []
    for _ in range(TOP_K):
        mx = jnp.max(work, axis=1, keepdims=True)
        first = jnp.min(jnp.where(work == mx, col, float(n_exp)), axis=1, keepdims=True)
        sel = col == first
        work = jnp.where(sel, -jnp.inf, work)
        vals.append(mx)
        idxs.append(first)
        sels.append(sel)
    es = [jnp.exp(val - vals[0]) for val in vals]
    inv = 1.0 / (es[0] + es[1] + es[2] + es[3])

    onehot = jnp.zeros(logits.shape, F32)
    for sel in sels:
        onehot = onehot + jnp.where(sel, 1.0, 0.0)
    ri = lax.broadcasted_iota(jnp.int32, (tm, tm), 0)
    ci = lax.broadcasted_iota(jnp.int32, (tm, tm), 1)
    tri = jnp.where(ci < ri, 1.0, 0.0).astype(BF16)
    before = jnp.dot(tri, onehot.astype(BF16), preferred_element_type=F32)
    ranks = [jnp.sum(jnp.where(sel, before, 0.0), axis=1, keepdims=True) for sel in sels]

    idx_ref[...] = _pack_cols(idxs, jnp.int32)
    gate_ref[...] = _pack_cols([e * inv for e in es], F32)
    rank_ref[...] = _pack_cols(ranks, jnp.int32)
    cnt_ref[...] = jnp.sum(onehot, axis=0, keepdims=True).astype(jnp.int32)


def _proj_route(a, w, bias, x, mod, ln_g, ln_b, w_router, b_router, alpha, nl, nc):
    t, d = x.shape
    n_exp = w_router.shape[1]
    tm = ROW_BLOCK
    row_spec = pl.BlockSpec((tm, d), lambda r: (r, 0))
    vec = pl.BlockSpec((1, d), lambda r: (0, 0))
    k_spec = pl.BlockSpec((tm, TOP_K), lambda r: (r, 0))
    wr_hi = w_router.astype(BF16)
    return pl.pallas_call(
        functools.partial(_proj_route_kernel, alpha),
        grid=(t // tm,),
        in_specs=[
            row_spec,
            pl.BlockSpec((d, d), lambda r: (0, 0)),
            vec,
            row_spec,
            pl.BlockSpec((None, 6, d), lambda r, f=_mod_row_map(nl, nc): (f(r), 0, 0)),
            vec, vec,
            pl.BlockSpec((2, d, n_exp), lambda r: (0, 0, 0)),
            pl.BlockSpec((1, n_exp), lambda r: (0, 0)),
        ],
        out_specs=[row_spec, row_spec, k_spec, k_spec, k_spec,
                   pl.BlockSpec((None, 1, n_exp), lambda r: (r, 0, 0))],
        out_shape=[
            jax.ShapeDtypeStruct((t, d), F32),
            jax.ShapeDtypeStruct((t, d), F32),
            jax.ShapeDtypeStruct((t, TOP_K), jnp.int32),
            jax.ShapeDtypeStruct((t, TOP_K), F32),
            jax.ShapeDtypeStruct((t, TOP_K), jnp.int32),
            jax.ShapeDtypeStruct((t // tm, 1, n_exp), jnp.int32),
        ],
        compiler_params=_cparams("parallel"),
        name="proj_ln_route",
    )(a, w.astype(BF16), bias.reshape(1, d), x, mod, ln_g.reshape(1, d), ln_b.reshape(1, d),
      jnp.stack([wr_hi, (w_router - wr_hi.astype(F32)).astype(BF16)]), b_router.reshape(1, n_exp))


def _local_rows(n_exp):
    return TOP_K * ROW_BLOCK + n_exp * SUBLANES


def _for_tiles(n_tiles, body):
    def trip(g, carry):
        for u in range(ISSUE_UNROLL):
            body(g * ISSUE_UNROLL + u)
        return carry

    def single(i, carry):
        body(i)
        return carry

    full = n_tiles // ISSUE_UNROLL
    lax.fori_loop(0, full, trip, 0)
    lax.fori_loop(full * ISSUE_UNROLL, n_tiles, single, 0)


def _wait_tiles(n_tiles, many, one):
    def wait_many(g, carry):
        many().wait()
        return carry

    def wait_one(i, carry):
        one().wait()
        return carry

    lax.fori_loop(0, n_tiles // WAIT_TILES, wait_many, 0)
    lax.fori_loop(0, n_tiles % WAIT_TILES, wait_one, 0)


def _pack_halves(x):
    half = x.shape[1] // 2
    hi = lax.bitcast_convert_type(x[:, :half], jnp.uint32)
    lo = lax.bitcast_convert_type(x[:, half:], jnp.uint32)
    return (hi & jnp.uint32(0xFFFF0000)) | (lo >> 16)


def _unpack_halves(w):
    hi = lax.bitcast_convert_type(w & jnp.uint32(0xFFFF0000), F32)
    lo = lax.bitcast_convert_type(w << 16, F32)
    return hi.astype(BF16), lo.astype(BF16)


def _local_pos(idx_ref, rank_ref, off_ref):
    tm = idx_ref.shape[0]
    col = lax.broadcasted_iota(jnp.int32, (tm, off_ref.shape[1]), 1)
    idx = idx_ref[...]
    rank = rank_ref[...]
    pos = []
    for k in range(TOP_K):
        start = jnp.sum(jnp.where(col == idx[:, k:k + 1], off_ref[...], 0.0), axis=1, keepdims=True)
        pos.append(start + rank[:, k:k + 1].astype(F32))
    return pos


def _dispatch_kernel(tail_ref, pad_ref, nu_ref, tab_ref, prev_tab_ref, v_ref, idx_ref, rank_ref, off_ref,
                     xb_ref, srt_ref, zero_ref, sems, fill_sem):
    r = pl.program_id(0)
    slot = r % 2
    tm = v_ref.shape[0]
    rows = srt_ref.shape[1]
    bm = zero_ref.shape[0]
    n_blocks = xb_ref.shape[0] // bm

    @pl.when(pl.program_id(0) == 0)
    def _():
        zero_ref[...] = jnp.zeros_like(zero_ref)

        def padding(do):
            for e in range(tail_ref.shape[0]):
                pad = pad_ref[e]
                for bit in reversed(range(SUBLANES.bit_length() - 1, bm.bit_length() - 1)):
                    size = 1 << bit
                    off = pl.multiple_of(tail_ref[e] + ((pad >> (bit + 1)) << (bit + 1)), SUBLANES)
                    copy = pltpu.make_async_copy(zero_ref.at[pl.ds(0, size)], xb_ref.at[pl.ds(off, size)], fill_sem)
                    pl.when((pad & size) != 0)(functools.partial(do, copy))

        def blocks(do):
            def body(j, carry):
                do(pltpu.make_async_copy(zero_ref, xb_ref.at[pl.ds(pl.multiple_of(j * bm, bm), bm)], fill_sem))
                return carry
            lax.fori_loop(nu_ref[0], n_blocks, body, 0)

        padding(lambda copy: copy.start())
        blocks(lambda copy: copy.start())
        padding(lambda copy: copy.wait())
        blocks(lambda copy: copy.wait())

    pos = _local_pos(idx_ref, rank_ref, off_ref)
    lane = lax.broadcasted_iota(jnp.int32, (tm, LANES), 1)
    pos_cols = jnp.zeros((tm, LANES), F32)
    for k in range(TOP_K):
        pos_cols = jnp.where(lane == k, pos[k], pos_cols)
    pos_rows = pos_cols.T
    row = lax.broadcasted_iota(jnp.int32, (rows, tm), 0).astype(F32)
    hit = row == pos_rows[0:1, :]
    for k in range(1, TOP_K):
        hit = jnp.logical_or(hit, row == pos_rows[k:k + 1, :])
    perm = jnp.where(hit, 1.0, 0.0).astype(BF16)
    srt_ref[slot] = _pack_halves(jnp.dot(perm, v_ref[...].astype(BF16), preferred_element_type=F32))

    def tile_copy(t_ref, buf, i):
        return pltpu.make_async_copy(
            srt_ref.at[buf, pl.ds(pl.multiple_of(i * SUBLANES, SUBLANES), SUBLANES)],
            xb_ref.at[pl.ds(pl.multiple_of(t_ref[i] * SUBLANES, SUBLANES), SUBLANES)], sems.at[buf])

    def drain(t_ref, buf):
        def many():
            return pltpu.make_async_copy(srt_ref.at[buf, pl.ds(0, WAIT_TILES * SUBLANES)],
                                         xb_ref.at[pl.ds(0, WAIT_TILES * SUBLANES)], sems.at[buf])
        _wait_tiles(t_ref[TILE_TABLE - 1], many, lambda: tile_copy(t_ref, buf, 0))

    _for_tiles(tab_ref[TILE_TABLE - 1], lambda i: tile_copy(tab_ref, slot, i).start())

    @pl.when(r > 0)
    def _():
        drain(prev_tab_ref, 1 - slot)

    @pl.when(r == pl.num_programs(0) - 1)
    def _():
        drain(tab_ref, slot)


def _dispatch(v, idx, rank, plan, p_rows):
    t, d = v.shape
    tm = ROW_BLOCK
    n_exp = plan["loc_off"].shape[2]
    k_spec = pl.BlockSpec((tm, TOP_K), lambda r, *_: (r, 0))
    grid_spec = pltpu.PrefetchScalarGridSpec(
        num_scalar_prefetch=3,
        grid=(t // tm,),
        in_specs=[
            pl.BlockSpec((TILE_TABLE,), lambda r, *_: (r,), memory_space=pltpu.SMEM),
            pl.BlockSpec((TILE_TABLE,), lambda r, *_: (jnp.maximum(r - 1, 0),), memory_space=pltpu.SMEM),
            pl.BlockSpec((tm, d), lambda r, *_: (r, 0)),
            k_spec, k_spec,
            pl.BlockSpec((None, 1, n_exp), lambda r, *_: (r, 0, 0)),
        ],
        out_specs=pl.BlockSpec(memory_space=pl.ANY),
        scratch_shapes=[pltpu.VMEM((2, _local_rows(n_exp), d // 2), jnp.uint32),
                        pltpu.VMEM((EXPERT_BLOCK, d // 2), jnp.uint32),
                        pltpu.SemaphoreType.DMA((2,)), pltpu.SemaphoreType.DMA],
    )
    return pl.pallas_call(
        _dispatch_kernel,
        grid_spec=grid_spec,
        out_shape=jax.ShapeDtypeStruct((p_rows, d // 2), jnp.uint32),
        compiler_params=_cparams("arbitrary"),
        name="moe_dispatch",
    )(plan["tail_start"], plan["pad_len"], plan["n_used"], plan["tile_table"], plan["tile_table"],
      v, idx, rank, plan["loc_off"])


def _expert_kernel(layer, be_ref, ne_ref, nu_ref, x_ref, w1_hbm, b1_ref, w2_hbm, b2_ref, o_ref,
                   w1s_ref, w2s_ref, w1b_ref, w2b_ref, sems):
    j = pl.program_id(0)
    f = w2b_ref.shape[0]
    prev = be_ref[jnp.maximum(j - 1, 0)]

    def fetch(e):
        return (pltpu.make_async_copy(w1_hbm.at[layer, e], w1s_ref, sems.at[0]),
                pltpu.make_async_copy(w2_hbm.at[layer, e], w2s_ref, sems.at[1]))

    @pl.when(j == 0)
    def _():
        for copy in fetch(be_ref[0]):
            copy.start()

    @pl.when(jnp.logical_and(j < nu_ref[0], jnp.logical_or(j == 0, be_ref[j] != prev)))
    def _():
        for copy in fetch(be_ref[j]):
            copy.wait()
        w1b_ref[...] = w1s_ref[...].astype(BF16)
        w2b_ref[...] = w2s_ref[...].astype(BF16)

        @pl.when(ne_ref[j] >= 0)
        def _():
            for copy in fetch(ne_ref[j]):
                copy.start()

    @pl.when(j < nu_ref[0])
    def _():
        half = x_ref.shape[1]
        x_hi, x_lo = _unpack_halves(x_ref[...])
        h = (jnp.dot(x_hi, w1b_ref[:half, :], preferred_element_type=F32)
             + jnp.dot(x_lo, w1b_ref[half:, :], preferred_element_type=F32)) + b1_ref[...]
        glu = jnp.minimum(h[:, :f], SWIGLU_LIMIT)
        lin = jnp.clip(h[:, f:], -SWIGLU_LIMIT, SWIGLU_LIMIT)
        act = glu * _sigmoid(SWIGLU_ALPHA * glu) * (lin + 1.0)
        y = jnp.dot(act.astype(BF16), w2b_ref[...], preferred_element_type=F32) + b2_ref[...]
        o_ref[...] = _pack_halves(y.astype(BF16).astype(F32))

    @pl.when(j >= nu_ref[0])
    def _():
        o_ref[...] = jnp.zeros_like(o_ref)


def _experts(xb, block_e, next_e, n_used, layer, w1, b1, w2, b2):
    depth, n_exp, d, f2 = w1.shape
    f = f2 // 2
    bm = EXPERT_BLOCK
    n_blocks = block_e.shape[0]
    last = lambda j, nu: jnp.minimum(j, nu[0] - 1)
    grid_spec = pltpu.PrefetchScalarGridSpec(
        num_scalar_prefetch=3,
        grid=(n_blocks,),
        in_specs=[
            pl.BlockSpec((bm, d // 2), lambda j, be, ne, nu: (last(j, nu), 0)),
            pl.BlockSpec(memory_space=pl.ANY),
            pl.BlockSpec((None, None, 1, f2), lambda j, be, ne, nu: (layer, be[last(j, nu)], 0, 0)),
            pl.BlockSpec(memory_space=pl.ANY),
            pl.BlockSpec((None, None, 1, d), lambda j, be, ne, nu: (layer, be[last(j, nu)], 0, 0)),
        ],
        out_specs=pl.BlockSpec((bm, d // 2), lambda j, be, ne, nu: (j, 0)),
        scratch_shapes=[pltpu.VMEM((d, f2), F32), pltpu.VMEM((f, d), F32),
                        pltpu.VMEM((d, f2), BF16), pltpu.VMEM((f, d), BF16), pltpu.SemaphoreType.DMA((2,))],
    )
    return pl.pallas_call(
        functools.partial(_expert_kernel, layer),
        grid_spec=grid_spec,
        out_shape=jax.ShapeDtypeStruct((n_blocks * bm, d // 2), jnp.uint32),
        compiler_params=_cparams("arbitrary"),
        name="moe_experts",
    )(block_e, next_e, n_used, xb, w1, b1.reshape(depth, n_exp, 1, f2), w2, b2.reshape(depth, n_exp, 1, d))


def _combine_kernel(alpha, tab_ref, next_tab_ref, idx_ref, rank_ref, off_ref, gate_ref, x1_ref, mod_ref,
                    g_ref, beta_ref, yb_ref, o_ref, rows_ref, sems):
    r = pl.program_id(0)
    tm = x1_ref.shape[0]
    rows = rows_ref.shape[1]
    slot = r % 2

    def tile_copy(t_ref, into, i):
        return pltpu.make_async_copy(
            yb_ref.at[pl.ds(pl.multiple_of(t_ref[i] * SUBLANES, SUBLANES), SUBLANES)],
            rows_ref.at[into, pl.ds(pl.multiple_of(i * SUBLANES, SUBLANES), SUBLANES)], sems.at[into])

    def gather(t_ref, into):
        _for_tiles(t_ref[TILE_TABLE - 1], lambda i: tile_copy(t_ref, into, i).start())

    @pl.when(r == 0)
    def _():
        gather(tab_ref, 0)

    @pl.when(r + 1 < pl.num_programs(0))
    def _():
        gather(next_tab_ref, 1 - slot)

    n_tiles = tab_ref[TILE_TABLE - 1]

    def many():
        return pltpu.make_async_copy(yb_ref.at[pl.ds(0, WAIT_TILES * SUBLANES)],
                                     rows_ref.at[slot, pl.ds(0, WAIT_TILES * SUBLANES)], sems.at[slot])

    _wait_tiles(n_tiles, many, lambda: tile_copy(tab_ref, slot, 0))

    pos = _local_pos(idx_ref, rank_ref, off_ref)
    lane = lax.broadcasted_iota(jnp.int32, (tm, rows), 1).astype(F32)
    gates = gate_ref[...]
    unperm = jnp.where(lane == pos[0], gates[:, 0:1], 0.0)
    for k in range(1, TOP_K):
        unperm = unperm + jnp.where(lane == pos[k], gates[:, k:k + 1], 0.0)
    row = lax.broadcasted_iota(jnp.int32, (rows, 1), 0)
    y_hi, y_lo = _unpack_halves(jnp.where(row < n_tiles * SUBLANES, rows_ref[slot], jnp.uint32(0)))
    unperm = unperm.astype(BF16)
    f = jnp.concatenate([jnp.dot(unperm, y_hi, preferred_element_type=F32),
                         jnp.dot(unperm, y_lo, preferred_element_type=F32)], axis=1)
    o_ref[...] = _layer_norm(alpha * x1_ref[...] + mod_ref[5:6, :] * f, g_ref[...], beta_ref[...])


def _combine(idx, rank, gates, x1, mod, ln_g, ln_b, yb, plan, alpha, nl, nc):
    t, d = x1.shape
    tm = ROW_BLOCK
    n_exp = plan["loc_off"].shape[2]
    row_spec = pl.BlockSpec((tm, d), lambda r: (r, 0))
    vec = pl.BlockSpec((1, d), lambda r: (0, 0))
    k_spec = pl.BlockSpec((tm, TOP_K), lambda r: (r, 0))
    return pl.pallas_call(
        functools.partial(_combine_kernel, alpha),
        grid=(t // tm,),
        in_specs=[
            pl.BlockSpec((TILE_TABLE,), lambda r: (r,), memory_space=pltpu.SMEM),
            pl.BlockSpec((TILE_TABLE,), lambda r: (jnp.minimum(r + 1, t // tm - 1),), memory_space=pltpu.SMEM),
            k_spec, k_spec,
            pl.BlockSpec((None, 1, n_exp), lambda r: (r, 0, 0)),
            k_spec,
            row_spec,
            pl.BlockSpec((None, 6, d), lambda r, f=_mod_row_map(nl, nc): (f(r), 0, 0)),
            vec, vec,
            pl.BlockSpec(memory_space=pl.ANY),
        ],
        out_specs=row_spec,
        out_shape=jax.ShapeDtypeStruct((t, d), F32),
        scratch_shapes=[pltpu.VMEM((2, _local_rows(n_exp), d // 2), jnp.uint32), pltpu.SemaphoreType.DMA((2,))],
        compiler_params=_cparams("arbitrary"),
        name="moe_combine_ln",
    )(plan["tile_table"], plan["tile_table"], idx, rank, plan["loc_off"], gates, x1, mod,
      ln_g.reshape(1, d), ln_b.reshape(1, d), yb)


def _moe_plan(cnt, n_blocks):
    nblk, n_exp = cnt.shape
    bm = EXPERT_BLOCK
    n_tiles_max = _local_rows(n_exp) // SUBLANES
    experts = jnp.arange(n_exp)
    blocks = jnp.arange(nblk)
    seg = (cnt + SUBLANES - 1) // SUBLANES * SUBLANES
    loc_off = jnp.sum(jnp.where(experts[None, None, :] < experts[None, :, None], seg[:, None, :], 0), axis=2)
    before = jnp.sum(jnp.where(blocks[None, :, None] < blocks[:, None, None], seg[None, :, :], 0), axis=1)
    rows_e = jnp.sum(seg, axis=0)
    padded = (rows_e + bm - 1) // bm * bm
    pend = jnp.sum(jnp.where(experts[None, :] <= experts[:, None], padded[None, :], 0), axis=1)
    pstart = pend - padded
    base = pstart[None, :] + before
    first_row = (jnp.arange(n_tiles_max) * SUBLANES)[None, :, None]
    e_of = jnp.sum(((loc_off + seg)[:, None, :] <= first_row).astype(jnp.int32), axis=2)
    pick = jnp.minimum(e_of, n_exp - 1)[:, :, None] == experts
    delta = jnp.sum(jnp.where(pick, (base - loc_off)[:, None, :], 0), axis=2)
    tile_of = (delta + first_row[:, :, 0]) // SUBLANES
    table = jnp.concatenate([tile_of.astype(jnp.int32),
                             jnp.zeros((nblk, TILE_TABLE - 1 - n_tiles_max), jnp.int32),
                             (jnp.sum(seg, axis=1, keepdims=True) // SUBLANES).astype(jnp.int32)], axis=1)
    block_e = jnp.sum((pend[None, :] <= (jnp.arange(n_blocks) * bm)[:, None]).astype(jnp.int32), axis=1)
    block_e = jnp.minimum(block_e, n_exp - 1)
    later = jnp.logical_and(experts[None, :] > experts[:, None], (padded > 0)[None, :])
    next_of = jnp.min(jnp.where(later, experts[None, :], n_exp), axis=1)
    next_e = jnp.sum(jnp.where(block_e[:, None] == experts, next_of[None, :], 0), axis=1)
    next_e = jnp.where(next_e >= n_exp, -1, next_e)
    return dict(
        next_e=next_e.astype(jnp.int32),
        loc_off=loc_off.astype(F32).reshape(nblk, 1, n_exp),
        tile_table=table.reshape(-1),
        tail_start=(pstart + rows_e).astype(jnp.int32),
        pad_len=(padded - rows_e).astype(jnp.int32),
        block_e=block_e.astype(jnp.int32),
        n_used=(pend[-1:] // bm).astype(jnp.int32),
    )


def kernel(x, c, ctx, c_ctx, w_mod, b_mod, ln1_g, ln1_b, ln2_g, ln2_b, attn_w_qkv, attn_w_o, attn_lam_q1, attn_lam_k1, attn_lam_q2, attn_lam_k2, attn_subln_g, fnet_w, fnet_b, conv_w_pw1, conv_b_pw1, conv_w_dw, conv_b_dw, conv_ln_g, conv_ln_b, conv_w_pw2, conv_b_pw2, moe_w_router, moe_b_router, moe_w1, moe_b1, moe_w2, moe_b2):
    b, n, d = x.shape
    cl = ctx.shape[1]
    s = n + cl
    t = b * s
    depth = w_mod.shape[0]
    n_exp = moe_w_router.shape[2]
    nl, nc = n // ROW_BLOCK, cl // ROW_BLOCK
    alpha = (2 * depth) ** 0.25
    worst_rows = t * TOP_K + (t // ROW_BLOCK) * n_exp * (SUBLANES - 1)
    n_blocks = -(-worst_rows // EXPERT_BLOCK) + n_exp
    p_rows = n_blocks * EXPERT_BLOCK

    xs = jnp.concatenate([x, ctx], axis=1).reshape(t, d)
    cond = jnp.stack([c, jnp.broadcast_to(c_ctx, c.shape)], axis=1).reshape(2 * b, d)
    mod = _modulation(cond, w_mod, b_mod)
    cos, sin = _rope_tables(n, cl)

    for i in range(depth):
        kind, j = i % 3, i // 3
        if kind == 0:
            lam_init = 0.8 - 0.6 * float(np.exp(-0.3 * i))
            q, k, v = _qkv(xs, mod[i], attn_w_qkv[j], cos, sin, nl, nc)
            q3, k3 = q.reshape(b, s, d), k.reshape(b, s, d)
            vt = v.reshape(b, s, d // HEAD_W, HEAD_W).transpose(0, 2, 3, 1)
            vt = jnp.concatenate([vt, jnp.ones((b, d // HEAD_W, SUM_ROWS, s), BF16)], axis=2)
            lam_params = (attn_lam_q1[j], attn_lam_k1[j], attn_lam_q2[j], attn_lam_k2[j])
            o_lat = _attention(q3, k3, vt, lam_params, attn_subln_g[j], lam_init, n, cl, context=False)
            o_ctx = _attention(q3, k3, vt, lam_params, attn_subln_g[j], lam_init, n, cl, context=True)
            a = jnp.concatenate([o_lat, o_ctx], axis=1).reshape(t, d)
            w_out, b_out = attn_w_o[j], jnp.zeros((d,), F32)
        elif kind == 1:
            fa, fb = _chan_dft(xs, mod[i], nl, nc)
            fa3, fb3 = fa.reshape(b, s, d), fb.reshape(b, s, d)
            a = jnp.concatenate([_seq_dft(fa3, fb3, n, 0), _seq_dft(fa3, fb3, cl, n)], axis=1).reshape(t, d)
            w_out, b_out = fnet_w[j], fnet_b[j]
        else:
            hid = _glu(xs, mod[i], conv_w_pw1[j], conv_b_pw1[j], nl, nc)
            a = _dwconv_ln_silu(hid, conv_w_dw[j], conv_b_dw[j], conv_ln_g[j], conv_ln_b[j], nl, nc)
            w_out, b_out = conv_w_pw2[j], conv_b_pw2[j]

        x1, v, idx, gates, rank, cnt = _proj_route(
            a, w_out, b_out, xs, mod[i], ln1_g[i], ln1_b[i], moe_w_router[i], moe_b_router[i], alpha, nl, nc)
        plan = _moe_plan(cnt[:, 0, :], n_blocks)
        xb = _dispatch(v, idx, rank, plan, p_rows)
        yb = _experts(xb, plan["block_e"], plan["next_e"], plan["n_used"], i, moe_w1, moe_b1, moe_w2, moe_b2)
        xs = _combine(idx, rank, gates, x1, mod[i], ln2_g[i], ln2_b[i], yb, plan, alpha, nl, nc)

    return xs.reshape(b, s, d)[:, :n]
```

```python
import functools

import jax
import jax.numpy as jnp
import numpy as np
from jax import lax
from jax.experimental import pallas as pl
from jax.experimental.pallas import tpu as pltpu

F32 = jnp.float32
BF16 = jnp.bfloat16
HIGHEST = lax.Precision.HIGHEST

GRID_W = 64
HEAD_W = 128
MAP_W = HEAD_W // 2
ROPE_FREQS = MAP_W // 4
ROPE_BASE = 10000.0
CONV_WIDTH = 31
CONV_PAD = CONV_WIDTH // 2
TOP_K = 4
SWIGLU_ALPHA = 1.702
SWIGLU_LIMIT = 7.0
LN_EPS = 1e-5

LANES = 128
SUBLANES = 8
ROW_BLOCK = 256
SUM_ROWS = 8
HALO = 16
ATTN_TQ = 512
ATTN_TK = 512
ATTN_QC = 256
DFT_TILE = 1024
EXPERT_BLOCK = 512
TILE_TABLE = 1024
ISSUE_UNROLL = 8
WAIT_TILES = 16
VMEM_LIMIT = 56 * 1024 * 1024


def _cparams(*sem):
    return pltpu.CompilerParams(dimension_semantics=sem, vmem_limit_bytes=VMEM_LIMIT)


def _mod_row_map(nl, nc):
    rb = nl + nc

    def f(r):
        return 2 * (r // rb) + jnp.where((r % rb) >= nl, 1, 0)

    return f


def _layer_norm(z, g, b):
    mu = jnp.mean(z, axis=-1, keepdims=True)
    zc = z - mu
    var = jnp.mean(zc * zc, axis=-1, keepdims=True)
    return zc * lax.rsqrt(var + LN_EPS) * g + b


def _sigmoid(x):
    return 1.0 / (1.0 + jnp.exp(-x))


def _modulation_kernel(cond_ref, w_ref, b_ref, o_ref):
    cnd = cond_ref[...]
    s = cnd * _sigmoid(cnd)
    o_ref[...] = jnp.dot(s, w_ref[...], precision=HIGHEST, preferred_element_type=F32) + b_ref[...]


def _modulation(cond, w_mod, b_mod):
    depth, d, d6 = w_mod.shape
    rows = cond.shape[0]
    out = pl.pallas_call(
        _modulation_kernel,
        grid=(depth, d6 // d),
        in_specs=[
            pl.BlockSpec((rows, d), lambda i, j: (0, 0)),
            pl.BlockSpec((None, d, d), lambda i, j: (i, 0, j)),
            pl.BlockSpec((None, 1, d), lambda i, j: (i, 0, j)),
        ],
        out_specs=pl.BlockSpec((None, rows, d), lambda i, j: (i, 0, j)),
        out_shape=jax.ShapeDtypeStruct((depth, rows, d6), F32),
        compiler_params=_cparams("parallel", "parallel"),
        name="modulation",
    )(cond, w_mod, b_mod.reshape(depth, 1, d6))
    return out.reshape(depth, rows, d6 // d, d)


def _rope_tables(n, c):
    pos = jnp.arange(n)
    rows = (pos // GRID_W).astype(F32)
    cols = (pos % GRID_W).astype(F32)
    inv_freq = ROPE_BASE ** (-jnp.arange(ROPE_FREQS, dtype=F32) / ROPE_FREQS)
    lane = np.arange(HEAD_W)
    dim = lane % MAP_W
    freq = dim % ROPE_FREQS
    use_row = (dim // (MAP_W // 2)) == 0
    first_half = (dim % (MAP_W // 2)) < ROPE_FREQS
    ang = jnp.where(use_row[None, :], rows[:, None] * inv_freq[freq][None, :],
                    cols[:, None] * inv_freq[freq][None, :])
    cos = jnp.cos(ang)
    sin = jnp.sin(ang)
    sin = jnp.where(first_half[None, :], -sin, sin)
    cos = jnp.concatenate([cos, jnp.ones((c, HEAD_W), F32)], axis=0)
    sin = jnp.concatenate([sin, jnp.zeros((c, HEAD_W), F32)], axis=0)
    return cos, sin


def _qkv_kernel(x_ref, mod_ref, w_ref, cos_ref, sin_ref, q_ref, k_ref, v_ref):
    d = x_ref.shape[1]
    u = (x_ref[...] * (1.0 + mod_ref[1:2, :]) + mod_ref[0:1, :]).astype(BF16)
    cos = cos_ref[...]
    sin = sin_ref[...]
    lane = lax.broadcasted_iota(jnp.int32, cos.shape, 1)
    first_half = (lane % (MAP_W // 2)) < ROPE_FREQS

    def rope(t):
        partner = jnp.where(first_half, pltpu.roll(t, HEAD_W - ROPE_FREQS, 1), pltpu.roll(t, ROPE_FREQS, 1))
        return t * cos + partner * sin

    q = jnp.dot(u, w_ref[:, :d], preferred_element_type=F32)
    k = jnp.dot(u, w_ref[:, d:2 * d], preferred_element_type=F32)
    for h in range(d // HEAD_W):
        cs = slice(h * HEAD_W, (h + 1) * HEAD_W)
        q_ref[:, cs] = rope(q[:, cs]).astype(BF16)
        k_ref[:, cs] = rope(k[:, cs]).astype(BF16)
    v_ref[...] = jnp.dot(u, w_ref[:, 2 * d:], preferred_element_type=F32).astype(BF16)


def _qkv(x, mod, w_qkv, cos, sin, nl, nc):
    t, d = x.shape
    rb = nl + nc
    tm = ROW_BLOCK
    scale = jnp.concatenate([jnp.full((d,), np.log2(np.e) * MAP_W ** -0.5, F32), jnp.ones((2 * d,), F32)])
    w = (w_qkv * scale[None, :]).astype(BF16)
    row_spec = pl.BlockSpec((tm, d), lambda r: (r, 0))
    tab_spec = pl.BlockSpec((tm, HEAD_W), lambda r: (r % rb, 0))
    return pl.pallas_call(
        _qkv_kernel,
        grid=(t // tm,),
        in_specs=[
            row_spec,
            pl.BlockSpec((None, 6, d), lambda r, f=_mod_row_map(nl, nc): (f(r), 0, 0)),
            pl.BlockSpec((d, 3 * d), lambda r: (0, 0)),
            tab_spec, tab_spec,
        ],
        out_specs=[row_spec, row_spec, row_spec],
        out_shape=[jax.ShapeDtypeStruct((t, d), BF16)] * 3,
        compiler_params=_cparams("parallel"),
        name="qkv_rope",
    )(x, mod, w, cos, sin)


def _attn_kernel(lam_init, tiles, q_ref, k_ref, vt_ref, lq1_ref, lk1_ref, lq2_ref, lk2_ref, g_ref,
                 o_ref, acc1_ref, acc2_ref, sa1_ref, sa2_ref, sb1_ref, sb2_ref):
    tq = q_ref.shape[0]
    q = q_ref[...].astype(F32)
    lane = lax.broadcasted_iota(jnp.int32, q.shape, 1)
    q1 = jnp.where(lane < MAP_W, q, 0.0).astype(BF16)
    q2 = jnp.where(lane >= MAP_W, q, 0.0).astype(BF16)
    acc1_ref[...] = jnp.zeros_like(acc1_ref)
    acc2_ref[...] = jnp.zeros_like(acc2_ref)
    acc_refs = (acc1_ref, acc2_ref)
    s_refs = ((sa1_ref, sa2_ref), (sb1_ref, sb2_ref))

    qc = min(tq, ATTN_QC)
    pieces = [(mp, c) for c in range(tq // qc) for mp in range(2)]
    qms = (q1, q2)

    def score_piece(i, mp, c):
        off, size = tiles[i]
        s = lax.dot_general(k_ref[off:off + size, :], qms[mp][c * qc:(c + 1) * qc, :],
                            (((1,), (1,)), ((), ())), preferred_element_type=F32)
        s_refs[i % 2][mp][0:size, c * qc:(c + 1) * qc] = s
        return jnp.max(s, axis=0, keepdims=True)

    def acc_piece(i, mp, c, cm, m):
        off, size = tiles[i]
        cs = slice(c * qc, (c + 1) * qc)
        m_new = jnp.maximum(m, cm)
        p = jnp.exp2(s_refs[i % 2][mp][0:size, cs] - m_new).astype(BF16)
        acc_refs[mp][:, cs] = (jnp.exp2(m - m_new) * acc_refs[mp][:, cs]
                               + jnp.dot(vt_ref[:, off:off + size], p, preferred_element_type=F32))
        return m_new

    def stage(i_score, i_acc, cm_acc, ms):
        cm_new, ms_new = [], []
        for n_piece, (mp, c) in enumerate(pieces):
            if i_score is not None:
                cm_new.append(score_piece(i_score, mp, c))
            if i_acc is not None:
                ms_new.append(acc_piece(i_acc, mp, c, cm_acc[n_piece], ms[n_piece]))
        return tuple(cm_new), tuple(ms_new) if i_acc is not None else ms

    ms = tuple(jnp.full((1, qc), -jnp.inf, F32) for _ in pieces)
    cm, _ = stage(0, None, None, ms)
    for i in range(len(tiles) - 1):
        cm, ms = stage(i + 1, i, cm, ms)
    stage(None, len(tiles) - 1, cm, ms)

    lam = (jnp.exp(jnp.sum(lq1_ref[...] * lk1_ref[...], axis=1, keepdims=True))
           - jnp.exp(jnp.sum(lq2_ref[...] * lk2_ref[...], axis=1, keepdims=True)) + lam_init)
    inv1 = 1.0 / acc1_ref[HEAD_W:HEAD_W + 1, :]
    inv2 = 1.0 / acc2_ref[HEAD_W:HEAD_W + 1, :]
    o = acc1_ref[0:HEAD_W, :] * inv1 - lam * (acc2_ref[0:HEAD_W, :] * inv2)
    ms = jnp.mean(o * o, axis=0, keepdims=True)
    o = o * lax.rsqrt(ms + LN_EPS) * (g_ref[...] * (1.0 - lam_init))
    o_ref[...] = o.T.astype(BF16)


def _attention(q3, k3, vt, lam_params, subln_g, lam_init, n, c, context):
    b, s, d = q3.shape
    h = d // HEAD_W
    if context:
        tq, kv_len, kv_blk, q_off, nq = c, c, n // c, n // c, 1
    else:
        tq, kv_len, kv_blk, q_off, nq = ATTN_TQ, s, 0, 0, n // ATTN_TQ
    tk = min(ATTN_TK, kv_len)
    tiles = tuple((off, min(tk, kv_len - off)) for off in range(0, kv_len, tk))
    lam_spec = pl.BlockSpec((1, MAP_W), lambda bi, hi, qi: (0, 0))
    return pl.pallas_call(
        functools.partial(_attn_kernel, lam_init, tiles),
        grid=(b, h, nq),
        in_specs=[
            pl.BlockSpec((None, tq, HEAD_W), lambda bi, hi, qi: (bi, qi + q_off, hi)),
            pl.BlockSpec((None, kv_len, HEAD_W), lambda bi, hi, qi: (bi, kv_blk, hi)),
            pl.BlockSpec((None, None, HEAD_W + SUM_ROWS, kv_len), lambda bi, hi, qi: (bi, hi, 0, kv_blk)),
            lam_spec, lam_spec, lam_spec, lam_spec,
            pl.BlockSpec((HEAD_W, 1), lambda bi, hi, qi: (0, 0)),
        ],
        out_specs=pl.BlockSpec((None, tq, HEAD_W), lambda bi, hi, qi: (bi, qi, hi)),
        out_shape=jax.ShapeDtypeStruct((b, tq * nq, d), BF16),
        scratch_shapes=[pltpu.VMEM((HEAD_W + SUM_ROWS, tq), F32)] * 2 + [pltpu.VMEM((tk, tq), F32)] * 4,
        compiler_params=_cparams("parallel", "parallel", "parallel"),
        name="attn_ctx" if context else "attn_lat",
    )(q3, k3, vt, *[p.reshape(1, MAP_W) for p in lam_params], subln_g.reshape(HEAD_W, 1))


def _dft_matrices(n, scale):
    blk = min(ROW_BLOCK, n)
    k = jnp.arange(n, dtype=jnp.int32)

    def cs(j):
        r = (j[:, None] * k[None, :]) % n
        ang = r.astype(F32) * (2.0 * np.pi / n)
        return jnp.cos(ang), jnp.sin(ang)

    c0, s0 = cs(jnp.arange(blk, dtype=jnp.int32))
    cj, sj = cs(jnp.arange(0, n, blk, dtype=jnp.int32))
    cm = cj[:, None, :] * c0[None] - sj[:, None, :] * s0[None]
    sm = sj[:, None, :] * c0[None] + cj[:, None, :] * s0[None]
    return (cm * scale).reshape(n, n).astype(BF16), (sm * scale).reshape(n, n).astype(BF16)


def _chan_dft_kernel(x_ref, mod_ref, cs_ref, a_ref, b_ref):
    d = x_ref.shape[1]
    u = (x_ref[...] * (1.0 + mod_ref[1:2, :]) + mod_ref[0:1, :]).astype(BF16)
    for g in range(d // LANES):
        sl = slice(g * LANES, (g + 1) * LANES)
        ab = jnp.dot(u[:, sl], cs_ref[...], preferred_element_type=F32)
        a_ref[:, sl] = ab[:, :LANES].astype(BF16)
        b_ref[:, sl] = ab[:, LANES:].astype(BF16)


def _chan_dft(x, mod, nl, nc):
    t, d = x.shape
    tm = ROW_BLOCK
    j = np.arange(LANES)
    ang = 2.0 * np.pi * ((j[:, None] * j[None, :]) % LANES) / LANES
    cs = jnp.asarray(np.concatenate([np.cos(ang), np.sin(ang)], axis=1) / np.sqrt(LANES), BF16)
    row_spec = pl.BlockSpec((tm, d), lambda r: (r, 0))
    return pl.pallas_call(
        _chan_dft_kernel,
        grid=(t // tm,),
        in_specs=[
            row_spec,
            pl.BlockSpec((None, 6, d), lambda r, f=_mod_row_map(nl, nc): (f(r), 0, 0)),
            pl.BlockSpec((LANES, 2 * LANES), lambda r: (0, 0)),
        ],
        out_specs=[row_spec, row_spec],
        out_shape=[jax.ShapeDtypeStruct((t, d), BF16)] * 2,
        compiler_params=_cparams("parallel"),
        name="chan_dft",
    )(x, mod, cs)


def _seq_dft_kernel(c_ref, s_ref, a_ref, b_ref, o_ref, acc_ref):
    kk = pl.program_id(2)

    @pl.when(kk == 0)
    def _():
        acc_ref[...] = jnp.zeros_like(acc_ref)

    acc_ref[...] += (jnp.dot(c_ref[...], a_ref[...], preferred_element_type=F32)
                     - jnp.dot(s_ref[...], b_ref[...], preferred_element_type=F32))

    @pl.when(kk == pl.num_programs(2) - 1)
    def _():
        o_ref[...] = acc_ref[...].astype(o_ref.dtype)


def _seq_dft(a3, b3, length, row_off):
    b, s, d = a3.shape
    tile = min(DFT_TILE, length)
    cm, sm = _dft_matrices(length, length ** -0.5)
    nt = length // tile
    off = row_off // tile
    mat_spec = pl.BlockSpec((tile, tile), lambda bi, m, k: (m, k))
    in_spec = pl.BlockSpec((None, tile, d), lambda bi, m, k: (bi, k + off, 0))
    return pl.pallas_call(
        _seq_dft_kernel,
        grid=(b, nt, nt),
        in_specs=[mat_spec, mat_spec, in_spec, in_spec],
        out_specs=pl.BlockSpec((None, tile, d), lambda bi, m, k: (bi, m, 0)),
        out_shape=jax.ShapeDtypeStruct((b, length, d), BF16),
        scratch_shapes=[pltpu.VMEM((tile, d), F32)],
        compiler_params=_cparams("parallel", "parallel", "arbitrary"),
        name="seq_dft",
    )(cm, sm, a3, b3)


def _glu_kernel(x_ref, mod_ref, w_ref, b_ref, o_ref):
    d = x_ref.shape[1]
    u = (x_ref[...] * (1.0 + mod_ref[1:2, :]) + mod_ref[0:1, :]).astype(BF16)
    a = jnp.dot(u, w_ref[:, :d], preferred_element_type=F32) + b_ref[:, :d]
    g = jnp.dot(u, w_ref[:, d:], preferred_element_type=F32) + b_ref[:, d:]
    o_ref[...] = a * _sigmoid(g)


def _glu(x, mod, w, bias, nl, nc):
    t, d = x.shape
    tm = ROW_BLOCK
    row_spec = pl.BlockSpec((tm, d), lambda r: (r, 0))
    return pl.pallas_call(
        _glu_kernel,
        grid=(t // tm,),
        in_specs=[
            row_spec,
            pl.BlockSpec((None, 6, d), lambda r, f=_mod_row_map(nl, nc): (f(r), 0, 0)),
            pl.BlockSpec((d, 2 * d), lambda r: (0, 0)),
            pl.BlockSpec((1, 2 * d), lambda r: (0, 0)),
        ],
        out_specs=row_spec,
        out_shape=jax.ShapeDtypeStruct((t, d), F32),
        compiler_params=_cparams("parallel"),
        name="pw1_glu",
    )(x, mod, w.astype(BF16), bias.reshape(1, 2 * d))


def _dwconv_kernel(nl, nc, prev_ref, cur_ref, next_ref, w_ref, b_ref, g_ref, beta_ref, o_ref, win_ref, h_ref):
    tm, d = cur_ref.shape
    rr = pl.program_id(0) % (nl + nc)
    has_prev = jnp.logical_and(rr != 0, rr != nl)
    has_next = jnp.logical_and(rr != nl - 1, rr != nl + nc - 1)
    win_ref[0, 0:HALO, :] = jnp.where(has_prev, prev_ref[...], 0.0)
    win_ref[0, HALO:HALO + tm, :] = cur_ref[...]
    win_ref[0, HALO + tm:, :] = jnp.where(has_next, next_ref[...], 0.0)
    span = tm + 2 * HALO - SUBLANES
    for s in range(1, SUBLANES):
        win_ref[s, 0:span, :] = win_ref[0, s:s + span, :]
    rows = 64
    for c in range(d // LANES):
        cs = slice(c * LANES, (c + 1) * LANES)
        wc = w_ref[:, cs]
        for r0 in range(0, tm, rows):
            acc = jnp.broadcast_to(b_ref[:, cs], (rows, LANES))
            for j in range(CONV_WIDTH):
                off = HALO - CONV_PAD + j
                start = r0 + off - off % SUBLANES
                acc = acc + wc[j:j + 1, :] * win_ref[off % SUBLANES, start:start + rows, cs]
            h_ref[r0:r0 + rows, cs] = acc
    y = _layer_norm(h_ref[...], g_ref[...], beta_ref[...])
    o_ref[...] = (y * _sigmoid(y)).astype(BF16)


def _dwconv_ln_silu(hid, w_dw, b_dw, ln_g, ln_b, nl, nc):
    t, d = hid.shape
    tm = ROW_BLOCK
    per = tm // HALO
    n_halo = t // HALO
    vec = pl.BlockSpec((1, d), lambda r: (0, 0))
    return pl.pallas_call(
        functools.partial(_dwconv_kernel, nl, nc),
        grid=(t // tm,),
        in_specs=[
            pl.BlockSpec((HALO, d), lambda r: (jnp.maximum(r * per - 1, 0), 0)),
            pl.BlockSpec((tm, d), lambda r: (r, 0)),
            pl.BlockSpec((HALO, d), lambda r: (jnp.minimum((r + 1) * per, n_halo - 1), 0)),
            pl.BlockSpec((CONV_WIDTH, d), lambda r: (0, 0)),
            vec, vec, vec,
        ],
        out_specs=pl.BlockSpec((tm, d), lambda r: (r, 0)),
        out_shape=jax.ShapeDtypeStruct((t, d), BF16),
        scratch_shapes=[pltpu.VMEM((SUBLANES, tm + 2 * HALO, d), F32), pltpu.VMEM((tm, d), F32)],
        compiler_params=_cparams("parallel"),
        name="dwconv_ln_silu",
    )(hid, hid, hid, w_dw, b_dw.reshape(1, d), ln_g.reshape(1, d), ln_b.reshape(1, d))


def _pack_cols(cols, dtype):
    rows = cols[0].shape[0]
    lane = lax.broadcasted_iota(jnp.int32, (rows, len(cols)), 1)
    out = jnp.zeros((rows, len(cols)), dtype)
    for i, col in enumerate(cols):
        out = jnp.where(lane == i, col.astype(dtype), out)
    return out


def _proj_route_kernel(alpha, a_ref, w_ref, bias_ref, x_ref, mod_ref, g_ref, beta_ref, wr_ref, br_ref,
                       x1_ref, v_ref, idx_ref, gate_ref, rank_ref, cnt_ref):
    tm = x_ref.shape[0]
    n_exp = wr_ref.shape[2]

    y = jnp.dot(a_ref[...], w_ref[...], preferred_element_type=F32) + bias_ref[...]
    x1 = _layer_norm(alpha * x_ref[...] + mod_ref[2:3, :] * y, g_ref[...], beta_ref[...])
    x1_ref[...] = x1
    v = x1 * (1.0 + mod_ref[4:5, :]) + mod_ref[3:4, :]
    v_ref[...] = v

    v_hi = v.astype(BF16)
    v_lo = (v - v_hi.astype(F32)).astype(BF16)
    logits = (jnp.dot(v_hi, wr_ref[0], preferred_element_type=F32)
              + jnp.dot(v_lo, wr_ref[0], preferred_element_type=F32)
              + jnp.dot(v_hi, wr_ref[1], preferred_element_type=F32)) + br_ref[...]
    col = lax.broadcasted_iota(jnp.int32, logits.shape, 1).astype(F32)
    work = logits
    vals, idxs, sels = [], [], []
    for _ in range(TOP_K):
        mx = jnp.max(work, axis=1, keepdims=True)
        first = jnp.min(jnp.where(work == mx, col, float(n_exp)), axis=1, keepdims=True)
        sel = col == first
        work = jnp.where(sel, -jnp.inf, work)
        vals.append(mx)
        idxs.append(first)
        sels.append(sel)
    es = [jnp.exp(val - vals[0]) for val in vals]
    inv = 1.0 / (es[0] + es[1] + es[2] + es[3])

    onehot = jnp.zeros(logits.shape, F32)
    for sel in sels:
        onehot = onehot + jnp.where(sel, 1.0, 0.0)
    ri = lax.broadcasted_iota(jnp.int32, (tm, tm), 0)
    ci = lax.broadcasted_iota(jnp.int32, (tm, tm), 1)
    tri = jnp.where(ci < ri, 1.0, 0.0).astype(BF16)
    before = jnp.dot(tri, onehot.astype(BF16), preferred_element_type=F32)
    ranks = [jnp.sum(jnp.where(sel, before, 0.0), axis=1, keepdims=True) for sel in sels]

    idx_ref[...] = _pack_cols(idxs, jnp.int32)
    gate_ref[...] = _pack_cols([e * inv for e in es], F32)
    rank_ref[...] = _pack_cols(ranks, jnp.int32)
    cnt_ref[...] = jnp.sum(onehot, axis=0, keepdims=True).astype(jnp.int32)


def _proj_route(a, w, bias, x, mod, ln_g, ln_b, w_router, b_router, alpha, nl, nc):
    t, d = x.shape
    n_exp = w_router.shape[1]
    tm = ROW_BLOCK
    row_spec = pl.BlockSpec((tm, d), lambda r: (r, 0))
    vec = pl.BlockSpec((1, d), lambda r: (0, 0))
    k_spec = pl.BlockSpec((tm, TOP_K), lambda r: (r, 0))
    wr_hi = w_router.astype(BF16)
    return pl.pallas_call(
        functools.partial(_proj_route_kernel, alpha),
        grid=(t // tm,),
        in_specs=[
            row_spec,
            pl.BlockSpec((d, d), lambda r: (0, 0)),
            vec,
            row_spec,
            pl.BlockSpec((None, 6, d), lambda r, f=_mod_row_map(nl, nc): (f(r), 0, 0)),
            vec, vec,
            pl.BlockSpec((2, d, n_exp), lambda r: (0, 0, 0)),
            pl.BlockSpec((1, n_exp), lambda r: (0, 0)),
        ],
        out_specs=[row_spec, row_spec, k_spec, k_spec, k_spec,
                   pl.BlockSpec((None, 1, n_exp), lambda r: (r, 0, 0))],
        out_shape=[
            jax.ShapeDtypeStruct((t, d), F32),
            jax.ShapeDtypeStruct((t, d), F32),
            jax.ShapeDtypeStruct((t, TOP_K), jnp.int32),
            jax.ShapeDtypeStruct((t, TOP_K), F32),
            jax.ShapeDtypeStruct((t, TOP_K), jnp.int32),
            jax.ShapeDtypeStruct((t // tm, 1, n_exp), jnp.int32),
        ],
        compiler_params=_cparams("parallel"),
        name="proj_ln_route",
    )(a, w.astype(BF16), bias.reshape(1, d), x, mod, ln_g.reshape(1, d), ln_b.reshape(1, d),
      jnp.stack([wr_hi, (w_router - wr_hi.astype(F32)).astype(BF16)]), b_router.reshape(1, n_exp))


def _local_rows(n_exp):
    return TOP_K * ROW_BLOCK + n_exp * SUBLANES


def _for_tiles(n_tiles, body):
    def trip(g, carry):
        for u in range(ISSUE_UNROLL):
            body(g * ISSUE_UNROLL + u)
        return carry

    def single(i, carry):
        body(i)
        return carry

    full = n_tiles // ISSUE_UNROLL
    lax.fori_loop(0, full, trip, 0)
    lax.fori_loop(full * ISSUE_UNROLL, n_tiles, single, 0)


def _wait_tiles(n_tiles, many, one):
    def wait_many(g, carry):
        many().wait()
        return carry

    def wait_one(i, carry):
        one().wait()
        return carry

    lax.fori_loop(0, n_tiles // WAIT_TILES, wait_many, 0)
    lax.fori_loop(0, n_tiles % WAIT_TILES, wait_one, 0)


def _pack_halves(x):
    half = x.shape[1] // 2
    hi = lax.bitcast_convert_type(x[:, :half], jnp.uint32)
    lo = lax.bitcast_convert_type(x[:, half:], jnp.uint32)
    return (hi & jnp.uint32(0xFFFF0000)) | (lo >> 16)


def _unpack_halves(w):
    hi = lax.bitcast_convert_type(w & jnp.uint32(0xFFFF0000), F32)
    lo = lax.bitcast_convert_type(w << 16, F32)
    return hi.astype(BF16), lo.astype(BF16)


def _local_pos(idx_ref, rank_ref, off_ref):
    tm = idx_ref.shape[0]
    col = lax.broadcasted_iota(jnp.int32, (tm, off_ref.shape[1]), 1)
    idx = idx_ref[...]
    rank = rank_ref[...]
    pos = []
    for k in range(TOP_K):
        start = jnp.sum(jnp.where(col == idx[:, k:k + 1], off_ref[...], 0.0), axis=1, keepdims=True)
        pos.append(start + rank[:, k:k + 1].astype(F32))
    return pos


def _dispatch_kernel(tail_ref, pad_ref, nu_ref, tab_ref, prev_tab_ref, v_ref, idx_ref, rank_ref, off_ref,
                     xb_ref, srt_ref, zero_ref, sems, fill_sem):
    r = pl.program_id(0)
    slot = r % 2
    tm = v_ref.shape[0]
    rows = srt_ref.shape[1]
    bm = zero_ref.shape[0]
    n_blocks = xb_ref.shape[0] // bm

    @pl.when(pl.program_id(0) == 0)
    def _():
        zero_ref[...] = jnp.zeros_like(zero_ref)

        def padding(do):
            for e in range(tail_ref.shape[0]):
                pad = pad_ref[e]
                for bit in reversed(range(SUBLANES.bit_length() - 1, bm.bit_length() - 1)):
                    size = 1 << bit
                    off = pl.multiple_of(tail_ref[e] + ((pad >> (bit + 1)) << (bit + 1)), SUBLANES)
                    copy = pltpu.make_async_copy(zero_ref.at[pl.ds(0, size)], xb_ref.at[pl.ds(off, size)], fill_sem)
                    pl.when((pad & size) != 0)(functools.partial(do, copy))

        def blocks(do):
            def body(j, carry):
                do(pltpu.make_async_copy(zero_ref, xb_ref.at[pl.ds(pl.multiple_of(j * bm, bm), bm)], fill_sem))
                return carry
            lax.fori_loop(nu_ref[0], n_blocks, body, 0)

        padding(lambda copy: copy.start())
        blocks(lambda copy: copy.start())
        padding(lambda copy: copy.wait())
        blocks(lambda copy: copy.wait())

    pos = _local_pos(idx_ref, rank_ref, off_ref)
    lane = lax.broadcasted_iota(jnp.int32, (tm, LANES), 1)
    pos_cols = jnp.zeros((tm, LANES), F32)
    for k in range(TOP_K):
        pos_cols = jnp.where(lane == k, pos[k], pos_cols)
    pos_rows = pos_cols.T
    row = lax.broadcasted_iota(jnp.int32, (rows, tm), 0).astype(F32)
    hit = row == pos_rows[0:1, :]
    for k in range(1, TOP_K):
        hit = jnp.logical_or(hit, row == pos_rows[k:k + 1, :])
    perm = jnp.where(hit, 1.0, 0.0).astype(BF16)
    srt_ref[slot] = _pack_halves(jnp.dot(perm, v_ref[...].astype(BF16), preferred_element_type=F32))

    def tile_copy(t_ref, buf, i):
        return pltpu.make_async_copy(
            srt_ref.at[buf, pl.ds(pl.multiple_of(i * SUBLANES, SUBLANES), SUBLANES)],
            xb_ref.at[pl.ds(pl.multiple_of(t_ref[i] * SUBLANES, SUBLANES), SUBLANES)], sems.at[buf])

    def drain(t_ref, buf):
        def many():
            return pltpu.make_async_copy(srt_ref.at[buf, pl.ds(0, WAIT_TILES * SUBLANES)],
                                         xb_ref.at[pl.ds(0, WAIT_TILES * SUBLANES)], sems.at[buf])
        _wait_tiles(t_ref[TILE_TABLE - 1], many, lambda: tile_copy(t_ref, buf, 0))

    _for_tiles(tab_ref[TILE_TABLE - 1], lambda i: tile_copy(tab_ref, slot, i).start())

    @pl.when(r > 0)
    def _():
        drain(prev_tab_ref, 1 - slot)

    @pl.when(r == pl.num_programs(0) - 1)
    def _():
        drain(tab_ref, slot)


def _dispatch(v, idx, rank, plan, p_rows):
    t, d = v.shape
    tm = ROW_BLOCK
    n_exp = plan["loc_off"].shape[2]
    k_spec = pl.BlockSpec((tm, TOP_K), lambda r, *_: (r, 0))
    grid_spec = pltpu.PrefetchScalarGridSpec(
        num_scalar_prefetch=3,
        grid=(t // tm,),
        in_specs=[
            pl.BlockSpec((TILE_TABLE,), lambda r, *_: (r,), memory_space=pltpu.SMEM),
            pl.BlockSpec((TILE_TABLE,), lambda r, *_: (jnp.maximum(r - 1, 0),), memory_space=pltpu.SMEM),
            pl.BlockSpec((tm, d), lambda r, *_: (r, 0)),
            k_spec, k_spec,
            pl.BlockSpec((None, 1, n_exp), lambda r, *_: (r, 0, 0)),
        ],
        out_specs=pl.BlockSpec(memory_space=pl.ANY),
        scratch_shapes=[pltpu.VMEM((2, _local_rows(n_exp), d // 2), jnp.uint32),
                        pltpu.VMEM((EXPERT_BLOCK, d // 2), jnp.uint32),
                        pltpu.SemaphoreType.DMA((2,)), pltpu.SemaphoreType.DMA],
    )
    return pl.pallas_call(
        _dispatch_kernel,
        grid_spec=grid_spec,
        out_shape=jax.ShapeDtypeStruct((p_rows, d // 2), jnp.uint32),
        compiler_params=_cparams("arbitrary"),
        name="moe_dispatch",
    )(plan["tail_start"], plan["pad_len"], plan["n_used"], plan["tile_table"], plan["tile_table"],
      v, idx, rank, plan["loc_off"])


def _expert_kernel(layer, be_ref, ne_ref, nu_ref, x_ref, w1_hbm, b1_ref, w2_hbm, b2_ref, o_ref,
                   w1s_ref, w2s_ref, w1b_ref, w2b_ref, sems):
    j = pl.program_id(0)
    f = w2b_ref.shape[0]
    prev = be_ref[jnp.maximum(j - 1, 0)]

    def fetch(e):
        return (pltpu.make_async_copy(w1_hbm.at[layer, e], w1s_ref, sems.at[0]),
                pltpu.make_async_copy(w2_hbm.at[layer, e], w2s_ref, sems.at[1]))

    @pl.when(j == 0)
    def _():
        for copy in fetch(be_ref[0]):
            copy.start()

    @pl.when(jnp.logical_and(j < nu_ref[0], jnp.logical_or(j == 0, be_ref[j] != prev)))
    def _():
        for copy in fetch(be_ref[j]):
            copy.wait()
        w1b_ref[...] = w1s_ref[...].astype(BF16)
        w2b_ref[...] = w2s_ref[...].astype(BF16)

        @pl.when(ne_ref[j] >= 0)
        def _():
            for copy in fetch(ne_ref[j]):
                copy.start()

    @pl.when(j < nu_ref[0])
    def _():
        half = x_ref.shape[1]
        x_hi, x_lo = _unpack_halves(x_ref[...])
        h = (jnp.dot(x_hi, w1b_ref[:half, :], preferred_element_type=F32)
             + jnp.dot(x_lo, w1b_ref[half:, :], preferred_element_type=F32)) + b1_ref[...]
        glu = jnp.minimum(h[:, :f], SWIGLU_LIMIT)
        lin = jnp.clip(h[:, f:], -SWIGLU_LIMIT, SWIGLU_LIMIT)
        act = glu * _sigmoid(SWIGLU_ALPHA * glu) * (lin + 1.0)
        y = jnp.dot(act.astype(BF16), w2b_ref[...], preferred_element_type=F32) + b2_ref[...]
        o_ref[...] = _pack_halves(y.astype(BF16).astype(F32))

    @pl.when(j >= nu_ref[0])
    def _():
        o_ref[...] = jnp.zeros_like(o_ref)


def _experts(xb, block_e, next_e, n_used, layer, w1, b1, w2, b2):
    depth, n_exp, d, f2 = w1.shape
    f = f2 // 2
    bm = EXPERT_BLOCK
    n_blocks = block_e.shape[0]
    last = lambda j, nu: jnp.minimum(j, nu[0] - 1)
    grid_spec = pltpu.PrefetchScalarGridSpec(
        num_scalar_prefetch=3,
        grid=(n_blocks,),
        in_specs=[
            pl.BlockSpec((bm, d // 2), lambda j, be, ne, nu: (last(j, nu), 0)),
            pl.BlockSpec(memory_space=pl.ANY),
            pl.BlockSpec((None, None, 1, f2), lambda j, be, ne, nu: (layer, be[last(j, nu)], 0, 0)),
            pl.BlockSpec(memory_space=pl.ANY),
            pl.BlockSpec((None, None, 1, d), lambda j, be, ne, nu: (layer, be[last(j, nu)], 0, 0)),
        ],
        out_specs=pl.BlockSpec((bm, d // 2), lambda j, be, ne, nu: (j, 0)),
        scratch_shapes=[pltpu.VMEM((d, f2), F32), pltpu.VMEM((f, d), F32),
                        pltpu.VMEM((d, f2), BF16), pltpu.VMEM((f, d), BF16), pltpu.SemaphoreType.DMA((2,))],
    )
    return pl.pallas_call(
        functools.partial(_expert_kernel, layer),
        grid_spec=grid_spec,
        out_shape=jax.ShapeDtypeStruct((n_blocks * bm, d // 2), jnp.uint32),
        compiler_params=_cparams("arbitrary"),
        name="moe_experts",
    )(block_e, next_e, n_used, xb, w1, b1.reshape(depth, n_exp, 1, f2), w2, b2.reshape(depth, n_exp, 1, d))


def _combine_kernel(alpha, tab_ref, next_tab_ref, idx_ref, rank_ref, off_ref, gate_ref, x1_ref, mod_ref,
                    g_ref, beta_ref, yb_ref, o_ref, rows_ref, sems):
    r = pl.program_id(0)
    tm = x1_ref.shape[0]
    rows = rows_ref.shape[1]
    slot = r % 2

    def tile_copy(t_ref, into, i):
        return pltpu.make_async_copy(
            yb_ref.at[pl.ds(pl.multiple_of(t_ref[i] * SUBLANES, SUBLANES), SUBLANES)],
            rows_ref.at[into, pl.ds(pl.multiple_of(i * SUBLANES, SUBLANES), SUBLANES)], sems.at[into])

    def gather(t_ref, into):
        _for_tiles(t_ref[TILE_TABLE - 1], lambda i: tile_copy(t_ref, into, i).start())

    @pl.when(r == 0)
    def _():
        gather(tab_ref, 0)

    @pl.when(r + 1 < pl.num_programs(0))
    def _():
        gather(next_tab_ref, 1 - slot)

    n_tiles = tab_ref[TILE_TABLE - 1]

    def many():
        return pltpu.make_async_copy(yb_ref.at[pl.ds(0, WAIT_TILES * SUBLANES)],
                                     rows_ref.at[slot, pl.ds(0, WAIT_TILES * SUBLANES)], sems.at[slot])

    _wait_tiles(n_tiles, many, lambda: tile_copy(tab_ref, slot, 0))

    pos = _local_pos(idx_ref, rank_ref, off_ref)
    lane = lax.broadcasted_iota(jnp.int32, (tm, rows), 1).astype(F32)
    gates = gate_ref[...]
    unperm = jnp.where(lane == pos[0], gates[:, 0:1], 0.0)
    for k in range(1, TOP_K):
        unperm = unperm + jnp.where(lane == pos[k], gates[:, k:k + 1], 0.0)
    row = lax.broadcasted_iota(jnp.int32, (rows, 1), 0)
    y_hi, y_lo = _unpack_halves(jnp.where(row < n_tiles * SUBLANES, rows_ref[slot], jnp.uint32(0)))
    unperm = unperm.astype(BF16)
    f = jnp.concatenate([jnp.dot(unperm, y_hi, preferred_element_type=F32),
                         jnp.dot(unperm, y_lo, preferred_element_type=F32)], axis=1)
    o_ref[...] = _layer_norm(alpha * x1_ref[...] + mod_ref[5:6, :] * f, g_ref[...], beta_ref[...])


def _combine(idx, rank, gates, x1, mod, ln_g, ln_b, yb, plan, alpha, nl, nc):
    t, d = x1.shape
    tm = ROW_BLOCK
    n_exp = plan["loc_off"].shape[2]
    row_spec = pl.BlockSpec((tm, d), lambda r: (r, 0))
    vec = pl.BlockSpec((1, d), lambda r: (0, 0))
    k_spec = pl.BlockSpec((tm, TOP_K), lambda r: (r, 0))
    return pl.pallas_call(
        functools.partial(_combine_kernel, alpha),
        grid=(t // tm,),
        in_specs=[
            pl.BlockSpec((TILE_TABLE,), lambda r: (r,), memory_space=pltpu.SMEM),
            pl.BlockSpec((TILE_TABLE,), lambda r: (jnp.minimum(r + 1, t // tm - 1),), memory_space=pltpu.SMEM),
            k_spec, k_spec,
            pl.BlockSpec((None, 1, n_exp), lambda r: (r, 0, 0)),
            k_spec,
            row_spec,
            pl.BlockSpec((None, 6, d), lambda r, f=_mod_row_map(nl, nc): (f(r), 0, 0)),
            vec, vec,
            pl.BlockSpec(memory_space=pl.ANY),
        ],
        out_specs=row_spec,
        out_shape=jax.ShapeDtypeStruct((t, d), F32),
        scratch_shapes=[pltpu.VMEM((2, _local_rows(n_exp), d // 2), jnp.uint32), pltpu.SemaphoreType.DMA((2,))],
        compiler_params=_cparams("arbitrary"),
        name="moe_combine_ln",
    )(plan["tile_table"], plan["tile_table"], idx, rank, plan["loc_off"], gates, x1, mod,
      ln_g.reshape(1, d), ln_b.reshape(1, d), yb)


def _moe_plan(cnt, n_blocks):
    nblk, n_exp = cnt.shape
    bm = EXPERT_BLOCK
    n_tiles_max = _local_rows(n_exp) // SUBLANES
    experts = jnp.arange(n_exp)
    blocks = jnp.arange(nblk)
    seg = (cnt + SUBLANES - 1) // SUBLANES * SUBLANES
    loc_off = jnp.sum(jnp.where(experts[None, None, :] < experts[None, :, None], seg[:, None, :], 0), axis=2)
    before = jnp.sum(jnp.where(blocks[None, :, None] < blocks[:, None, None], seg[None, :, :], 0), axis=1)
    rows_e = jnp.sum(seg, axis=0)
    padded = (rows_e + bm - 1) // bm * bm
    pend = jnp.sum(jnp.where(experts[None, :] <= experts[:, None], padded[None, :], 0), axis=1)
    pstart = pend - padded
    base = pstart[None, :] + before
    first_row = (jnp.arange(n_tiles_max) * SUBLANES)[None, :, None]
    e_of = jnp.sum(((loc_off + seg)[:, None, :] <= first_row).astype(jnp.int32), axis=2)
    pick = jnp.minimum(e_of, n_exp - 1)[:, :, None] == experts
    delta = jnp.sum(jnp.where(pick, (base - loc_off)[:, None, :], 0), axis=2)
    tile_of = (delta + first_row[:, :, 0]) // SUBLANES
    table = jnp.concatenate([tile_of.astype(jnp.int32),
                             jnp.zeros((nblk, TILE_TABLE - 1 - n_tiles_max), jnp.int32),
                             (jnp.sum(seg, axis=1, keepdims=True) // SUBLANES).astype(jnp.int32)], axis=1)
    block_e = jnp.sum((pend[None, :] <= (jnp.arange(n_blocks) * bm)[:, None]).astype(jnp.int32), axis=1)
    block_e = jnp.minimum(block_e, n_exp - 1)
    later = jnp.logical_and(experts[None, :] > experts[:, None], (padded > 0)[None, :])
    next_of = jnp.min(jnp.where(later, experts[None, :], n_exp), axis=1)
    next_e = jnp.sum(jnp.where(block_e[:, None] == experts, next_of[None, :], 0), axis=1)
    next_e = jnp.where(next_e >= n_exp, -1, next_e)
    return dict(
        next_e=next_e.astype(jnp.int32),
        loc_off=loc_off.astype(F32).reshape(nblk, 1, n_exp),
        tile_table=table.reshape(-1),
        tail_start=(pstart + rows_e).astype(jnp.int32),
        pad_len=(padded - rows_e).astype(jnp.int32),
        block_e=block_e.astype(jnp.int32),
        n_used=(pend[-1:] // bm).astype(jnp.int32),
    )


def kernel(x, c, ctx, c_ctx, w_mod, b_mod, ln1_g, ln1_b, ln2_g, ln2_b, attn_w_qkv, attn_w_o, attn_lam_q1, attn_lam_k1, attn_lam_q2, attn_lam_k2, attn_subln_g, fnet_w, fnet_b, conv_w_pw1, conv_b_pw1, conv_w_dw, conv_b_dw, conv_ln_g, conv_ln_b, conv_w_pw2, conv_b_pw2, moe_w_router, moe_b_router, moe_w1, moe_b1, moe_w2, moe_b2):
    b, n, d = x.shape
    cl = ctx.shape[1]
    s = n + cl
    t = b * s
    depth = w_mod.shape[0]
    n_exp = moe_w_router.shape[2]
    nl, nc = n // ROW_BLOCK, cl // ROW_BLOCK
    alpha = (2 * depth) ** 0.25

    def grouped_blocks(tokens):
        worst_rows = tokens * TOP_K + (tokens // ROW_BLOCK) * n_exp * (SUBLANES - 1)
        return -(-worst_rows // EXPERT_BLOCK) + n_exp

    xs = jnp.concatenate([x, ctx], axis=1).reshape(t, d)
    cond = jnp.stack([c, jnp.broadcast_to(c_ctx, c.shape)], axis=1).reshape(2 * b, d)
    mod = _modulation(cond, w_mod, b_mod)
    cos, sin = _rope_tables(n, cl)

    for i in range(depth):
        kind, j = i % 3, i // 3
        last = i == depth - 1
        if kind == 0:
            lam_init = 0.8 - 0.6 * float(np.exp(-0.3 * i))
            q, k, v = _qkv(xs, mod[i], attn_w_qkv[j], cos, sin, nl, nc)
            q3, k3 = q.reshape(b, s, d), k.reshape(b, s, d)
            vt = v.reshape(b, s, d // HEAD_W, HEAD_W).transpose(0, 2, 3, 1)
            vt = jnp.concatenate([vt, jnp.ones((b, d // HEAD_W, SUM_ROWS, s), BF16)], axis=2)
            lam_params = (attn_lam_q1[j], attn_lam_k1[j], attn_lam_q2[j], attn_lam_k2[j])
            a_lat = _attention(q3, k3, vt, lam_params, attn_subln_g[j], lam_init, n, cl, context=False)
            a_ctx = None if last else _attention(q3, k3, vt, lam_params, attn_subln_g[j], lam_init, n, cl,
                                                 context=True)
            w_out, b_out = attn_w_o[j], jnp.zeros((d,), F32)
        elif kind == 1:
            fa, fb = _chan_dft(xs, mod[i], nl, nc)
            fa3, fb3 = fa.reshape(b, s, d), fb.reshape(b, s, d)
            a_lat = _seq_dft(fa3, fb3, n, 0)
            a_ctx = None if last else _seq_dft(fa3, fb3, cl, n)
            w_out, b_out = fnet_w[j], fnet_b[j]
        else:
            hid = _glu(xs, mod[i], conv_w_pw1[j], conv_b_pw1[j], nl, nc)
            a_all = _dwconv_ln_silu(hid, conv_w_dw[j], conv_b_dw[j], conv_ln_g[j], conv_ln_b[j], nl, nc)
            a_all = a_all.reshape(b, s, d)
            a_lat, a_ctx = a_all[:, :n], a_all[:, n:]
            w_out, b_out = conv_w_pw2[j], conv_b_pw2[j]

        if last:
            a, xs, tokens, nc_now = a_lat.reshape(b * n, d), xs.reshape(b, s, d)[:, :n].reshape(b * n, d), b * n, 0
        else:
            a, tokens, nc_now = jnp.concatenate([a_lat, a_ctx], axis=1).reshape(t, d), t, nc
        n_blocks = grouped_blocks(tokens)
        x1, v, idx, gates, rank, cnt = _proj_route(
            a, w_out, b_out, xs, mod[i], ln1_g[i], ln1_b[i], moe_w_router[i], moe_b_router[i], alpha, nl, nc_now)
        plan = _moe_plan(cnt[:, 0, :], n_blocks)
        xb = _dispatch(v, idx, rank, plan, n_blocks * EXPERT_BLOCK)
        yb = _experts(xb, plan["block_e"], plan["next_e"], plan["n_used"], i, moe_w1, moe_b1, moe_w2, moe_b2)
        xs = _combine(idx, rank, gates, x1, mod[i], ln2_g[i], ln2_b[i], yb, plan, alpha, nl, nc_now)

    return xs.reshape(b, n, d)
```

```python
import functools

import jax
import jax.numpy as jnp
import numpy as np
from jax import lax
from jax.experimental import pallas as pl
from jax.experimental.pallas import tpu as pltpu

F32 = jnp.float32
BF16 = jnp.bfloat16
HIGHEST = lax.Precision.HIGHEST

GRID_W = 64
HEAD_W = 128
MAP_W = HEAD_W // 2
ROPE_FREQS = MAP_W // 4
ROPE_BASE = 10000.0
CONV_WIDTH = 31
CONV_PAD = CONV_WIDTH // 2
TOP_K = 4
SWIGLU_ALPHA = 1.702
SWIGLU_LIMIT = 7.0
LN_EPS = 1e-5

LANES = 128
SUBLANES = 8
ROW_BLOCK = 256
SUM_ROWS = 8
HALO = 16
ATTN_TQ = 512
ATTN_TK = 512
ATTN_QC = 256
DFT_TILE = 1024
EXPERT_BLOCK = 512
TILE_TABLE = 1024
ISSUE_UNROLL = 8
WAIT_TILES = 16
VMEM_LIMIT = 56 * 1024 * 1024


def _cparams(*sem):
    return pltpu.CompilerParams(dimension_semantics=sem, vmem_limit_bytes=VMEM_LIMIT)


def _mod_row_map(nl, nc):
    rb = nl + nc

    def f(r):
        return 2 * (r // rb) + jnp.where((r % rb) >= nl, 1, 0)

    return f


def _layer_norm(z, g, b):
    mu = jnp.mean(z, axis=-1, keepdims=True)
    zc = z - mu
    var = jnp.mean(zc * zc, axis=-1, keepdims=True)
    return zc * lax.rsqrt(var + LN_EPS) * g + b


def _sigmoid(x):
    return 1.0 / (1.0 + jnp.exp(-x))


def _modulation_kernel(cond_ref, w_ref, b_ref, o_ref):
    cnd = cond_ref[...]
    s = cnd * _sigmoid(cnd)
    o_ref[...] = jnp.dot(s, w_ref[...], precision=HIGHEST, preferred_element_type=F32) + b_ref[...]


def _modulation(cond, w_mod, b_mod):
    depth, d, d6 = w_mod.shape
    rows = cond.shape[0]
    out = pl.pallas_call(
        _modulation_kernel,
        grid=(depth, d6 // d),
        in_specs=[
            pl.BlockSpec((rows, d), lambda i, j: (0, 0)),
            pl.BlockSpec((None, d, d), lambda i, j: (i, 0, j)),
            pl.BlockSpec((None, 1, d), lambda i, j: (i, 0, j)),
        ],
        out_specs=pl.BlockSpec((None, rows, d), lambda i, j: (i, 0, j)),
        out_shape=jax.ShapeDtypeStruct((depth, rows, d6), F32),
        compiler_params=_cparams("parallel", "parallel"),
        name="modulation",
    )(cond, w_mod, b_mod.reshape(depth, 1, d6))
    return out.reshape(depth, rows, d6 // d, d)


def _rope_tables(n, c):
    pos = jnp.arange(n)
    rows = (pos // GRID_W).astype(F32)
    cols = (pos % GRID_W).astype(F32)
    inv_freq = ROPE_BASE ** (-jnp.arange(ROPE_FREQS, dtype=F32) / ROPE_FREQS)
    lane = np.arange(HEAD_W)
    dim = lane % MAP_W
    freq = dim % ROPE_FREQS
    use_row = (dim // (MAP_W // 2)) == 0
    first_half = (dim % (MAP_W // 2)) < ROPE_FREQS
    ang = jnp.where(use_row[None, :], rows[:, None] * inv_freq[freq][None, :],
                    cols[:, None] * inv_freq[freq][None, :])
    cos = jnp.cos(ang)
    sin = jnp.sin(ang)
    sin = jnp.where(first_half[None, :], -sin, sin)
    cos = jnp.concatenate([cos, jnp.ones((c, HEAD_W), F32)], axis=0)
    sin = jnp.concatenate([sin, jnp.zeros((c, HEAD_W), F32)], axis=0)
    return cos, sin


def _qkv_kernel(x_ref, mod_ref, w_ref, cos_ref, sin_ref, q_ref, k_ref, v_ref):
    d = x_ref.shape[1]
    u = (x_ref[...] * (1.0 + mod_ref[1:2, :]) + mod_ref[0:1, :]).astype(BF16)
    cos = cos_ref[...]
    sin = sin_ref[...]
    lane = lax.broadcasted_iota(jnp.int32, cos.shape, 1)
    first_half = (lane % (MAP_W // 2)) < ROPE_FREQS

    def rope(t):
        partner = jnp.where(first_half, pltpu.roll(t, HEAD_W - ROPE_FREQS, 1), pltpu.roll(t, ROPE_FREQS, 1))
        return t * cos + partner * sin

    q = jnp.dot(u, w_ref[:, :d], preferred_element_type=F32)
    k = jnp.dot(u, w_ref[:, d:2 * d], preferred_element_type=F32)
    for h in range(d // HEAD_W):
        cs = slice(h * HEAD_W, (h + 1) * HEAD_W)
        q_ref[:, cs] = rope(q[:, cs]).astype(BF16)
        k_ref[:, cs] = rope(k[:, cs]).astype(BF16)
    v_ref[...] = jnp.dot(u, w_ref[:, 2 * d:], preferred_element_type=F32).astype(BF16)


def _qkv(x, mod, w_qkv, cos, sin, nl, nc):
    t, d = x.shape
    rb = nl + nc
    tm = ROW_BLOCK
    scale = jnp.concatenate([jnp.full((d,), np.log2(np.e) * MAP_W ** -0.5, F32), jnp.ones((2 * d,), F32)])
    w = (w_qkv * scale[None, :]).astype(BF16)
    row_spec = pl.BlockSpec((tm, d), lambda r: (r, 0))
    tab_spec = pl.BlockSpec((tm, HEAD_W), lambda r: (r % rb, 0))
    return pl.pallas_call(
        _qkv_kernel,
        grid=(t // tm,),
        in_specs=[
            row_spec,
            pl.BlockSpec((None, 6, d), lambda r, f=_mod_row_map(nl, nc): (f(r), 0, 0)),
            pl.BlockSpec((d, 3 * d), lambda r: (0, 0)),
            tab_spec, tab_spec,
        ],
        out_specs=[row_spec, row_spec, row_spec],
        out_shape=[jax.ShapeDtypeStruct((t, d), BF16)] * 3,
        compiler_params=_cparams("parallel"),
        name="qkv_rope",
    )(x, mod, w, cos, sin)


def _attn_kernel(lam_init, tiles, q_ref, k_ref, vt_ref, lq1_ref, lk1_ref, lq2_ref, lk2_ref, g_ref,
                 o_ref, acc1_ref, acc2_ref, sa1_ref, sa2_ref, sb1_ref, sb2_ref):
    tq = q_ref.shape[0]
    q = q_ref[...].astype(F32)
    lane = lax.broadcasted_iota(jnp.int32, q.shape, 1)
    q1 = jnp.where(lane < MAP_W, q, 0.0).astype(BF16)
    q2 = jnp.where(lane >= MAP_W, q, 0.0).astype(BF16)
    acc1_ref[...] = jnp.zeros_like(acc1_ref)
    acc2_ref[...] = jnp.zeros_like(acc2_ref)
    acc_refs = (acc1_ref, acc2_ref)
    s_refs = ((sa1_ref, sa2_ref), (sb1_ref, sb2_ref))

    qc = min(tq, ATTN_QC)
    pieces = [(mp, c) for c in range(tq // qc) for mp in range(2)]
    qms = (q1, q2)

    def score_piece(i, mp, c):
        off, size = tiles[i]
        s = lax.dot_general(k_ref[off:off + size, :], qms[mp][c * qc:(c + 1) * qc, :],
                            (((1,), (1,)), ((), ())), preferred_element_type=F32)
        s_refs[i % 2][mp][0:size, c * qc:(c + 1) * qc] = s
        return jnp.max(s, axis=0, keepdims=True)

    def acc_piece(i, mp, c, cm, m):
        off, size = tiles[i]
        cs = slice(c * qc, (c + 1) * qc)
        m_new = jnp.maximum(m, cm)
        p = jnp.exp2(s_refs[i % 2][mp][0:size, cs] - m_new).astype(BF16)
        acc_refs[mp][:, cs] = (jnp.exp2(m - m_new) * acc_refs[mp][:, cs]
                               + jnp.dot(vt_ref[:, off:off + size], p, preferred_element_type=F32))
        return m_new

    def stage(i_score, i_acc, cm_acc, ms):
        cm_new, ms_new = [], []
        for n_piece, (mp, c) in enumerate(pieces):
            if i_score is not None:
                cm_new.append(score_piece(i_score, mp, c))
            if i_acc is not None:
                ms_new.append(acc_piece(i_acc, mp, c, cm_acc[n_piece], ms[n_piece]))
        return tuple(cm_new), tuple(ms_new) if i_acc is not None else ms

    ms = tuple(jnp.full((1, qc), -jnp.inf, F32) for _ in pieces)
    cm, _ = stage(0, None, None, ms)
    for i in range(len(tiles) - 1):
        cm, ms = stage(i + 1, i, cm, ms)
    stage(None, len(tiles) - 1, cm, ms)

    lam = (jnp.exp(jnp.sum(lq1_ref[...] * lk1_ref[...], axis=1, keepdims=True))
           - jnp.exp(jnp.sum(lq2_ref[...] * lk2_ref[...], axis=1, keepdims=True)) + lam_init)
    inv1 = 1.0 / acc1_ref[HEAD_W:HEAD_W + 1, :]
    inv2 = 1.0 / acc2_ref[HEAD_W:HEAD_W + 1, :]
    o = acc1_ref[0:HEAD_W, :] * inv1 - lam * (acc2_ref[0:HEAD_W, :] * inv2)
    ms = jnp.mean(o * o, axis=0, keepdims=True)
    o = o * lax.rsqrt(ms + LN_EPS) * (g_ref[...] * (1.0 - lam_init))
    o_ref[...] = o.T.astype(BF16)


def _attention(q3, k3, vt, lam_params, subln_g, lam_init, n, c, context):
    b, s, d = q3.shape
    h = d // HEAD_W
    if context:
        tq, kv_len, kv_blk, q_off, nq = c, c, n // c, n // c, 1
    else:
        tq, kv_len, kv_blk, q_off, nq = ATTN_TQ, s, 0, 0, n // ATTN_TQ
    tk = min(ATTN_TK, kv_len)
    tiles = tuple((off, min(tk, kv_len - off)) for off in range(0, kv_len, tk))
    lam_spec = pl.BlockSpec((1, MAP_W), lambda bi, hi, qi: (0, 0))
    return pl.pallas_call(
        functools.partial(_attn_kernel, lam_init, tiles),
        grid=(b, h, nq),
        in_specs=[
            pl.BlockSpec((None, tq, HEAD_W), lambda bi, hi, qi: (bi, qi + q_off, hi)),
            pl.BlockSpec((None, kv_len, HEAD_W), lambda bi, hi, qi: (bi, kv_blk, hi)),
            pl.BlockSpec((None, None, HEAD_W + SUM_ROWS, kv_len), lambda bi, hi, qi: (bi, hi, 0, kv_blk)),
            lam_spec, lam_spec, lam_spec, lam_spec,
            pl.BlockSpec((HEAD_W, 1), lambda bi, hi, qi: (0, 0)),
        ],
        out_specs=pl.BlockSpec((None, tq, HEAD_W), lambda bi, hi, qi: (bi, qi, hi)),
        out_shape=jax.ShapeDtypeStruct((b, tq * nq, d), BF16),
        scratch_shapes=[pltpu.VMEM((HEAD_W + SUM_ROWS, tq), F32)] * 2 + [pltpu.VMEM((tk, tq), F32)] * 4,
        compiler_params=_cparams("parallel", "parallel", "parallel"),
        name="attn_ctx" if context else "attn_lat",
    )(q3, k3, vt, *[p.reshape(1, MAP_W) for p in lam_params], subln_g.reshape(HEAD_W, 1))


def _dft_matrices(n, scale):
    blk = min(ROW_BLOCK, n)
    k = jnp.arange(n, dtype=jnp.int32)

    def cs(j):
        r = (j[:, None] * k[None, :]) % n
        ang = r.astype(F32) * (2.0 * np.pi / n)
        return jnp.cos(ang), jnp.sin(ang)

    c0, s0 = cs(jnp.arange(blk, dtype=jnp.int32))
    cj, sj = cs(jnp.arange(0, n, blk, dtype=jnp.int32))
    cm = cj[:, None, :] * c0[None] - sj[:, None, :] * s0[None]
    sm = sj[:, None, :] * c0[None] + cj[:, None, :] * s0[None]
    return (cm * scale).reshape(n, n).astype(BF16), (sm * scale).reshape(n, n).astype(BF16)


def _chan_dft_kernel(x_ref, mod_ref, cs_ref, a_ref, b_ref):
    d = x_ref.shape[1]
    u = (x_ref[...] * (1.0 + mod_ref[1:2, :]) + mod_ref[0:1, :]).astype(BF16)
    for g in range(d // LANES):
        sl = slice(g * LANES, (g + 1) * LANES)
        ab = jnp.dot(u[:, sl], cs_ref[...], preferred_element_type=F32)
        a_ref[:, sl] = ab[:, :LANES].astype(BF16)
        b_ref[:, sl] = ab[:, LANES:].astype(BF16)


def _chan_dft(x, mod, nl, nc):
    t, d = x.shape
    tm = ROW_BLOCK
    j = np.arange(LANES)
    ang = 2.0 * np.pi * ((j[:, None] * j[None, :]) % LANES) / LANES
    cs = jnp.asarray(np.concatenate([np.cos(ang), np.sin(ang)], axis=1) / np.sqrt(LANES), BF16)
    row_spec = pl.BlockSpec((tm, d), lambda r: (r, 0))
    return pl.pallas_call(
        _chan_dft_kernel,
        grid=(t // tm,),
        in_specs=[
            row_spec,
            pl.BlockSpec((None, 6, d), lambda r, f=_mod_row_map(nl, nc): (f(r), 0, 0)),
            pl.BlockSpec((LANES, 2 * LANES), lambda r: (0, 0)),
        ],
        out_specs=[row_spec, row_spec],
        out_shape=[jax.ShapeDtypeStruct((t, d), BF16)] * 2,
        compiler_params=_cparams("parallel"),
        name="chan_dft",
    )(x, mod, cs)


def _seq_dft_kernel(c_ref, s_ref, a_ref, b_ref, o_ref, acc_ref):
    kk = pl.program_id(2)

    @pl.when(kk == 0)
    def _():
        acc_ref[...] = jnp.zeros_like(acc_ref)

    acc_ref[...] += (jnp.dot(c_ref[...], a_ref[...], preferred_element_type=F32)
                     - jnp.dot(s_ref[...], b_ref[...], preferred_element_type=F32))

    @pl.when(kk == pl.num_programs(2) - 1)
    def _():
        o_ref[...] = acc_ref[...].astype(o_ref.dtype)


def _seq_dft(a3, b3, length, row_off):
    b, s, d = a3.shape
    tile = min(DFT_TILE, length)
    cm, sm = _dft_matrices(length, length ** -0.5)
    nt = length // tile
    off = row_off // tile
    mat_spec = pl.BlockSpec((tile, tile), lambda bi, m, k: (m, k))
    in_spec = pl.BlockSpec((None, tile, d), lambda bi, m, k: (bi, k + off, 0))
    return pl.pallas_call(
        _seq_dft_kernel,
        grid=(b, nt, nt),
        in_specs=[mat_spec, mat_spec, in_spec, in_spec],
        out_specs=pl.BlockSpec((None, tile, d), lambda bi, m, k: (bi, m, 0)),
        out_shape=jax.ShapeDtypeStruct((b, length, d), BF16),
        scratch_shapes=[pltpu.VMEM((tile, d), F32)],
        compiler_params=_cparams("parallel", "parallel", "arbitrary"),
        name="seq_dft",
    )(cm, sm, a3, b3)


def _glu_kernel(x_ref, mod_ref, w_ref, b_ref, o_ref):
    d = x_ref.shape[1]
    u = (x_ref[...] * (1.0 + mod_ref[1:2, :]) + mod_ref[0:1, :]).astype(BF16)
    a = jnp.dot(u, w_ref[:, :d], preferred_element_type=F32) + b_ref[:, :d]
    g = jnp.dot(u, w_ref[:, d:], preferred_element_type=F32) + b_ref[:, d:]
    o_ref[...] = a * _sigmoid(g)


def _glu(x, mod, w, bias, nl, nc):
    t, d = x.shape
    tm = ROW_BLOCK
    row_spec = pl.BlockSpec((tm, d), lambda r: (r, 0))
    return pl.pallas_call(
        _glu_kernel,
        grid=(t // tm,),
        in_specs=[
            row_spec,
            pl.BlockSpec((None, 6, d), lambda r, f=_mod_row_map(nl, nc): (f(r), 0, 0)),
            pl.BlockSpec((d, 2 * d), lambda r: (0, 0)),
            pl.BlockSpec((1, 2 * d), lambda r: (0, 0)),
        ],
        out_specs=row_spec,
        out_shape=jax.ShapeDtypeStruct((t, d), F32),
        compiler_params=_cparams("parallel"),
        name="pw1_glu",
    )(x, mod, w.astype(BF16), bias.reshape(1, 2 * d))


def _dwconv_kernel(nl, nc, prev_ref, cur_ref, next_ref, w_ref, b_ref, g_ref, beta_ref, o_ref, win_ref, h_ref):
    tm, d = cur_ref.shape
    rr = pl.program_id(0) % (nl + nc)
    has_prev = jnp.logical_and(rr != 0, rr != nl)
    has_next = jnp.logical_and(rr != nl - 1, rr != nl + nc - 1)
    win_ref[0, 0:HALO, :] = jnp.where(has_prev, prev_ref[...], 0.0)
    win_ref[0, HALO:HALO + tm, :] = cur_ref[...]
    win_ref[0, HALO + tm:, :] = jnp.where(has_next, next_ref[...], 0.0)
    span = tm + 2 * HALO - SUBLANES
    for s in range(1, SUBLANES):
        win_ref[s, 0:span, :] = win_ref[0, s:s + span, :]
    rows = 64
    for c in range(d // LANES):
        cs = slice(c * LANES, (c + 1) * LANES)
        wc = w_ref[:, cs]
        for r0 in range(0, tm, rows):
            acc = jnp.broadcast_to(b_ref[:, cs], (rows, LANES))
            for j in range(CONV_WIDTH):
                off = HALO - CONV_PAD + j
                start = r0 + off - off % SUBLANES
                acc = acc + wc[j:j + 1, :] * win_ref[off % SUBLANES, start:start + rows, cs]
            h_ref[r0:r0 + rows, cs] = acc
    y = _layer_norm(h_ref[...], g_ref[...], beta_ref[...])
    o_ref[...] = (y * _sigmoid(y)).astype(BF16)


def _dwconv_ln_silu(hid, w_dw, b_dw, ln_g, ln_b, nl, nc):
    t, d = hid.shape
    tm = ROW_BLOCK
    per = tm // HALO
    n_halo = t // HALO
    vec = pl.BlockSpec((1, d), lambda r: (0, 0))
    return pl.pallas_call(
        functools.partial(_dwconv_kernel, nl, nc),
        grid=(t // tm,),
        in_specs=[
            pl.BlockSpec((HALO, d), lambda r: (jnp.maximum(r * per - 1, 0), 0)),
            pl.BlockSpec((tm, d), lambda r: (r, 0)),
            pl.BlockSpec((HALO, d), lambda r: (jnp.minimum((r + 1) * per, n_halo - 1), 0)),
            pl.BlockSpec((CONV_WIDTH, d), lambda r: (0, 0)),
            vec, vec, vec,
        ],
        out_specs=pl.BlockSpec((tm, d), lambda r: (r, 0)),
        out_shape=jax.ShapeDtypeStruct((t, d), BF16),
        scratch_shapes=[pltpu.VMEM((SUBLANES, tm + 2 * HALO, d), F32), pltpu.VMEM((tm, d), F32)],
        compiler_params=_cparams("parallel"),
        name="dwconv_ln_silu",
    )(hid, hid, hid, w_dw, b_dw.reshape(1, d), ln_g.reshape(1, d), ln_b.reshape(1, d))


def _pack_cols(cols, dtype):
    rows = cols[0].shape[0]
    lane = lax.broadcasted_iota(jnp.int32, (rows, len(cols)), 1)
    out = jnp.zeros((rows, len(cols)), dtype)
    for i, col in enumerate(cols):
        out = jnp.where(lane == i, col.astype(dtype), out)
    return out


def _proj_route_kernel(alpha, a_ref, w_ref, bias_ref, x_ref, mod_ref, g_ref, beta_ref, wr_ref, br_ref,
                       x1_ref, v_ref, idx_ref, gate_ref, rank_ref, cnt_ref):
    tm = x_ref.shape[0]
    n_exp = wr_ref.shape[2]

    y = jnp.dot(a_ref[...], w_ref[...], preferred_element_type=F32) + bias_ref[...]
    x1 = _layer_norm(alpha * x_ref[...] + mod_ref[2:3, :] * y, g_ref[...], beta_ref[...])
    x1_ref[...] = x1
    v = x1 * (1.0 + mod_ref[4:5, :]) + mod_ref[3:4, :]
    v_ref[...] = v

    v_hi = v.astype(BF16)
    v_lo = (v - v_hi.astype(F32)).astype(BF16)
    logits = (jnp.dot(v_hi, wr_ref[0], preferred_element_type=F32)
              + jnp.dot(v_lo, wr_ref[0], preferred_element_type=F32)
              + jnp.dot(v_hi, wr_ref[1], preferred_element_type=F32)) + br_ref[...]
    col = lax.broadcasted_iota(jnp.int32, logits.shape, 1).astype(F32)
    work = logits
    vals, idxs, sels = [], [], []
    for _ in range(TOP_K):
        mx = jnp.max(work, axis=1, keepdims=True)
        first = jnp.min(jnp.where(work == mx, col, float(n_exp)), axis=1, keepdims=True)
        sel = col == first
        work = jnp.where(sel, -jnp.inf, work)
        vals.append(mx)
        idxs.append(first)
        sels.append(sel)
    es = [jnp.exp(val - vals[0]) for val in vals]
    inv = 1.0 / (es[0] + es[1] + es[2] + es[3])

    onehot = jnp.zeros(logits.shape, F32)
    for sel in sels:
        onehot = onehot + jnp.where(sel, 1.0, 0.0)
    ri = lax.broadcasted_iota(jnp.int32, (tm, tm), 0)
    ci = lax.broadcasted_iota(jnp.int32, (tm, tm), 1)
    tri = jnp.where(ci < ri, 1.0, 0.0).astype(BF16)
    before = jnp.dot(tri, onehot.astype(BF16), preferred_element_type=F32)
    ranks = [jnp.sum(jnp.where(sel, before, 0.0), axis=1, keepdims=True) for sel in sels]

    idx_ref[...] = _pack_cols(idxs, jnp.int32)
    gate_ref[...] = _pack_cols([e * inv for e in es], F32)
    rank_ref[...] = _pack_cols(ranks, jnp.int32)
    cnt_ref[...] = jnp.sum(onehot, axis=0, keepdims=True).astype(jnp.int32)


def _proj_route(a, w, bias, x, mod, ln_g, ln_b, w_router, b_router, alpha, nl, nc, x_skip=0):
    t, d = a.shape
    n_exp = w_router.shape[1]
    tm = ROW_BLOCK
    row_spec = pl.BlockSpec((tm, d), lambda r: (r, 0))
    x_spec = pl.BlockSpec((tm, d), lambda r: (r + (r // nl) * x_skip, 0))
    vec = pl.BlockSpec((1, d), lambda r: (0, 0))
    k_spec = pl.BlockSpec((tm, TOP_K), lambda r: (r, 0))
    wr_hi = w_router.astype(BF16)
    return pl.pallas_call(
        functools.partial(_proj_route_kernel, alpha),
        grid=(t // tm,),
        in_specs=[
            row_spec,
            pl.BlockSpec((d, d), lambda r: (0, 0)),
            vec,
            x_spec,
            pl.BlockSpec((None, 6, d), lambda r, f=_mod_row_map(nl, nc): (f(r), 0, 0)),
            vec, vec,
            pl.BlockSpec((2, d, n_exp), lambda r: (0, 0, 0)),
            pl.BlockSpec((1, n_exp), lambda r: (0, 0)),
        ],
        out_specs=[row_spec, row_spec, k_spec, k_spec, k_spec,
                   pl.BlockSpec((None, 1, n_exp), lambda r: (r, 0, 0))],
        out_shape=[
            jax.ShapeDtypeStruct((t, d), F32),
            jax.ShapeDtypeStruct((t, d), F32),
            jax.ShapeDtypeStruct((t, TOP_K), jnp.int32),
            jax.ShapeDtypeStruct((t, TOP_K), F32),
            jax.ShapeDtypeStruct((t, TOP_K), jnp.int32),
            jax.ShapeDtypeStruct((t // tm, 1, n_exp), jnp.int32),
        ],
        compiler_params=_cparams("parallel"),
        name="proj_ln_route",
    )(a, w.astype(BF16), bias.reshape(1, d), x, mod, ln_g.reshape(1, d), ln_b.reshape(1, d),
      jnp.stack([wr_hi, (w_router - wr_hi.astype(F32)).astype(BF16)]), b_router.reshape(1, n_exp))


def _local_rows(n_exp):
    return TOP_K * ROW_BLOCK + n_exp * SUBLANES


def _for_tiles(n_tiles, body):
    def trip(g, carry):
        for u in range(ISSUE_UNROLL):
            body(g * ISSUE_UNROLL + u)
        return carry

    def single(i, carry):
        body(i)
        return carry

    full = n_tiles // ISSUE_UNROLL
    lax.fori_loop(0, full, trip, 0)
    lax.fori_loop(full * ISSUE_UNROLL, n_tiles, single, 0)


def _wait_tiles(n_tiles, many, one):
    def wait_many(g, carry):
        many().wait()
        return carry

    def wait_one(i, carry):
        one().wait()
        return carry

    lax.fori_loop(0, n_tiles // WAIT_TILES, wait_many, 0)
    lax.fori_loop(0, n_tiles % WAIT_TILES, wait_one, 0)


def _pack_halves(x):
    half = x.shape[1] // 2
    hi = lax.bitcast_convert_type(x[:, :half], jnp.uint32)
    lo = lax.bitcast_convert_type(x[:, half:], jnp.uint32)
    return (hi & jnp.uint32(0xFFFF0000)) | (lo >> 16)


def _unpack_halves(w):
    hi = lax.bitcast_convert_type(w & jnp.uint32(0xFFFF0000), F32)
    lo = lax.bitcast_convert_type(w << 16, F32)
    return hi.astype(BF16), lo.astype(BF16)


def _local_pos(idx_ref, rank_ref, off_ref):
    tm = idx_ref.shape[0]
    col = lax.broadcasted_iota(jnp.int32, (tm, off_ref.shape[1]), 1)
    idx = idx_ref[...]
    rank = rank_ref[...]
    pos = []
    for k in range(TOP_K):
        start = jnp.sum(jnp.where(col == idx[:, k:k + 1], off_ref[...], 0.0), axis=1, keepdims=True)
        pos.append(start + rank[:, k:k + 1].astype(F32))
    return pos


def _dispatch_kernel(tail_ref, pad_ref, nu_ref, tab_ref, prev_tab_ref, v_ref, idx_ref, rank_ref, off_ref,
                     xb_ref, srt_ref, zero_ref, sems, fill_sem):
    r = pl.program_id(0)
    slot = r % 2
    tm = v_ref.shape[0]
    rows = srt_ref.shape[1]
    bm = zero_ref.shape[0]
    n_blocks = xb_ref.shape[0] // bm

    @pl.when(pl.program_id(0) == 0)
    def _():
        zero_ref[...] = jnp.zeros_like(zero_ref)

        def padding(do):
            for e in range(tail_ref.shape[0]):
                pad = pad_ref[e]
                for bit in reversed(range(SUBLANES.bit_length() - 1, bm.bit_length() - 1)):
                    size = 1 << bit
                    off = pl.multiple_of(tail_ref[e] + ((pad >> (bit + 1)) << (bit + 1)), SUBLANES)
                    copy = pltpu.make_async_copy(zero_ref.at[pl.ds(0, size)], xb_ref.at[pl.ds(off, size)], fill_sem)
                    pl.when((pad & size) != 0)(functools.partial(do, copy))

        def blocks(do):
            def body(j, carry):
                do(pltpu.make_async_copy(zero_ref, xb_ref.at[pl.ds(pl.multiple_of(j * bm, bm), bm)], fill_sem))
                return carry
            lax.fori_loop(nu_ref[0], n_blocks, body, 0)

        padding(lambda copy: copy.start())
        blocks(lambda copy: copy.start())
        padding(lambda copy: copy.wait())
        blocks(lambda copy: copy.wait())

    pos = _local_pos(idx_ref, rank_ref, off_ref)
    lane = lax.broadcasted_iota(jnp.int32, (tm, LANES), 1)
    pos_cols = jnp.zeros((tm, LANES), F32)
    for k in range(TOP_K):
        pos_cols = jnp.where(lane == k, pos[k], pos_cols)
    pos_rows = pos_cols.T
    row = lax.broadcasted_iota(jnp.int32, (rows, tm), 0).astype(F32)
    hit = row == pos_rows[0:1, :]
    for k in range(1, TOP_K):
        hit = jnp.logical_or(hit, row == pos_rows[k:k + 1, :])
    perm = jnp.where(hit, 1.0, 0.0).astype(BF16)
    srt_ref[slot] = _pack_halves(jnp.dot(perm, v_ref[...].astype(BF16), preferred_element_type=F32))

    def tile_copy(t_ref, buf, i):
        return pltpu.make_async_copy(
            srt_ref.at[buf, pl.ds(pl.multiple_of(i * SUBLANES, SUBLANES), SUBLANES)],
            xb_ref.at[pl.ds(pl.multiple_of(t_ref[i] * SUBLANES, SUBLANES), SUBLANES)], sems.at[buf])

    def drain(t_ref, buf):
        def many():
            return pltpu.make_async_copy(srt_ref.at[buf, pl.ds(0, WAIT_TILES * SUBLANES)],
                                         xb_ref.at[pl.ds(0, WAIT_TILES * SUBLANES)], sems.at[buf])
        _wait_tiles(t_ref[TILE_TABLE - 1], many, lambda: tile_copy(t_ref, buf, 0))

    _for_tiles(tab_ref[TILE_TABLE - 1], lambda i: tile_copy(tab_ref, slot, i).start())

    @pl.when(r > 0)
    def _():
        drain(prev_tab_ref, 1 - slot)

    @pl.when(r == pl.num_programs(0) - 1)
    def _():
        drain(tab_ref, slot)


def _dispatch(v, idx, rank, plan, p_rows):
    t, d = v.shape
    tm = ROW_BLOCK
    n_exp = plan["loc_off"].shape[2]
    k_spec = pl.BlockSpec((tm, TOP_K), lambda r, *_: (r, 0))
    grid_spec = pltpu.PrefetchScalarGridSpec(
        num_scalar_prefetch=3,
        grid=(t // tm,),
        in_specs=[
            pl.BlockSpec((TILE_TABLE,), lambda r, *_: (r,), memory_space=pltpu.SMEM),
            pl.BlockSpec((TILE_TABLE,), lambda r, *_: (jnp.maximum(r - 1, 0),), memory_space=pltpu.SMEM),
            pl.BlockSpec((tm, d), lambda r, *_: (r, 0)),
            k_spec, k_spec,
            pl.BlockSpec((None, 1, n_exp), lambda r, *_: (r, 0, 0)),
        ],
        out_specs=pl.BlockSpec(memory_space=pl.ANY),
        scratch_shapes=[pltpu.VMEM((2, _local_rows(n_exp), d // 2), jnp.uint32),
                        pltpu.VMEM((EXPERT_BLOCK, d // 2), jnp.uint32),
                        pltpu.SemaphoreType.DMA((2,)), pltpu.SemaphoreType.DMA],
    )
    return pl.pallas_call(
        _dispatch_kernel,
        grid_spec=grid_spec,
        out_shape=jax.ShapeDtypeStruct((p_rows, d // 2), jnp.uint32),
        compiler_params=_cparams("arbitrary"),
        name="moe_dispatch",
    )(plan["tail_start"], plan["pad_len"], plan["n_used"], plan["tile_table"], plan["tile_table"],
      v, idx, rank, plan["loc_off"])


def _expert_kernel(layer, be_ref, ne_ref, nu_ref, x_ref, w1_hbm, b1_ref, w2_hbm, b2_ref, o_ref,
                   w1s_ref, w2s_ref, w1b_ref, w2b_ref, sems):
    j = pl.program_id(0)
    f = w2b_ref.shape[0]
    prev = be_ref[jnp.maximum(j - 1, 0)]

    def fetch(e):
        return (pltpu.make_async_copy(w1_hbm.at[layer, e], w1s_ref, sems.at[0]),
                pltpu.make_async_copy(w2_hbm.at[layer, e], w2s_ref, sems.at[1]))

    @pl.when(j == 0)
    def _():
        for copy in fetch(be_ref[0]):
            copy.start()

    @pl.when(jnp.logical_and(j < nu_ref[0], jnp.logical_or(j == 0, be_ref[j] != prev)))
    def _():
        for copy in fetch(be_ref[j]):
            copy.wait()
        w1b_ref[...] = w1s_ref[...].astype(BF16)
        w2b_ref[...] = w2s_ref[...].astype(BF16)

        @pl.when(ne_ref[j] >= 0)
        def _():
            for copy in fetch(ne_ref[j]):
                copy.start()

    @pl.when(j < nu_ref[0])
    def _():
        half = x_ref.shape[1]
        x_hi, x_lo = _unpack_halves(x_ref[...])
        h = (jnp.dot(x_hi, w1b_ref[:half, :], preferred_element_type=F32)
             + jnp.dot(x_lo, w1b_ref[half:, :], preferred_element_type=F32)) + b1_ref[...]
        glu = jnp.minimum(h[:, :f], SWIGLU_LIMIT)
        lin = jnp.clip(h[:, f:], -SWIGLU_LIMIT, SWIGLU_LIMIT)
        act = glu * _sigmoid(SWIGLU_ALPHA * glu) * (lin + 1.0)
        y = jnp.dot(act.astype(BF16), w2b_ref[...], preferred_element_type=F32) + b2_ref[...]
        o_ref[...] = _pack_halves(y.astype(BF16).astype(F32))

    @pl.when(j >= nu_ref[0])
    def _():
        o_ref[...] = jnp.zeros_like(o_ref)


def _experts(xb, block_e, next_e, n_used, layer, w1, b1, w2, b2):
    depth, n_exp, d, f2 = w1.shape
    f = f2 // 2
    bm = EXPERT_BLOCK
    n_blocks = block_e.shape[0]
    last = lambda j, nu: jnp.minimum(j, nu[0] - 1)
    grid_spec = pltpu.PrefetchScalarGridSpec(
        num_scalar_prefetch=3,
        grid=(n_blocks,),
        in_specs=[
            pl.BlockSpec((bm, d // 2), lambda j, be, ne, nu: (last(j, nu), 0)),
            pl.BlockSpec(memory_space=pl.ANY),
            pl.BlockSpec((None, None, 1, f2), lambda j, be, ne, nu: (layer, be[last(j, nu)], 0, 0)),
            pl.BlockSpec(memory_space=pl.ANY),
            pl.BlockSpec((None, None, 1, d), lambda j, be, ne, nu: (layer, be[last(j, nu)], 0, 0)),
        ],
        out_specs=pl.BlockSpec((bm, d // 2), lambda j, be, ne, nu: (j, 0)),
        scratch_shapes=[pltpu.VMEM((d, f2), F32), pltpu.VMEM((f, d), F32),
                        pltpu.VMEM((d, f2), BF16), pltpu.VMEM((f, d), BF16), pltpu.SemaphoreType.DMA((2,))],
    )
    return pl.pallas_call(
        functools.partial(_expert_kernel, layer),
        grid_spec=grid_spec,
        out_shape=jax.ShapeDtypeStruct((n_blocks * bm, d // 2), jnp.uint32),
        compiler_params=_cparams("arbitrary"),
        name="moe_experts",
    )(block_e, next_e, n_used, xb, w1, b1.reshape(depth, n_exp, 1, f2), w2, b2.reshape(depth, n_exp, 1, d))


def _combine_kernel(alpha, tab_ref, next_tab_ref, idx_ref, rank_ref, off_ref, gate_ref, x1_ref, mod_ref,
                    g_ref, beta_ref, yb_ref, o_ref, rows_ref, sems):
    r = pl.program_id(0)
    tm = x1_ref.shape[0]
    rows = rows_ref.shape[1]
    slot = r % 2

    def tile_copy(t_ref, into, i):
        return pltpu.make_async_copy(
            yb_ref.at[pl.ds(pl.multiple_of(t_ref[i] * SUBLANES, SUBLANES), SUBLANES)],
            rows_ref.at[into, pl.ds(pl.multiple_of(i * SUBLANES, SUBLANES), SUBLANES)], sems.at[into])

    def gather(t_ref, into):
        _for_tiles(t_ref[TILE_TABLE - 1], lambda i: tile_copy(t_ref, into, i).start())

    @pl.when(r == 0)
    def _():
        gather(tab_ref, 0)

    @pl.when(r + 1 < pl.num_programs(0))
    def _():
        gather(next_tab_ref, 1 - slot)

    n_tiles = tab_ref[TILE_TABLE - 1]

    def many():
        return pltpu.make_async_copy(yb_ref.at[pl.ds(0, WAIT_TILES * SUBLANES)],
                                     rows_ref.at[slot, pl.ds(0, WAIT_TILES * SUBLANES)], sems.at[slot])

    _wait_tiles(n_tiles, many, lambda: tile_copy(tab_ref, slot, 0))

    pos = _local_pos(idx_ref, rank_ref, off_ref)
    lane = lax.broadcasted_iota(jnp.int32, (tm, rows), 1).astype(F32)
    gates = gate_ref[...]
    unperm = jnp.where(lane == pos[0], gates[:, 0:1], 0.0)
    for k in range(1, TOP_K):
        unperm = unperm + jnp.where(lane == pos[k], gates[:, k:k + 1], 0.0)
    row = lax.broadcasted_iota(jnp.int32, (rows, 1), 0)
    y_hi, y_lo = _unpack_halves(jnp.where(row < n_tiles * SUBLANES, rows_ref[slot], jnp.uint32(0)))
    unperm = unperm.astype(BF16)
    f = jnp.concatenate([jnp.dot(unperm, y_hi, preferred_element_type=F32),
                         jnp.dot(unperm, y_lo, preferred_element_type=F32)], axis=1)
    o_ref[...] = _layer_norm(alpha * x1_ref[...] + mod_ref[5:6, :] * f, g_ref[...], beta_ref[...])


def _combine(idx, rank, gates, x1, mod, ln_g, ln_b, yb, plan, alpha, nl, nc):
    t, d = x1.shape
    tm = ROW_BLOCK
    n_exp = plan["loc_off"].shape[2]
    row_spec = pl.BlockSpec((tm, d), lambda r: (r, 0))
    vec = pl.BlockSpec((1, d), lambda r: (0, 0))
    k_spec = pl.BlockSpec((tm, TOP_K), lambda r: (r, 0))
    return pl.pallas_call(
        functools.partial(_combine_kernel, alpha),
        grid=(t // tm,),
        in_specs=[
            pl.BlockSpec((TILE_TABLE,), lambda r: (r,), memory_space=pltpu.SMEM),
            pl.BlockSpec((TILE_TABLE,), lambda r: (jnp.minimum(r + 1, t // tm - 1),), memory_space=pltpu.SMEM),
            k_spec, k_spec,
            pl.BlockSpec((None, 1, n_exp), lambda r: (r, 0, 0)),
            k_spec,
            row_spec,
            pl.BlockSpec((None, 6, d), lambda r, f=_mod_row_map(nl, nc): (f(r), 0, 0)),
            vec, vec,
            pl.BlockSpec(memory_space=pl.ANY),
        ],
        out_specs=row_spec,
        out_shape=jax.ShapeDtypeStruct((t, d), F32),
        scratch_shapes=[pltpu.VMEM((2, _local_rows(n_exp), d // 2), jnp.uint32), pltpu.SemaphoreType.DMA((2,))],
        compiler_params=_cparams("arbitrary"),
        name="moe_combine_ln",
    )(plan["tile_table"], plan["tile_table"], idx, rank, plan["loc_off"], gates, x1, mod,
      ln_g.reshape(1, d), ln_b.reshape(1, d), yb)


def _moe_plan(cnt, n_blocks):
    nblk, n_exp = cnt.shape
    bm = EXPERT_BLOCK
    n_tiles_max = _local_rows(n_exp) // SUBLANES
    experts = jnp.arange(n_exp)
    blocks = jnp.arange(nblk)
    seg = (cnt + SUBLANES - 1) // SUBLANES * SUBLANES
    loc_off = jnp.sum(jnp.where(experts[None, None, :] < experts[None, :, None], seg[:, None, :], 0), axis=2)
    before = jnp.sum(jnp.where(blocks[None, :, None] < blocks[:, None, None], seg[None, :, :], 0), axis=1)
    rows_e = jnp.sum(seg, axis=0)
    padded = (rows_e + bm - 1) // bm * bm
    pend = jnp.sum(jnp.where(experts[None, :] <= experts[:, None], padded[None, :], 0), axis=1)
    pstart = pend - padded
    base = pstart[None, :] + before
    first_row = (jnp.arange(n_tiles_max) * SUBLANES)[None, :, None]
    e_of = jnp.sum(((loc_off + seg)[:, None, :] <= first_row).astype(jnp.int32), axis=2)
    pick = jnp.minimum(e_of, n_exp - 1)[:, :, None] == experts
    delta = jnp.sum(jnp.where(pick, (base - loc_off)[:, None, :], 0), axis=2)
    tile_of = (delta + first_row[:, :, 0]) // SUBLANES
    table = jnp.concatenate([tile_of.astype(jnp.int32),
                             jnp.zeros((nblk, TILE_TABLE - 1 - n_tiles_max), jnp.int32),
                             (jnp.sum(seg, axis=1, keepdims=True) // SUBLANES).astype(jnp.int32)], axis=1)
    block_e = jnp.sum((pend[None, :] <= (jnp.arange(n_blocks) * bm)[:, None]).astype(jnp.int32), axis=1)
    block_e = jnp.minimum(block_e, n_exp - 1)
    later = jnp.logical_and(experts[None, :] > experts[:, None], (padded > 0)[None, :])
    next_of = jnp.min(jnp.where(later, experts[None, :], n_exp), axis=1)
    next_e = jnp.sum(jnp.where(block_e[:, None] == experts, next_of[None, :], 0), axis=1)
    next_e = jnp.where(next_e >= n_exp, -1, next_e)
    return dict(
        next_e=next_e.astype(jnp.int32),
        loc_off=loc_off.astype(F32).reshape(nblk, 1, n_exp),
        tile_table=table.reshape(-1),
        tail_start=(pstart + rows_e).astype(jnp.int32),
        pad_len=(padded - rows_e).astype(jnp.int32),
        block_e=block_e.astype(jnp.int32),
        n_used=(pend[-1:] // bm).astype(jnp.int32),
    )


def kernel(x, c, ctx, c_ctx, w_mod, b_mod, ln1_g, ln1_b, ln2_g, ln2_b, attn_w_qkv, attn_w_o, attn_lam_q1, attn_lam_k1, attn_lam_q2, attn_lam_k2, attn_subln_g, fnet_w, fnet_b, conv_w_pw1, conv_b_pw1, conv_w_dw, conv_b_dw, conv_ln_g, conv_ln_b, conv_w_pw2, conv_b_pw2, moe_w_router, moe_b_router, moe_w1, moe_b1, moe_w2, moe_b2):
    b, n, d = x.shape
    cl = ctx.shape[1]
    s = n + cl
    t = b * s
    depth = w_mod.shape[0]
    n_exp = moe_w_router.shape[2]
    nl, nc = n // ROW_BLOCK, cl // ROW_BLOCK
    alpha = (2 * depth) ** 0.25

    def grouped_blocks(tokens):
        worst_rows = tokens * TOP_K + (tokens // ROW_BLOCK) * n_exp * (SUBLANES - 1)
        return -(-worst_rows // EXPERT_BLOCK) + n_exp

    xs = jnp.concatenate([x, ctx], axis=1).reshape(t, d)
    cond = jnp.stack([c, jnp.broadcast_to(c_ctx, c.shape)], axis=1).reshape(2 * b, d)
    mod = _modulation(cond, w_mod, b_mod)
    cos, sin = _rope_tables(n, cl)

    for i in range(depth):
        kind, j = i % 3, i // 3
        last = i == depth - 1
        if kind == 0:
            lam_init = 0.8 - 0.6 * float(np.exp(-0.3 * i))
            q, k, v = _qkv(xs, mod[i], attn_w_qkv[j], cos, sin, nl, nc)
            q3, k3 = q.reshape(b, s, d), k.reshape(b, s, d)
            vt = v.reshape(b, s, d // HEAD_W, HEAD_W).transpose(0, 2, 3, 1)
            vt = jnp.concatenate([vt, jnp.ones((b, d // HEAD_W, SUM_ROWS, s), BF16)], axis=2)
            lam_params = (attn_lam_q1[j], attn_lam_k1[j], attn_lam_q2[j], attn_lam_k2[j])
            a_lat = _attention(q3, k3, vt, lam_params, attn_subln_g[j], lam_init, n, cl, context=False)
            a_ctx = None if last else _attention(q3, k3, vt, lam_params, attn_subln_g[j], lam_init, n, cl,
                                                 context=True)
            w_out, b_out = attn_w_o[j], jnp.zeros((d,), F32)
        elif kind == 1:
            fa, fb = _chan_dft(xs, mod[i], nl, nc)
            fa3, fb3 = fa.reshape(b, s, d), fb.reshape(b, s, d)
            a_lat = _seq_dft(fa3, fb3, n, 0)
            a_ctx = None if last else _seq_dft(fa3, fb3, cl, n)
            w_out, b_out = fnet_w[j], fnet_b[j]
        else:
            hid = _glu(xs, mod[i], conv_w_pw1[j], conv_b_pw1[j], nl, nc)
            a_all = _dwconv_ln_silu(hid, conv_w_dw[j], conv_b_dw[j], conv_ln_g[j], conv_ln_b[j], nl, nc)
            a_all = a_all.reshape(b, s, d)
            a_lat, a_ctx = a_all[:, :n], a_all[:, n:]
            w_out, b_out = conv_w_pw2[j], conv_b_pw2[j]

        if last:
            a, tokens, nc_now = a_lat.reshape(b * n, d), b * n, 0
        else:
            a, tokens, nc_now = jnp.concatenate([a_lat, a_ctx], axis=1).reshape(t, d), t, nc
        n_blocks = grouped_blocks(tokens)
        x1, v, idx, gates, rank, cnt = _proj_route(
            a, w_out, b_out, xs, mod[i], ln1_g[i], ln1_b[i], moe_w_router[i], moe_b_router[i], alpha, nl, nc_now,
            x_skip=nc - nc_now)
        plan = _moe_plan(cnt[:, 0, :], n_blocks)
        xb = _dispatch(v, idx, rank, plan, n_blocks * EXPERT_BLOCK)
        yb = _experts(xb, plan["block_e"], plan["next_e"], plan["n_used"], i, moe_w1, moe_b1, moe_w2, moe_b2)
        xs = _combine(idx, rank, gates, x1, mod[i], ln2_g[i], ln2_b[i], yb, plan, alpha, nl, nc_now)

    return xs.reshape(b, n, d)
```

```python
import functools

import jax
import jax.numpy as jnp
import numpy as np
from jax import lax
from jax.experimental import pallas as pl
from jax.experimental.pallas import tpu as pltpu

F32 = jnp.float32
BF16 = jnp.bfloat16
HIGHEST = lax.Precision.HIGHEST

GRID_W = 64
HEAD_W = 128
MAP_W = HEAD_W // 2
ROPE_FREQS = MAP_W // 4
ROPE_BASE = 10000.0
CONV_WIDTH = 31
CONV_PAD = CONV_WIDTH // 2
TOP_K = 4
SWIGLU_ALPHA = 1.702
SWIGLU_LIMIT = 7.0
LN_EPS = 1e-5

LANES = 128
SUBLANES = 8
ROW_BLOCK = 256
SUM_ROWS = 8
HALO = 16
ATTN_TQ = 512
ATTN_TK = 512
ATTN_QC = 256
DFT_TILE = 1024
EXPERT_BLOCK = 512
TILE_TABLE = 1024
ISSUE_UNROLL = 8
WAIT_TILES = 16
VMEM_LIMIT = 56 * 1024 * 1024


def _cparams(*sem):
    return pltpu.CompilerParams(dimension_semantics=sem, vmem_limit_bytes=VMEM_LIMIT)


def _mod_row_map(nl, nc):
    rb = nl + nc

    def f(r):
        return 2 * (r // rb) + jnp.where((r % rb) >= nl, 1, 0)

    return f


def _layer_norm(z, g, b):
    mu = jnp.mean(z, axis=-1, keepdims=True)
    zc = z - mu
    var = jnp.mean(zc * zc, axis=-1, keepdims=True)
    return zc * lax.rsqrt(var + LN_EPS) * g + b


def _sigmoid(x):
    return 1.0 / (1.0 + jnp.exp(-x))


def _modulation_kernel(cond_ref, w_ref, b_ref, o_ref):
    cnd = cond_ref[...]
    s = cnd * _sigmoid(cnd)
    o_ref[...] = jnp.dot(s, w_ref[...], precision=HIGHEST, preferred_element_type=F32) + b_ref[...]


def _modulation(cond, w_mod, b_mod):
    depth, d, d6 = w_mod.shape
    rows = cond.shape[0]
    out = pl.pallas_call(
        _modulation_kernel,
        grid=(depth, d6 // d),
        in_specs=[
            pl.BlockSpec((rows, d), lambda i, j: (0, 0)),
            pl.BlockSpec((None, d, d), lambda i, j: (i, 0, j)),
            pl.BlockSpec((None, 1, d), lambda i, j: (i, 0, j)),
        ],
        out_specs=pl.BlockSpec((None, rows, d), lambda i, j: (i, 0, j)),
        out_shape=jax.ShapeDtypeStruct((depth, rows, d6), F32),
        compiler_params=_cparams("parallel", "parallel"),
        name="modulation",
    )(cond, w_mod, b_mod.reshape(depth, 1, d6))
    return out.reshape(depth, rows, d6 // d, d)


def _rope_tables(n, c):
    pos = jnp.arange(n)
    rows = (pos // GRID_W).astype(F32)
    cols = (pos % GRID_W).astype(F32)
    inv_freq = ROPE_BASE ** (-jnp.arange(ROPE_FREQS, dtype=F32) / ROPE_FREQS)
    lane = np.arange(HEAD_W)
    dim = lane % MAP_W
    freq = dim % ROPE_FREQS
    use_row = (dim // (MAP_W // 2)) == 0
    first_half = (dim % (MAP_W // 2)) < ROPE_FREQS
    ang = jnp.where(use_row[None, :], rows[:, None] * inv_freq[freq][None, :],
                    cols[:, None] * inv_freq[freq][None, :])
    cos = jnp.cos(ang)
    sin = jnp.sin(ang)
    sin = jnp.where(first_half[None, :], -sin, sin)
    cos = jnp.concatenate([cos, jnp.ones((c, HEAD_W), F32)], axis=0)
    sin = jnp.concatenate([sin, jnp.zeros((c, HEAD_W), F32)], axis=0)
    return cos, sin


def _qkv_kernel(x_ref, mod_ref, w_ref, cos_ref, sin_ref, q_ref, k_ref, vt_ref):
    d = x_ref.shape[1]
    u = (x_ref[...] * (1.0 + mod_ref[1:2, :]) + mod_ref[0:1, :]).astype(BF16)
    cos = cos_ref[...]
    sin = sin_ref[...]
    lane = lax.broadcasted_iota(jnp.int32, cos.shape, 1)
    first_half = (lane % (MAP_W // 2)) < ROPE_FREQS

    def rope(t):
        partner = jnp.where(first_half, pltpu.roll(t, HEAD_W - ROPE_FREQS, 1), pltpu.roll(t, ROPE_FREQS, 1))
        return t * cos + partner * sin

    q = jnp.dot(u, w_ref[:, :d], preferred_element_type=F32)
    k = jnp.dot(u, w_ref[:, d:2 * d], preferred_element_type=F32)
    for h in range(d // HEAD_W):
        cs = slice(h * HEAD_W, (h + 1) * HEAD_W)
        q_ref[:, cs] = rope(q[:, cs]).astype(BF16)
        k_ref[:, cs] = rope(k[:, cs]).astype(BF16)
    v = jnp.dot(u, w_ref[:, 2 * d:], preferred_element_type=F32)
    for h in range(d // HEAD_W):
        vt_ref[h, 0:HEAD_W, :] = v[:, h * HEAD_W:(h + 1) * HEAD_W].T.astype(BF16)
    vt_ref[:, HEAD_W:, :] = jnp.ones((d // HEAD_W, SUM_ROWS, x_ref.shape[0]), BF16)


def _qkv(x, mod, w_qkv, cos, sin, nl, nc):
    t, d = x.shape
    rb = nl + nc
    tm = ROW_BLOCK
    scale = jnp.concatenate([jnp.full((d,), np.log2(np.e) * MAP_W ** -0.5, F32), jnp.ones((2 * d,), F32)])
    w = (w_qkv * scale[None, :]).astype(BF16)
    heads = d // HEAD_W
    row_spec = pl.BlockSpec((tm, d), lambda r: (r, 0))
    tab_spec = pl.BlockSpec((tm, HEAD_W), lambda r: (r % rb, 0))
    return pl.pallas_call(
        _qkv_kernel,
        grid=(t // tm,),
        in_specs=[
            row_spec,
            pl.BlockSpec((None, 6, d), lambda r, f=_mod_row_map(nl, nc): (f(r), 0, 0)),
            pl.BlockSpec((d, 3 * d), lambda r: (0, 0)),
            tab_spec, tab_spec,
        ],
        out_specs=[row_spec, row_spec,
                   pl.BlockSpec((None, heads, HEAD_W + SUM_ROWS, tm), lambda r: (r // rb, 0, 0, r % rb))],
        out_shape=[jax.ShapeDtypeStruct((t, d), BF16)] * 2
        + [jax.ShapeDtypeStruct((t // (rb * tm), heads, HEAD_W + SUM_ROWS, rb * tm), BF16)],
        compiler_params=_cparams("parallel"),
        name="qkv_rope",
    )(x, mod, w, cos, sin)


def _attn_kernel(lam_init, tiles, q_ref, k_ref, vt_ref, lq1_ref, lk1_ref, lq2_ref, lk2_ref, g_ref,
                 o_ref, acc1_ref, acc2_ref, sa1_ref, sa2_ref, sb1_ref, sb2_ref):
    tq = q_ref.shape[0]
    q = q_ref[...].astype(F32)
    lane = lax.broadcasted_iota(jnp.int32, q.shape, 1)
    q1 = jnp.where(lane < MAP_W, q, 0.0).astype(BF16)
    q2 = jnp.where(lane >= MAP_W, q, 0.0).astype(BF16)
    acc1_ref[...] = jnp.zeros_like(acc1_ref)
    acc2_ref[...] = jnp.zeros_like(acc2_ref)
    acc_refs = (acc1_ref, acc2_ref)
    s_refs = ((sa1_ref, sa2_ref), (sb1_ref, sb2_ref))

    qc = min(tq, ATTN_QC)
    pieces = [(mp, c) for c in range(tq // qc) for mp in range(2)]
    qms = (q1, q2)

    def score_piece(i, mp, c):
        off, size = tiles[i]
        s = lax.dot_general(k_ref[off:off + size, :], qms[mp][c * qc:(c + 1) * qc, :],
                            (((1,), (1,)), ((), ())), preferred_element_type=F32)
        s_refs[i % 2][mp][0:size, c * qc:(c + 1) * qc] = s
        return jnp.max(s, axis=0, keepdims=True)

    def acc_piece(i, mp, c, cm, m):
        off, size = tiles[i]
        cs = slice(c * qc, (c + 1) * qc)
        m_new = jnp.maximum(m, cm)
        p = jnp.exp2(s_refs[i % 2][mp][0:size, cs] - m_new).astype(BF16)
        acc_refs[mp][:, cs] = (jnp.exp2(m - m_new) * acc_refs[mp][:, cs]
                               + jnp.dot(vt_ref[:, off:off + size], p, preferred_element_type=F32))
        return m_new

    def stage(i_score, i_acc, cm_acc, ms):
        cm_new, ms_new = [], []
        for n_piece, (mp, c) in enumerate(pieces):
            if i_score is not None:
                cm_new.append(score_piece(i_score, mp, c))
            if i_acc is not None:
                ms_new.append(acc_piece(i_acc, mp, c, cm_acc[n_piece], ms[n_piece]))
        return tuple(cm_new), tuple(ms_new) if i_acc is not None else ms

    ms = tuple(jnp.full((1, qc), -jnp.inf, F32) for _ in pieces)
    cm, _ = stage(0, None, None, ms)
    for i in range(len(tiles) - 1):
        cm, ms = stage(i + 1, i, cm, ms)
    stage(None, len(tiles) - 1, cm, ms)

    lam = (jnp.exp(jnp.sum(lq1_ref[...] * lk1_ref[...], axis=1, keepdims=True))
           - jnp.exp(jnp.sum(lq2_ref[...] * lk2_ref[...], axis=1, keepdims=True)) + lam_init)
    inv1 = 1.0 / acc1_ref[HEAD_W:HEAD_W + 1, :]
    inv2 = 1.0 / acc2_ref[HEAD_W:HEAD_W + 1, :]
    o = acc1_ref[0:HEAD_W, :] * inv1 - lam * (acc2_ref[0:HEAD_W, :] * inv2)
    ms = jnp.mean(o * o, axis=0, keepdims=True)
    o = o * lax.rsqrt(ms + LN_EPS) * (g_ref[...] * (1.0 - lam_init))
    o_ref[...] = o.T.astype(BF16)


def _attention(q3, k3, vt, lam_params, subln_g, lam_init, n, c, context):
    b, s, d = q3.shape
    h = d // HEAD_W
    if context:
        tq, kv_len, kv_blk, q_off, nq = c, c, n // c, n // c, 1
    else:
        tq, kv_len, kv_blk, q_off, nq = ATTN_TQ, s, 0, 0, n // ATTN_TQ
    tk = min(ATTN_TK, kv_len)
    tiles = tuple((off, min(tk, kv_len - off)) for off in range(0, kv_len, tk))
    lam_spec = pl.BlockSpec((1, MAP_W), lambda bi, hi, qi: (0, 0))
    return pl.pallas_call(
        functools.partial(_attn_kernel, lam_init, tiles),
        grid=(b, h, nq),
        in_specs=[
            pl.BlockSpec((None, tq, HEAD_W), lambda bi, hi, qi: (bi, qi + q_off, hi)),
            pl.BlockSpec((None, kv_len, HEAD_W), lambda bi, hi, qi: (bi, kv_blk, hi)),
            pl.BlockSpec((None, None, HEAD_W + SUM_ROWS, kv_len), lambda bi, hi, qi: (bi, hi, 0, kv_blk)),
            lam_spec, lam_spec, lam_spec, lam_spec,
            pl.BlockSpec((HEAD_W, 1), lambda bi, hi, qi: (0, 0)),
        ],
        out_specs=pl.BlockSpec((None, tq, HEAD_W), lambda bi, hi, qi: (bi, qi, hi)),
        out_shape=jax.ShapeDtypeStruct((b, tq * nq, d), BF16),
        scratch_shapes=[pltpu.VMEM((HEAD_W + SUM_ROWS, tq), F32)] * 2 + [pltpu.VMEM((tk, tq), F32)] * 4,
        compiler_params=_cparams("parallel", "parallel", "parallel"),
        name="attn_ctx" if context else "attn_lat",
    )(q3, k3, vt, *[p.reshape(1, MAP_W) for p in lam_params], subln_g.reshape(HEAD_W, 1))


def _dft_matrices(n, scale):
    blk = min(ROW_BLOCK, n)
    k = jnp.arange(n, dtype=jnp.int32)

    def cs(j):
        r = (j[:, None] * k[None, :]) % n
        ang = r.astype(F32) * (2.0 * np.pi / n)
        return jnp.cos(ang), jnp.sin(ang)

    c0, s0 = cs(jnp.arange(blk, dtype=jnp.int32))
    cj, sj = cs(jnp.arange(0, n, blk, dtype=jnp.int32))
    cm = cj[:, None, :] * c0[None] - sj[:, None, :] * s0[None]
    sm = sj[:, None, :] * c0[None] + cj[:, None, :] * s0[None]
    return (cm * scale).reshape(n, n).astype(BF16), (sm * scale).reshape(n, n).astype(BF16)


def _chan_dft_kernel(x_ref, mod_ref, cs_ref, a_ref, b_ref):
    d = x_ref.shape[1]
    u = (x_ref[...] * (1.0 + mod_ref[1:2, :]) + mod_ref[0:1, :]).astype(BF16)
    for g in range(d // LANES):
        sl = slice(g * LANES, (g + 1) * LANES)
        ab = jnp.dot(u[:, sl], cs_ref[...], preferred_element_type=F32)
        a_ref[:, sl] = ab[:, :LANES].astype(BF16)
        b_ref[:, sl] = ab[:, LANES:].astype(BF16)


def _chan_dft(x, mod, nl, nc):
    t, d = x.shape
    tm = ROW_BLOCK
    j = np.arange(LANES)
    ang = 2.0 * np.pi * ((j[:, None] * j[None, :]) % LANES) / LANES
    cs = jnp.asarray(np.concatenate([np.cos(ang), np.sin(ang)], axis=1) / np.sqrt(LANES), BF16)
    row_spec = pl.BlockSpec((tm, d), lambda r: (r, 0))
    return pl.pallas_call(
        _chan_dft_kernel,
        grid=(t // tm,),
        in_specs=[
            row_spec,
            pl.BlockSpec((None, 6, d), lambda r, f=_mod_row_map(nl, nc): (f(r), 0, 0)),
            pl.BlockSpec((LANES, 2 * LANES), lambda r: (0, 0)),
        ],
        out_specs=[row_spec, row_spec],
        out_shape=[jax.ShapeDtypeStruct((t, d), BF16)] * 2,
        compiler_params=_cparams("parallel"),
        name="chan_dft",
    )(x, mod, cs)


def _seq_dft_kernel(c_ref, s_ref, a_ref, b_ref, o_ref, acc_ref):
    kk = pl.program_id(2)

    @pl.when(kk == 0)
    def _():
        acc_ref[...] = jnp.zeros_like(acc_ref)

    acc_ref[...] += (jnp.dot(c_ref[...], a_ref[...], preferred_element_type=F32)
                     - jnp.dot(s_ref[...], b_ref[...], preferred_element_type=F32))

    @pl.when(kk == pl.num_programs(2) - 1)
    def _():
        o_ref[...] = acc_ref[...].astype(o_ref.dtype)


def _seq_dft(a3, b3, length, row_off):
    b, s, d = a3.shape
    tile = min(DFT_TILE, length)
    cm, sm = _dft_matrices(length, length ** -0.5)
    nt = length // tile
    off = row_off // tile
    mat_spec = pl.BlockSpec((tile, tile), lambda bi, m, k: (m, k))
    in_spec = pl.BlockSpec((None, tile, d), lambda bi, m, k: (bi, k + off, 0))
    return pl.pallas_call(
        _seq_dft_kernel,
        grid=(b, nt, nt),
        in_specs=[mat_spec, mat_spec, in_spec, in_spec],
        out_specs=pl.BlockSpec((None, tile, d), lambda bi, m, k: (bi, m, 0)),
        out_shape=jax.ShapeDtypeStruct((b, length, d), BF16),
        scratch_shapes=[pltpu.VMEM((tile, d), F32)],
        compiler_params=_cparams("parallel", "parallel", "arbitrary"),
        name="seq_dft",
    )(cm, sm, a3, b3)


def _glu_kernel(x_ref, mod_ref, w_ref, b_ref, o_ref):
    d = x_ref.shape[1]
    u = (x_ref[...] * (1.0 + mod_ref[1:2, :]) + mod_ref[0:1, :]).astype(BF16)
    a = jnp.dot(u, w_ref[:, :d], preferred_element_type=F32) + b_ref[:, :d]
    g = jnp.dot(u, w_ref[:, d:], preferred_element_type=F32) + b_ref[:, d:]
    o_ref[...] = a * _sigmoid(g)


def _glu(x, mod, w, bias, nl, nc):
    t, d = x.shape
    tm = ROW_BLOCK
    row_spec = pl.BlockSpec((tm, d), lambda r: (r, 0))
    return pl.pallas_call(
        _glu_kernel,
        grid=(t // tm,),
        in_specs=[
            row_spec,
            pl.BlockSpec((None, 6, d), lambda r, f=_mod_row_map(nl, nc): (f(r), 0, 0)),
            pl.BlockSpec((d, 2 * d), lambda r: (0, 0)),
            pl.BlockSpec((1, 2 * d), lambda r: (0, 0)),
        ],
        out_specs=row_spec,
        out_shape=jax.ShapeDtypeStruct((t, d), F32),
        compiler_params=_cparams("parallel"),
        name="pw1_glu",
    )(x, mod, w.astype(BF16), bias.reshape(1, 2 * d))


def _dwconv_kernel(nl, nc, prev_ref, cur_ref, next_ref, w_ref, b_ref, g_ref, beta_ref, o_ref, win_ref, h_ref):
    tm, d = cur_ref.shape
    rr = pl.program_id(0) % (nl + nc)
    has_prev = jnp.logical_and(rr != 0, rr != nl)
    has_next = jnp.logical_and(rr != nl - 1, rr != nl + nc - 1)
    win_ref[0, 0:HALO, :] = jnp.where(has_prev, prev_ref[...], 0.0)
    win_ref[0, HALO:HALO + tm, :] = cur_ref[...]
    win_ref[0, HALO + tm:, :] = jnp.where(has_next, next_ref[...], 0.0)
    span = tm + 2 * HALO - SUBLANES
    for s in range(1, SUBLANES):
        win_ref[s, 0:span, :] = win_ref[0, s:s + span, :]
    rows = 64
    for c in range(d // LANES):
        cs = slice(c * LANES, (c + 1) * LANES)
        wc = w_ref[:, cs]
        for r0 in range(0, tm, rows):
            acc = jnp.broadcast_to(b_ref[:, cs], (rows, LANES))
            for j in range(CONV_WIDTH):
                off = HALO - CONV_PAD + j
                start = r0 + off - off % SUBLANES
                acc = acc + wc[j:j + 1, :] * win_ref[off % SUBLANES, start:start + rows, cs]
            h_ref[r0:r0 + rows, cs] = acc
    y = _layer_norm(h_ref[...], g_ref[...], beta_ref[...])
    o_ref[...] = (y * _sigmoid(y)).astype(BF16)


def _dwconv_ln_silu(hid, w_dw, b_dw, ln_g, ln_b, nl, nc):
    t, d = hid.shape
    tm = ROW_BLOCK
    per = tm // HALO
    n_halo = t // HALO
    vec = pl.BlockSpec((1, d), lambda r: (0, 0))
    return pl.pallas_call(
        functools.partial(_dwconv_kernel, nl, nc),
        grid=(t // tm,),
        in_specs=[
            pl.BlockSpec((HALO, d), lambda r: (jnp.maximum(r * per - 1, 0), 0)),
            pl.BlockSpec((tm, d), lambda r: (r, 0)),
            pl.BlockSpec((HALO, d), lambda r: (jnp.minimum((r + 1) * per, n_halo - 1), 0)),
            pl.BlockSpec((CONV_WIDTH, d), lambda r: (0, 0)),
            vec, vec, vec,
        ],
        out_specs=pl.BlockSpec((tm, d), lambda r: (r, 0)),
        out_shape=jax.ShapeDtypeStruct((t, d), BF16),
        scratch_shapes=[pltpu.VMEM((SUBLANES, tm + 2 * HALO, d), F32), pltpu.VMEM((tm, d), F32)],
        compiler_params=_cparams("parallel"),
        name="dwconv_ln_silu",
    )(hid, hid, hid, w_dw, b_dw.reshape(1, d), ln_g.reshape(1, d), ln_b.reshape(1, d))


def _pack_cols(cols, dtype):
    rows = cols[0].shape[0]
    lane = lax.broadcasted_iota(jnp.int32, (rows, len(cols)), 1)
    out = jnp.zeros((rows, len(cols)), dtype)
    for i, col in enumerate(cols):
        out = jnp.where(lane == i, col.astype(dtype), out)
    return out


def _proj_route_kernel(alpha, a_ref, w_ref, bias_ref, x_ref, mod_ref, g_ref, beta_ref, wr_ref, br_ref,
                       x1_ref, v_ref, idx_ref, gate_ref, rank_ref, cnt_ref):
    tm = x_ref.shape[0]
    n_exp = wr_ref.shape[2]

    y = jnp.dot(a_ref[...], w_ref[...], preferred_element_type=F32) + bias_ref[...]
    x1 = _layer_norm(alpha * x_ref[...] + mod_ref[2:3, :] * y, g_ref[...], beta_ref[...])
    x1_ref[...] = x1
    v = x1 * (1.0 + mod_ref[4:5, :]) + mod_ref[3:4, :]
    v_ref[...] = v

    v_hi = v.astype(BF16)
    v_lo = (v - v_hi.astype(F32)).astype(BF16)
    logits = (jnp.dot(v_hi, wr_ref[0], preferred_element_type=F32)
              + jnp.dot(v_lo, wr_ref[0], preferred_element_type=F32)
              + jnp.dot(v_hi, wr_ref[1], preferred_element_type=F32)) + br_ref[...]
    col = lax.broadcasted_iota(jnp.int32, logits.shape, 1).astype(F32)
    work = logits
    vals, idxs, sels = [], [], []
    for _ in range(TOP_K):
        mx = jnp.max(work, axis=1, keepdims=True)
        first = jnp.min(jnp.where(work == mx, col, float(n_exp)), axis=1, keepdims=True)
        sel = col == first
        work = jnp.where(sel, -jnp.inf, work)
        vals.append(mx)
        idxs.append(first)
        sels.append(sel)
    es = [jnp.exp(val - vals[0]) for val in vals]
    inv = 1.0 / (es[0] + es[1] + es[2] + es[3])

    onehot = jnp.zeros(logits.shape, F32)
    for sel in sels:
        onehot = onehot + jnp.where(sel, 1.0, 0.0)
    ri = lax.broadcasted_iota(jnp.int32, (tm, tm), 0)
    ci = lax.broadcasted_iota(jnp.int32, (tm, tm), 1)
    tri = jnp.where(ci < ri, 1.0, 0.0).astype(BF16)
    before = jnp.dot(tri, onehot.astype(BF16), preferred_element_type=F32)
    ranks = [jnp.sum(jnp.where(sel, before, 0.0), axis=1, keepdims=True) for sel in sels]

    idx_ref[...] = _pack_cols(idxs, jnp.int32)
    gate_ref[...] = _pack_cols([e * inv for e in es], F32)
    rank_ref[...] = _pack_cols(ranks, jnp.int32)
    cnt_ref[...] = jnp.sum(onehot, axis=0, keepdims=True).astype(jnp.int32)


def _proj_route(a, w, bias, x, mod, ln_g, ln_b, w_router, b_router, alpha, nl, nc, x_skip=0):
    t, d = a.shape
    n_exp = w_router.shape[1]
    tm = ROW_BLOCK
    row_spec = pl.BlockSpec((tm, d), lambda r: (r, 0))
    x_spec = pl.BlockSpec((tm, d), lambda r: (r + (r // nl) * x_skip, 0))
    vec = pl.BlockSpec((1, d), lambda r: (0, 0))
    k_spec = pl.BlockSpec((tm, TOP_K), lambda r: (r, 0))
    wr_hi = w_router.astype(BF16)
    return pl.pallas_call(
        functools.partial(_proj_route_kernel, alpha),
        grid=(t // tm,),
        in_specs=[
            row_spec,
            pl.BlockSpec((d, d), lambda r: (0, 0)),
            vec,
            x_spec,
            pl.BlockSpec((None, 6, d), lambda r, f=_mod_row_map(nl, nc): (f(r), 0, 0)),
            vec, vec,
            pl.BlockSpec((2, d, n_exp), lambda r: (0, 0, 0)),
            pl.BlockSpec((1, n_exp), lambda r: (0, 0)),
        ],
        out_specs=[row_spec, row_spec, k_spec, k_spec, k_spec,
                   pl.BlockSpec((None, 1, n_exp), lambda r: (r, 0, 0))],
        out_shape=[
            jax.ShapeDtypeStruct((t, d), F32),
            jax.ShapeDtypeStruct((t, d), F32),
            jax.ShapeDtypeStruct((t, TOP_K), jnp.int32),
            jax.ShapeDtypeStruct((t, TOP_K), F32),
            jax.ShapeDtypeStruct((t, TOP_K), jnp.int32),
            jax.ShapeDtypeStruct((t // tm, 1, n_exp), jnp.int32),
        ],
        compiler_params=_cparams("parallel"),
        name="proj_ln_route",
    )(a, w.astype(BF16), bias.reshape(1, d), x, mod, ln_g.reshape(1, d), ln_b.reshape(1, d),
      jnp.stack([wr_hi, (w_router - wr_hi.astype(F32)).astype(BF16)]), b_router.reshape(1, n_exp))


def _local_rows(n_exp):
    return TOP_K * ROW_BLOCK + n_exp * SUBLANES


def _for_tiles(n_tiles, body):
    def trip(g, carry):
        for u in range(ISSUE_UNROLL):
            body(g * ISSUE_UNROLL + u)
        return carry

    def single(i, carry):
        body(i)
        return carry

    full = n_tiles // ISSUE_UNROLL
    lax.fori_loop(0, full, trip, 0)
    lax.fori_loop(full * ISSUE_UNROLL, n_tiles, single, 0)


def _wait_tiles(n_tiles, many, one):
    def wait_many(g, carry):
        many().wait()
        return carry

    def wait_one(i, carry):
        one().wait()
        return carry

    lax.fori_loop(0, n_tiles // WAIT_TILES, wait_many, 0)
    lax.fori_loop(0, n_tiles % WAIT_TILES, wait_one, 0)


def _pack_halves(x):
    half = x.shape[1] // 2
    hi = lax.bitcast_convert_type(x[:, :half], jnp.uint32)
    lo = lax.bitcast_convert_type(x[:, half:], jnp.uint32)
    return (hi & jnp.uint32(0xFFFF0000)) | (lo >> 16)


def _unpack_halves(w):
    hi = lax.bitcast_convert_type(w & jnp.uint32(0xFFFF0000), F32)
    lo = lax.bitcast_convert_type(w << 16, F32)
    return hi.astype(BF16), lo.astype(BF16)


def _local_pos(idx_ref, rank_ref, off_ref):
    tm = idx_ref.shape[0]
    col = lax.broadcasted_iota(jnp.int32, (tm, off_ref.shape[1]), 1)
    idx = idx_ref[...]
    rank = rank_ref[...]
    pos = []
    for k in range(TOP_K):
        start = jnp.sum(jnp.where(col == idx[:, k:k + 1], off_ref[...], 0.0), axis=1, keepdims=True)
        pos.append(start + rank[:, k:k + 1].astype(F32))
    return pos


def _dispatch_kernel(tail_ref, pad_ref, nu_ref, tab_ref, prev_tab_ref, v_ref, idx_ref, rank_ref, off_ref,
                     xb_ref, srt_ref, zero_ref, sems, fill_sem):
    r = pl.program_id(0)
    slot = r % 2
    tm = v_ref.shape[0]
    rows = srt_ref.shape[1]
    bm = zero_ref.shape[0]
    n_blocks = xb_ref.shape[0] // bm

    @pl.when(pl.program_id(0) == 0)
    def _():
        zero_ref[...] = jnp.zeros_like(zero_ref)

        def padding(do):
            for e in range(tail_ref.shape[0]):
                pad = pad_ref[e]
                for bit in reversed(range(SUBLANES.bit_length() - 1, bm.bit_length() - 1)):
                    size = 1 << bit
                    off = pl.multiple_of(tail_ref[e] + ((pad >> (bit + 1)) << (bit + 1)), SUBLANES)
                    copy = pltpu.make_async_copy(zero_ref.at[pl.ds(0, size)], xb_ref.at[pl.ds(off, size)], fill_sem)
                    pl.when((pad & size) != 0)(functools.partial(do, copy))

        def blocks(do):
            def body(j, carry):
                do(pltpu.make_async_copy(zero_ref, xb_ref.at[pl.ds(pl.multiple_of(j * bm, bm), bm)], fill_sem))
                return carry
            lax.fori_loop(nu_ref[0], n_blocks, body, 0)

        padding(lambda copy: copy.start())
        blocks(lambda copy: copy.start())
        padding(lambda copy: copy.wait())
        blocks(lambda copy: copy.wait())

    pos = _local_pos(idx_ref, rank_ref, off_ref)
    lane = lax.broadcasted_iota(jnp.int32, (tm, LANES), 1)
    pos_cols = jnp.zeros((tm, LANES), F32)
    for k in range(TOP_K):
        pos_cols = jnp.where(lane == k, pos[k], pos_cols)
    pos_rows = pos_cols.T
    row = lax.broadcasted_iota(jnp.int32, (rows, tm), 0).astype(F32)
    hit = row == pos_rows[0:1, :]
    for k in range(1, TOP_K):
        hit = jnp.logical_or(hit, row == pos_rows[k:k + 1, :])
    perm = jnp.where(hit, 1.0, 0.0).astype(BF16)
    srt_ref[slot] = _pack_halves(jnp.dot(perm, v_ref[...].astype(BF16), preferred_element_type=F32))

    def tile_copy(t_ref, buf, i):
        return pltpu.make_async_copy(
            srt_ref.at[buf, pl.ds(pl.multiple_of(i * SUBLANES, SUBLANES), SUBLANES)],
            xb_ref.at[pl.ds(pl.multiple_of(t_ref[i] * SUBLANES, SUBLANES), SUBLANES)], sems.at[buf])

    def drain(t_ref, buf):
        def many():
            return pltpu.make_async_copy(srt_ref.at[buf, pl.ds(0, WAIT_TILES * SUBLANES)],
                                         xb_ref.at[pl.ds(0, WAIT_TILES * SUBLANES)], sems.at[buf])
        _wait_tiles(t_ref[TILE_TABLE - 1], many, lambda: tile_copy(t_ref, buf, 0))

    _for_tiles(tab_ref[TILE_TABLE - 1], lambda i: tile_copy(tab_ref, slot, i).start())

    @pl.when(r > 0)
    def _():
        drain(prev_tab_ref, 1 - slot)

    @pl.when(r == pl.num_programs(0) - 1)
    def _():
        drain(tab_ref, slot)


def _dispatch(v, idx, rank, plan, p_rows):
    t, d = v.shape
    tm = ROW_BLOCK
    n_exp = plan["loc_off"].shape[2]
    k_spec = pl.BlockSpec((tm, TOP_K), lambda r, *_: (r, 0))
    grid_spec = pltpu.PrefetchScalarGridSpec(
        num_scalar_prefetch=3,
        grid=(t // tm,),
        in_specs=[
            pl.BlockSpec((TILE_TABLE,), lambda r, *_: (r,), memory_space=pltpu.SMEM),
            pl.BlockSpec((TILE_TABLE,), lambda r, *_: (jnp.maximum(r - 1, 0),), memory_space=pltpu.SMEM),
            pl.BlockSpec((tm, d), lambda r, *_: (r, 0)),
            k_spec, k_spec,
            pl.BlockSpec((None, 1, n_exp), lambda r, *_: (r, 0, 0)),
        ],
        out_specs=pl.BlockSpec(memory_space=pl.ANY),
        scratch_shapes=[pltpu.VMEM((2, _local_rows(n_exp), d // 2), jnp.uint32),
                        pltpu.VMEM((EXPERT_BLOCK, d // 2), jnp.uint32),
                        pltpu.SemaphoreType.DMA((2,)), pltpu.SemaphoreType.DMA],
    )
    return pl.pallas_call(
        _dispatch_kernel,
        grid_spec=grid_spec,
        out_shape=jax.ShapeDtypeStruct((p_rows, d // 2), jnp.uint32),
        compiler_params=_cparams("arbitrary"),
        name="moe_dispatch",
    )(plan["tail_start"], plan["pad_len"], plan["n_used"], plan["tile_table"], plan["tile_table"],
      v, idx, rank, plan["loc_off"])


def _expert_kernel(layer, be_ref, ne_ref, nu_ref, x_ref, w1_hbm, b1_ref, w2_hbm, b2_ref, o_ref,
                   w1s_ref, w2s_ref, w1b_ref, w2b_ref, sems):
    j = pl.program_id(0)
    f = w2b_ref.shape[0]
    prev = be_ref[jnp.maximum(j - 1, 0)]

    def fetch(e):
        return (pltpu.make_async_copy(w1_hbm.at[layer, e], w1s_ref, sems.at[0]),
                pltpu.make_async_copy(w2_hbm.at[layer, e], w2s_ref, sems.at[1]))

    @pl.when(j == 0)
    def _():
        for copy in fetch(be_ref[0]):
            copy.start()

    @pl.when(jnp.logical_and(j < nu_ref[0], jnp.logical_or(j == 0, be_ref[j] != prev)))
    def _():
        for copy in fetch(be_ref[j]):
            copy.wait()
        w1b_ref[...] = w1s_ref[...].astype(BF16)
        w2b_ref[...] = w2s_ref[...].astype(BF16)

        @pl.when(ne_ref[j] >= 0)
        def _():
            for copy in fetch(ne_ref[j]):
                copy.start()

    @pl.when(j < nu_ref[0])
    def _():
        half = x_ref.shape[1]
        x_hi, x_lo = _unpack_halves(x_ref[...])
        h = (jnp.dot(x_hi, w1b_ref[:half, :], preferred_element_type=F32)
             + jnp.dot(x_lo, w1b_ref[half:, :], preferred_element_type=F32)) + b1_ref[...]
        glu = jnp.minimum(h[:, :f], SWIGLU_LIMIT)
        lin = jnp.clip(h[:, f:], -SWIGLU_LIMIT, SWIGLU_LIMIT)
        act = glu * _sigmoid(SWIGLU_ALPHA * glu) * (lin + 1.0)
        y = jnp.dot(act.astype(BF16), w2b_ref[...], preferred_element_type=F32) + b2_ref[...]
        o_ref[...] = _pack_halves(y.astype(BF16).astype(F32))

    @pl.when(j >= nu_ref[0])
    def _():
        o_ref[...] = jnp.zeros_like(o_ref)


def _experts(xb, block_e, next_e, n_used, layer, w1, b1, w2, b2):
    depth, n_exp, d, f2 = w1.shape
    f = f2 // 2
    bm = EXPERT_BLOCK
    n_blocks = block_e.shape[0]
    last = lambda j, nu: jnp.minimum(j, nu[0] - 1)
    grid_spec = pltpu.PrefetchScalarGridSpec(
        num_scalar_prefetch=3,
        grid=(n_blocks,),
        in_specs=[
            pl.BlockSpec((bm, d // 2), lambda j, be, ne, nu: (last(j, nu), 0)),
            pl.BlockSpec(memory_space=pl.ANY),
            pl.BlockSpec((None, None, 1, f2), lambda j, be, ne, nu: (layer, be[last(j, nu)], 0, 0)),
            pl.BlockSpec(memory_space=pl.ANY),
            pl.BlockSpec((None, None, 1, d), lambda j, be, ne, nu: (layer, be[last(j, nu)], 0, 0)),
        ],
        out_specs=pl.BlockSpec((bm, d // 2), lambda j, be, ne, nu: (j, 0)),
        scratch_shapes=[pltpu.VMEM((d, f2), F32), pltpu.VMEM((f, d), F32),
                        pltpu.VMEM((d, f2), BF16), pltpu.VMEM((f, d), BF16), pltpu.SemaphoreType.DMA((2,))],
    )
    return pl.pallas_call(
        functools.partial(_expert_kernel, layer),
        grid_spec=grid_spec,
        out_shape=jax.ShapeDtypeStruct((n_blocks * bm, d // 2), jnp.uint32),
        compiler_params=_cparams("arbitrary"),
        name="moe_experts",
    )(block_e, next_e, n_used, xb, w1, b1.reshape(depth, n_exp, 1, f2), w2, b2.reshape(depth, n_exp, 1, d))


def _combine_kernel(alpha, tab_ref, next_tab_ref, idx_ref, rank_ref, off_ref, gate_ref, x1_ref, mod_ref,
                    g_ref, beta_ref, yb_ref, o_ref, rows_ref, sems):
    r = pl.program_id(0)
    tm = x1_ref.shape[0]
    rows = rows_ref.shape[1]
    slot = r % 2

    def tile_copy(t_ref, into, i):
        return pltpu.make_async_copy(
            yb_ref.at[pl.ds(pl.multiple_of(t_ref[i] * SUBLANES, SUBLANES), SUBLANES)],
            rows_ref.at[into, pl.ds(pl.multiple_of(i * SUBLANES, SUBLANES), SUBLANES)], sems.at[into])

    def gather(t_ref, into):
        _for_tiles(t_ref[TILE_TABLE - 1], lambda i: tile_copy(t_ref, into, i).start())

    @pl.when(r == 0)
    def _():
        gather(tab_ref, 0)

    @pl.when(r + 1 < pl.num_programs(0))
    def _():
        gather(next_tab_ref, 1 - slot)

    n_tiles = tab_ref[TILE_TABLE - 1]

    def many():
        return pltpu.make_async_copy(yb_ref.at[pl.ds(0, WAIT_TILES * SUBLANES)],
                                     rows_ref.at[slot, pl.ds(0, WAIT_TILES * SUBLANES)], sems.at[slot])

    _wait_tiles(n_tiles, many, lambda: tile_copy(tab_ref, slot, 0))

    pos = _local_pos(idx_ref, rank_ref, off_ref)
    lane = lax.broadcasted_iota(jnp.int32, (tm, rows), 1).astype(F32)
    gates = gate_ref[...]
    unperm = jnp.where(lane == pos[0], gates[:, 0:1], 0.0)
    for k in range(1, TOP_K):
        unperm = unperm + jnp.where(lane == pos[k], gates[:, k:k + 1], 0.0)
    row = lax.broadcasted_iota(jnp.int32, (rows, 1), 0)
    y_hi, y_lo = _unpack_halves(jnp.where(row < n_tiles * SUBLANES, rows_ref[slot], jnp.uint32(0)))
    unperm = unperm.astype(BF16)
    f = jnp.concatenate([jnp.dot(unperm, y_hi, preferred_element_type=F32),
                         jnp.dot(unperm, y_lo, preferred_element_type=F32)], axis=1)
    o_ref[...] = _layer_norm(alpha * x1_ref[...] + mod_ref[5:6, :] * f, g_ref[...], beta_ref[...])


def _combine(idx, rank, gates, x1, mod, ln_g, ln_b, yb, plan, alpha, nl, nc):
    t, d = x1.shape
    tm = ROW_BLOCK
    n_exp = plan["loc_off"].shape[2]
    row_spec = pl.BlockSpec((tm, d), lambda r: (r, 0))
    vec = pl.BlockSpec((1, d), lambda r: (0, 0))
    k_spec = pl.BlockSpec((tm, TOP_K), lambda r: (r, 0))
    return pl.pallas_call(
        functools.partial(_combine_kernel, alpha),
        grid=(t // tm,),
        in_specs=[
            pl.BlockSpec((TILE_TABLE,), lambda r: (r,), memory_space=pltpu.SMEM),
            pl.BlockSpec((TILE_TABLE,), lambda r: (jnp.minimum(r + 1, t // tm - 1),), memory_space=pltpu.SMEM),
            k_spec, k_spec,
            pl.BlockSpec((None, 1, n_exp), lambda r: (r, 0, 0)),
            k_spec,
            row_spec,
            pl.BlockSpec((None, 6, d), lambda r, f=_mod_row_map(nl, nc): (f(r), 0, 0)),
            vec, vec,
            pl.BlockSpec(memory_space=pl.ANY),
        ],
        out_specs=row_spec,
        out_shape=jax.ShapeDtypeStruct((t, d), F32),
        scratch_shapes=[pltpu.VMEM((2, _local_rows(n_exp), d // 2), jnp.uint32), pltpu.SemaphoreType.DMA((2,))],
        compiler_params=_cparams("arbitrary"),
        name="moe_combine_ln",
    )(plan["tile_table"], plan["tile_table"], idx, rank, plan["loc_off"], gates, x1, mod,
      ln_g.reshape(1, d), ln_b.reshape(1, d), yb)


def _moe_plan(cnt, n_blocks):
    nblk, n_exp = cnt.shape
    bm = EXPERT_BLOCK
    n_tiles_max = _local_rows(n_exp) // SUBLANES
    experts = jnp.arange(n_exp)
    blocks = jnp.arange(nblk)
    seg = (cnt + SUBLANES - 1) // SUBLANES * SUBLANES
    loc_off = jnp.sum(jnp.where(experts[None, None, :] < experts[None, :, None], seg[:, None, :], 0), axis=2)
    before = jnp.sum(jnp.where(blocks[None, :, None] < blocks[:, None, None], seg[None, :, :], 0), axis=1)
    rows_e = jnp.sum(seg, axis=0)
    padded = (rows_e + bm - 1) // bm * bm
    pend = jnp.sum(jnp.where(experts[None, :] <= experts[:, None], padded[None, :], 0), axis=1)
    pstart = pend - padded
    base = pstart[None, :] + before
    first_row = (jnp.arange(n_tiles_max) * SUBLANES)[None, :, None]
    e_of = jnp.sum(((loc_off + seg)[:, None, :] <= first_row).astype(jnp.int32), axis=2)
    pick = jnp.minimum(e_of, n_exp - 1)[:, :, None] == experts
    delta = jnp.sum(jnp.where(pick, (base - loc_off)[:, None, :], 0), axis=2)
    tile_of = (delta + first_row[:, :, 0]) // SUBLANES
    table = jnp.concatenate([tile_of.astype(jnp.int32),
                             jnp.zeros((nblk, TILE_TABLE - 1 - n_tiles_max), jnp.int32),
                             (jnp.sum(seg, axis=1, keepdims=True) // SUBLANES).astype(jnp.int32)], axis=1)
    block_e = jnp.sum((pend[None, :] <= (jnp.arange(n_blocks) * bm)[:, None]).astype(jnp.int32), axis=1)
    block_e = jnp.minimum(block_e, n_exp - 1)
    later = jnp.logical_and(experts[None, :] > experts[:, None], (padded > 0)[None, :])
    next_of = jnp.min(jnp.where(later, experts[None, :], n_exp), axis=1)
    next_e = jnp.sum(jnp.where(block_e[:, None] == experts, next_of[None, :], 0), axis=1)
    next_e = jnp.where(next_e >= n_exp, -1, next_e)
    return dict(
        next_e=next_e.astype(jnp.int32),
        loc_off=loc_off.astype(F32).reshape(nblk, 1, n_exp),
        tile_table=table.reshape(-1),
        tail_start=(pstart + rows_e).astype(jnp.int32),
        pad_len=(padded - rows_e).astype(jnp.int32),
        block_e=block_e.astype(jnp.int32),
        n_used=(pend[-1:] // bm).astype(jnp.int32),
    )


def kernel(x, c, ctx, c_ctx, w_mod, b_mod, ln1_g, ln1_b, ln2_g, ln2_b, attn_w_qkv, attn_w_o, attn_lam_q1, attn_lam_k1, attn_lam_q2, attn_lam_k2, attn_subln_g, fnet_w, fnet_b, conv_w_pw1, conv_b_pw1, conv_w_dw, conv_b_dw, conv_ln_g, conv_ln_b, conv_w_pw2, conv_b_pw2, moe_w_router, moe_b_router, moe_w1, moe_b1, moe_w2, moe_b2):
    b, n, d = x.shape
    cl = ctx.shape[1]
    s = n + cl
    t = b * s
    depth = w_mod.shape[0]
    n_exp = moe_w_router.shape[2]
    nl, nc = n // ROW_BLOCK, cl // ROW_BLOCK
    alpha = (2 * depth) ** 0.25

    def grouped_blocks(tokens):
        worst_rows = tokens * TOP_K + (tokens // ROW_BLOCK) * n_exp * (SUBLANES - 1)
        return -(-worst_rows // EXPERT_BLOCK) + n_exp

    xs = jnp.concatenate([x, ctx], axis=1).reshape(t, d)
    cond = jnp.stack([c, jnp.broadcast_to(c_ctx, c.shape)], axis=1).reshape(2 * b, d)
    mod = _modulation(cond, w_mod, b_mod)
    cos, sin = _rope_tables(n, cl)

    for i in range(depth):
        kind, j = i % 3, i // 3
        last = i == depth - 1
        if kind == 0:
            lam_init = 0.8 - 0.6 * float(np.exp(-0.3 * i))
            q, k, vt = _qkv(xs, mod[i], attn_w_qkv[j], cos, sin, nl, nc)
            q3, k3 = q.reshape(b, s, d), k.reshape(b, s, d)
            lam_params = (attn_lam_q1[j], attn_lam_k1[j], attn_lam_q2[j], attn_lam_k2[j])
            a_lat = _attention(q3, k3, vt, lam_params, attn_subln_g[j], lam_init, n, cl, context=False)
            a_ctx = None if last else _attention(q3, k3, vt, lam_params, attn_subln_g[j], lam_init, n, cl,
                                                 context=True)
            w_out, b_out = attn_w_o[j], jnp.zeros((d,), F32)
        elif kind == 1:
            fa, fb = _chan_dft(xs, mod[i], nl, nc)
            fa3, fb3 = fa.reshape(b, s, d), fb.reshape(b, s, d)
            a_lat = _seq_dft(fa3, fb3, n, 0)
            a_ctx = None if last else _seq_dft(fa3, fb3, cl, n)
            w_out, b_out = fnet_w[j], fnet_b[j]
        else:
            hid = _glu(xs, mod[i], conv_w_pw1[j], conv_b_pw1[j], nl, nc)
            a_all = _dwconv_ln_silu(hid, conv_w_dw[j], conv_b_dw[j], conv_ln_g[j], conv_ln_b[j], nl, nc)
            a_all = a_all.reshape(b, s, d)
            a_lat, a_ctx = a_all[:, :n], a_all[:, n:]
            w_out, b_out = conv_w_pw2[j], conv_b_pw2[j]

        if last:
            a, tokens, nc_now = a_lat.reshape(b * n, d), b * n, 0
        else:
            a, tokens, nc_now = jnp.concatenate([a_lat, a_ctx], axis=1).reshape(t, d), t, nc
        n_blocks = grouped_blocks(tokens)
        x1, v, idx, gates, rank, cnt = _proj_route(
            a, w_out, b_out, xs, mod[i], ln1_g[i], ln1_b[i], moe_w_router[i], moe_b_router[i], alpha, nl, nc_now,
            x_skip=nc - nc_now)
        plan = _moe_plan(cnt[:, 0, :], n_blocks)
        xb = _dispatch(v, idx, rank, plan, n_blocks * EXPERT_BLOCK)
        yb = _experts(xb, plan["block_e"], plan["next_e"], plan["n_used"], i, moe_w1, moe_b1, moe_w2, moe_b2)
        xs = _combine(idx, rank, gates, x1, mod[i], ln2_g[i], ln2_b[i], yb, plan, alpha, nl, nc_now)

    return xs.reshape(b, n, d)
```
